```python
import jax, jax.numpy as jnp
from jax import lax
import numpy as np

D_MODEL = 1024
BATCH = 1
SEQ = 16384
DEPTH = 1

PLE_DIM = 256
GLA_HEADS = 4
GLA_DK = 64
GLA_DV = 128
GLA_LOWRANK = 16
GLA_TAU = 16.0
GLA_CHUNK = 64
MOBA_HEADS = 4
MOBA_HD = 128
MOBA_BLOCK = 256
MOBA_TOPK = 3
MOBA_QBLOCK = 128
ROPE_THETA = 10000.0
D_FF = 4 * D_MODEL
EPS = 1e-6

GLA_QK_W = GLA_HEADS * GLA_DK
GLA_V_W = GLA_HEADS * GLA_DV
MOBA_W = MOBA_HEADS * MOBA_HD
MIX_W = GLA_V_W + MOBA_W
SPLIT_SIZES = (GLA_QK_W, GLA_QK_W, GLA_V_W, GLA_V_W, GLA_LOWRANK, MOBA_W, MOBA_W, MOBA_W)
IN_W = 2 * GLA_QK_W + 2 * GLA_V_W + GLA_LOWRANK + 3 * MOBA_W

kernel_name = "hymba_gla_moba_sqrelu_ple"


def rms_norm(x, g):
    xf = x.astype(jnp.float32)
    y = xf * lax.rsqrt(jnp.mean(xf * xf, axis=-1, keepdims=True) + EPS)
    return (y * g.astype(jnp.float32)).astype(x.dtype)


def rope(x, positions):
    hd = x.shape[-1]
    half = hd // 2
    inv = 1.0 / (ROPE_THETA ** (jnp.arange(half, dtype=jnp.float32) / half))
    ang = positions.astype(jnp.float32)[:, None, :, None] * inv
    cos, sin = jnp.cos(ang), jnp.sin(ang)
    xf = x.astype(jnp.float32)
    x1, x2 = xf[..., :half], xf[..., half:]
    out = jnp.concatenate([x1 * cos - x2 * sin, x2 * cos + x1 * sin], axis=-1)
    return out.astype(x.dtype)


def gla_chunked(q, k, v, log_a):
    B, H, S, dk = q.shape
    dv = v.shape[-1]
    C = GLA_CHUNK
    nc = S // C
    q = q.reshape(B, H, nc, C, dk)
    k = k.reshape(B, H, nc, C, dk)
    v = v.reshape(B, H, nc, C, dv)
    g = log_a.reshape(B, H, nc, C, dk)
    b = jnp.cumsum(g, axis=3)
    b_last = b[:, :, :, -1:, :]
    q_dec = q * jnp.exp(b) * (dk ** -0.5)
    k_inv = k * jnp.exp(-b)
    causal = jnp.tril(jnp.ones((C, C), dtype=bool))
    A = jnp.einsum('bhncd,bhnjd->bhncj', q_dec, k_inv)
    A = jnp.where(causal, A, 0.0)
    o_intra = jnp.einsum('bhncj,bhnje->bhnce', A, v)
    k_to_end = k * jnp.exp(b_last - b)
    dS = jnp.einsum('bhncd,bhnce->bhnde', k_to_end, v)
    decay = jnp.exp(b_last[:, :, :, 0, :])

    def step(state, inp):
        ds_n, dec_n = inp
        return dec_n[..., None] * state + ds_n, state

    _, s_prev = lax.scan(step, jnp.zeros((B, H, dk, dv), jnp.float32),
                         (jnp.moveaxis(dS, 2, 0), jnp.moveaxis(decay, 2, 0)))
    s_prev = jnp.moveaxis(s_prev, 0, 2)
    o_inter = jnp.einsum('bhncd,bhnde->bhnce', q_dec, s_prev)
    return (o_intra + o_inter).reshape(B, H, S, dv)


def moba_attention(q, k, v):
    B, H, S, hd = q.shape
    BLK = MOBA_BLOCK
    QB = MOBA_QBLOCK
    nb = -(-S // BLK)
    pad = nb * BLK - S
    k_p = jnp.pad(k, ((0, 0), (0, 0), (0, pad), (0, 0)))
    v_p = jnp.pad(v, ((0, 0), (0, 0), (0, pad), (0, 0)))
    k_blocks = k_p.reshape(B, H, nb, BLK, hd)
    v_blocks = v_p.reshape(B, H, nb, BLK, hd)
    k_mean = jnp.mean(k_blocks.astype(jnp.float32), axis=3)
    n_sel = min(MOBA_TOPK, nb)
    nqb = S // QB
    q_blocks = q.reshape(B, H, nqb, QB, hd).transpose(2, 0, 1, 3, 4)
    scale = hd ** -0.5
    bi = jnp.arange(B)[:, None, None, None]
    hi = jnp.arange(H)[None, :, None, None]
    blk_ids = jnp.arange(nb)

    def one_block(args):
        qb, i = args
        q_start = i * QB
        kb = q_start // BLK
        qf = qb.astype(jnp.float32)
        gate = jnp.einsum('bhqd,bhnd->bhqn', qf, k_mean)
        gate = jnp.where(blk_ids < kb, gate, -jnp.inf)
        _, idx = lax.top_k(gate, n_sel)
        sel_valid = idx < kb
        k_sel = k_blocks[bi, hi, idx].astype(jnp.float32)
        v_sel = v_blocks[bi, hi, idx].astype(jnp.float32)
        s_sel = jnp.einsum('bhqd,bhqnkd->bhqnk', qf, k_sel) * scale
        s_sel = jnp.where(sel_valid[..., None], s_sel, -jnp.inf).reshape(B, H, QB, n_sel * BLK)
        k_own = lax.dynamic_index_in_dim(k_blocks, kb, axis=2, keepdims=False).astype(jnp.float32)
        v_own = lax.dynamic_index_in_dim(v_blocks, kb, axis=2, keepdims=False).astype(jnp.float32)
        s_own = jnp.einsum('bhqd,bhkd->bhqk', qf, k_own) * scale
        q_pos = q_start + jnp.arange(QB)
        k_pos = kb * BLK + jnp.arange(BLK)
        s_own = jnp.where(k_pos[None, :] <= q_pos[:, None], s_own, -jnp.inf)
        probs = jax.nn.softmax(jnp.concatenate([s_sel, s_own], axis=-1), axis=-1)
        p_sel = probs[..., :n_sel * BLK].reshape(B, H, QB, n_sel, BLK)
        p_own = probs[..., n_sel * BLK:]
        o = (jnp.einsum('bhqnk,bhqnkd->bhqd', p_sel, v_sel)
             + jnp.einsum('bhqk,bhkd->bhqd', p_own, v_own))
        return o.astype(q.dtype)

    out = lax.map(one_block, (q_blocks, jnp.arange(nqb)))
    return out.transpose(1, 2, 0, 3, 4).reshape(B, H, S, hd)


def setup_inputs(seed: int = 0) -> dict:
    key = jax.random.key(seed)
    ks = jax.random.split(key, 20)
    f32 = jnp.float32

    def nrm(k, shape, scale):
        return jax.random.normal(k, shape, f32) * scale

    def gain(k, shape):
        return 1.0 + 0.02 * jax.random.normal(k, shape, f32)

    L = DEPTH
    return {
        "x": nrm(ks[0], (BATCH, SEQ, D_MODEL), 1.0),
        "p": nrm(ks[1], (DEPTH, BATCH, SEQ, PLE_DIM), 1.0),
        "positions": jnp.broadcast_to(jnp.arange(SEQ, dtype=jnp.int32), (BATCH, SEQ)),
        "g_mix": gain(ks[2], (L, D_MODEL)),
        "w_in": nrm(ks[3], (L, D_MODEL, IN_W), D_MODEL ** -0.5),
        "w_gla_a2": nrm(ks[4], (L, GLA_LOWRANK, GLA_QK_W), GLA_LOWRANK ** -0.5),
        "b_gla_a": nrm(ks[5], (L, GLA_QK_W), 0.1),
        "g_gla_out": gain(ks[6], (L, GLA_DV)),
        "g_moba_q": gain(ks[7], (L, MOBA_HD)),
        "g_moba_k": gain(ks[8], (L, MOBA_HD)),
        "w_out": nrm(ks[9], (L, MIX_W, D_MODEL), MIX_W ** -0.5),
        "g_ffn": gain(ks[10], (L, D_MODEL)),
        "w_up": nrm(ks[11], (L, D_MODEL, D_FF), D_MODEL ** -0.5),
        "w_down": nrm(ks[12], (L, D_FF, D_MODEL), D_FF ** -0.5),
        "g_ple_gate": gain(ks[13], (L, D_MODEL)),
        "w_ple_gate": nrm(ks[14], (L, D_MODEL, D_MODEL), D_MODEL ** -0.5),
        "w_ple_proj": nrm(ks[15], (L, PLE_DIM, D_MODEL), PLE_DIM ** -0.5),
        "g_ple_emb": gain(ks[16], (L, D_MODEL)),
    }


def reference(x, p, positions, g_mix, w_in, w_gla_a2, b_gla_a, g_gla_out, g_moba_q, g_moba_k,
              w_out, g_ffn, w_up, w_down, g_ple_gate, w_ple_gate, w_ple_proj, g_ple_emb):
    B, S, _ = x.shape
    split_points = np.cumsum(SPLIT_SIZES)[:-1].tolist()

    def heads(t, n):
        return t.reshape(B, S, n, -1).transpose(0, 2, 1, 3)

    h = x
    for i in range(DEPTH):
        u = rms_norm(h, g_mix[i])
        z = u @ w_in[i]
        gq, gk, gv, gg, ga, mq, mk, mv = jnp.split(z, split_points, axis=-1)

        a_pre = (ga @ w_gla_a2[i] + b_gla_a[i]).astype(jnp.float32)
        log_a = jax.nn.log_sigmoid(a_pre) / GLA_TAU
        o_gla = gla_chunked(heads(gq, GLA_HEADS).astype(jnp.float32),
                            heads(gk, GLA_HEADS).astype(jnp.float32),
                            heads(gv, GLA_HEADS).astype(jnp.float32),
                            heads(log_a, GLA_HEADS))
        o_gla = rms_norm(o_gla, g_gla_out[i]).transpose(0, 2, 1, 3).reshape(B, S, GLA_V_W)
        o_gla = o_gla.astype(x.dtype) * jax.nn.silu(gg)

        q = rope(rms_norm(heads(mq, MOBA_HEADS), g_moba_q[i]), positions)
        k = rope(rms_norm(heads(mk, MOBA_HEADS), g_moba_k[i]), positions)
        v = heads(mv, MOBA_HEADS)
        o_moba = moba_attention(q, k, v).transpose(0, 2, 1, 3).reshape(B, S, MOBA_W)

        h = h + jnp.concatenate([o_gla, o_moba.astype(x.dtype)], axis=-1) @ w_out[i]

        f = rms_norm(h, g_ffn[i]) @ w_up[i]
        h = h + jnp.square(jax.nn.relu(f)) @ w_down[i]

        gate = jax.nn.sigmoid(rms_norm(h, g_ple_gate[i]) @ w_ple_gate[i])
        e = rms_norm(p[i] @ w_ple_proj[i], g_ple_emb[i])
        h = h + gate * e
    return h
```

```python
import functools
import math

import numpy as np
import jax
import jax.numpy as jnp
from jax import lax
from jax.experimental import pallas as pl
from jax.experimental.pallas import tpu as pltpu

F32 = jnp.float32
BF16 = jnp.bfloat16
HIGHEST = lax.Precision.HIGHEST

D_MODEL = 1024
PLE_DIM = 256
GLA_HEADS = 4
GLA_DK = 64
GLA_DV = 128
GLA_LOWRANK = 16
GLA_TAU = 16.0
GLA_CHUNK = 64
MOBA_HEADS = 4
MOBA_HD = 128
MOBA_BLOCK = 256
MOBA_TOPK = 3
ROPE_THETA = 10000.0
D_FF = 4 * D_MODEL
EPS = 1e-6

GLA_QK_W = GLA_HEADS * GLA_DK
GLA_V_W = GLA_HEADS * GLA_DV
MOBA_W = MOBA_HEADS * MOBA_HD
GLA_IN_W = 2 * GLA_QK_W + 2 * GLA_V_W + GLA_LOWRANK
LANES = 128
GA_PAD = LANES
WA_W = 2 * GLA_QK_W + 2 * GLA_V_W + GA_PAD

VMEM_LIMIT_BYTES = 56 * 1024 * 1024

IN_TM = 512
GLA_TM = 512
OUT_TM = 512
FF_CHUNK = 1024

MOBA_QSCALE = (MOBA_HD ** -0.5) * math.log2(math.e)
NEG_INF = float("-inf")


def _rms(x, axis):
    return x * lax.rsqrt(jnp.mean(x * x, axis=axis, keepdims=True) + EPS)


def _log_sigmoid(a):
    return jnp.minimum(a, 0.0) - jnp.log1p(jnp.exp(-jnp.abs(a)))


def _dot_nt(a, b):
    return lax.dot_general(a, b, (((1,), (1,)), ((), ())), preferred_element_type=F32)


def _dot_tn(a, b):
    return lax.dot_general(a, b, (((0,), (0,)), ((), ())), preferred_element_type=F32)


def _in_proj_kernel(x_ref, pos_ref, gmix_ref, wa_ref, wbt_ref, wa2_ref, ba_ref,
                    gq_ref, gk_ref, inv_ref,
                    q_out, k_out, v_out, gg_out, la_out,
                    qt_out, kb_out, vt_out, kmean_out):
    tm = x_ref.shape[0]
    nblk = tm // MOBA_BLOCK
    u = (_rms(x_ref[...], -1) * gmix_ref[...]).astype(BF16)

    za = jnp.dot(u, wa_ref[...], preferred_element_type=F32)
    q_out[...] = za[:, 0:GLA_QK_W]
    k_out[...] = za[:, GLA_QK_W:2 * GLA_QK_W]
    v_out[...] = za[:, 2 * GLA_QK_W:2 * GLA_QK_W + GLA_V_W].astype(BF16)
    gg_out[...] = za[:, 2 * GLA_QK_W + GLA_V_W:2 * GLA_QK_W + 2 * GLA_V_W].astype(BF16)
    ga = za[:, 2 * GLA_QK_W + 2 * GLA_V_W:]
    a_pre = jnp.dot(ga, wa2_ref[...], precision=HIGHEST,
                    preferred_element_type=F32) + ba_ref[...]
    la_out[...] = _log_sigmoid(a_pre) * (1.0 / GLA_TAU)

    zt = _dot_nt(wbt_ref[...], u)
    ang = inv_ref[...] * pos_ref[...].astype(F32)
    cos = jnp.cos(ang)
    sin = jnp.sin(ang)
    half = MOBA_HD // 2

    def norm_rope(t, g_col):
        t = _rms(t, 0) * g_col
        t1, t2 = t[:half], t[half:]
        return jnp.concatenate([t1 * cos - t2 * sin, t2 * cos + t1 * sin], axis=0)

    for h in range(MOBA_HEADS):
        r0 = h * MOBA_HD
        qr = norm_rope(zt[r0:r0 + MOBA_HD], gq_ref[...]) * MOBA_QSCALE
        kr = norm_rope(zt[MOBA_W + r0:MOBA_W + r0 + MOBA_HD], gk_ref[...])
        krow = kr.T
        vt = zt[2 * MOBA_W + r0:2 * MOBA_W + r0 + MOBA_HD]
        for t in range(nblk):
            c0 = t * MOBA_BLOCK
            qt_out[h, t] = qr[:, c0:c0 + MOBA_BLOCK].astype(BF16)
            vt_out[h, t] = vt[:, c0:c0 + MOBA_BLOCK].astype(BF16)
            kblk = krow[c0:c0 + MOBA_BLOCK]
            kb_out[h, t] = kblk.astype(BF16)
            kmean_out[h, t] = jnp.mean(kblk, axis=0, keepdims=True)


def _in_proj(x2, pos, g_mix, wa, wbt, wa2, ba, gq_col, gk_col, inv_col):
    s = x2.shape[0]
    tm = min(IN_TM, s)
    nb = s // MOBA_BLOCK
    nblk = tm // MOBA_BLOCK
    const = lambda shape: pl.BlockSpec(shape, lambda i: (0,) * len(shape),
                                       pipeline_mode=pl.Buffered(1))
    row = lambda w: pl.BlockSpec((tm, w), lambda i: (i, 0))
    hblk = lambda a, b: pl.BlockSpec((MOBA_HEADS, nblk, a, b), lambda i: (0, i, 0, 0))
    return pl.pallas_call(
        _in_proj_kernel,
        grid=(s // tm,),
        in_specs=[
            row(D_MODEL),
            pl.BlockSpec((1, tm), lambda i: (0, i)),
            const((1, D_MODEL)),
            const((D_MODEL, WA_W)),
            const((3 * MOBA_W, D_MODEL)),
            const((GA_PAD, GLA_QK_W)),
            const((1, GLA_QK_W)),
            const((MOBA_HD, 1)),
            const((MOBA_HD, 1)),
            const((MOBA_HD // 2, 1)),
        ],
        out_specs=[
            row(GLA_QK_W), row(GLA_QK_W), row(GLA_V_W), row(GLA_V_W), row(GLA_QK_W),
            hblk(MOBA_HD, MOBA_BLOCK), hblk(MOBA_BLOCK, MOBA_HD), hblk(MOBA_HD, MOBA_BLOCK),
            hblk(1, MOBA_HD),
        ],
        out_shape=[
            jax.ShapeDtypeStruct((s, GLA_QK_W), F32),
            jax.ShapeDtypeStruct((s, GLA_QK_W), F32),
            jax.ShapeDtypeStruct((s, GLA_V_W), BF16),
            jax.ShapeDtypeStruct((s, GLA_V_W), BF16),
            jax.ShapeDtypeStruct((s, GLA_QK_W), F32),
            jax.ShapeDtypeStruct((MOBA_HEADS, nb, MOBA_HD, MOBA_BLOCK), BF16),
            jax.ShapeDtypeStruct((MOBA_HEADS, nb, MOBA_BLOCK, MOBA_HD), BF16),
            jax.ShapeDtypeStruct((MOBA_HEADS, nb, MOBA_HD, MOBA_BLOCK), BF16),
            jax.ShapeDtypeStruct((MOBA_HEADS, nb, 1, MOBA_HD), F32),
        ],
        compiler_params=pltpu.CompilerParams(
            dimension_semantics=("arbitrary",), vmem_limit_bytes=VMEM_LIMIT_BYTES),
        name="in_proj",
    )(x2, pos, g_mix, wa, wbt, wa2, ba, gq_col, gk_col, inv_col)


def _gla_kernel(q_ref, k_ref, v_ref, gg_ref, la_ref, gout_ref, o_ref, st_ref):
    c = GLA_CHUNK
    hc = GLA_HEADS * c

    @pl.when(pl.program_id(0) == 0)
    def _():
        st_ref[...] = jnp.zeros_like(st_ref)

    r = lax.broadcasted_iota(jnp.int32, (hc, GLA_QK_W), 0)
    l = lax.broadcasted_iota(jnp.int32, (hc, GLA_QK_W), 1)
    same_head = (r // c) == (l // GLA_DK)
    causal = same_head & ((l % c) <= (r % c))
    tr = lax.broadcasted_iota(jnp.int32, (c, c), 0)
    tc = lax.broadcasted_iota(jnp.int32, (c, c), 1)
    tri = (tc <= tr).astype(F32)
    scale = GLA_DK ** -0.5

    def stack_heads(a, width):
        return jnp.concatenate([a[:, h * width:(h + 1) * width] for h in range(GLA_HEADS)], axis=0)

    def block_diag(a):
        return jnp.where(same_head, jnp.concatenate([a] * GLA_HEADS, axis=0), 0.0).astype(BF16)

    def chunk(ci, carry):
        rows = pl.ds(pl.multiple_of(ci * c, c), c)
        b = jnp.dot(tri, la_ref[rows, :], precision=HIGHEST, preferred_element_type=F32)
        b_last = b[c - 1:c, :]
        q = q_ref[rows, :]
        k = k_ref[rows, :]
        qb = block_diag(q * jnp.exp(b) * scale)
        kb = block_diag(k * jnp.exp(-b))
        keb = block_diag(k * jnp.exp(b_last - b))
        vst = stack_heads(v_ref[rows, :], GLA_DV)
        a = jnp.where(causal, _dot_nt(qb, kb), 0.0).astype(BF16)
        st = st_ref[...]
        o = jnp.dot(a, vst, preferred_element_type=F32) + _dot_nt(qb, st.astype(BF16))
        st_ref[...] = st * jnp.exp(b_last) + _dot_tn(vst, keb)
        o = _rms(o, -1) * gout_ref[...]
        gg = stack_heads(gg_ref[rows, :], GLA_DV).astype(F32)
        o = o * (gg * jax.nn.sigmoid(gg))
        for h in range(GLA_HEADS):
            o_ref[rows, h * GLA_DV:(h + 1) * GLA_DV] = o[h * c:(h + 1) * c].astype(o_ref.dtype)
        return carry

    lax.fori_loop(0, q_ref.shape[0] // c, chunk, 0)


def _gla(q, k, v, gg, la, g_out):
    s = q.shape[0]
    tm = min(GLA_TM, s)
    row = lambda w: pl.BlockSpec((tm, w), lambda i: (i, 0))
    return pl.pallas_call(
        _gla_kernel,
        grid=(s // tm,),
        in_specs=[row(GLA_QK_W), row(GLA_QK_W), row(GLA_V_W), row(GLA_V_W), row(GLA_QK_W),
                  pl.BlockSpec((1, GLA_DV), lambda i: (0, 0))],
        out_specs=row(GLA_V_W),
        out_shape=jax.ShapeDtypeStruct((s, GLA_V_W), BF16),
        scratch_shapes=[pltpu.VMEM((GLA_DV, GLA_QK_W), F32)],
        compiler_params=pltpu.CompilerParams(dimension_semantics=("arbitrary",)),
        name="gla",
    )(q, k, v, gg, la, g_out)


def _moba_kernel(qt_ref, k_ref, vt_ref, kmean_ref, o_ref, bias_ref):
    qi = pl.program_id(1)
    nb = kmean_ref.shape[0]
    tq = qt_ref.shape[1]
    qt = qt_ref[...]

    gate = jnp.dot(kmean_ref[...], qt.astype(F32), precision=HIGHEST,
                   preferred_element_type=F32)
    blk = lax.broadcasted_iota(jnp.int32, (nb, tq), 0)
    gate = jnp.where(blk < qi, gate, NEG_INF)
    sel = jnp.zeros((nb, tq), jnp.bool_)
    for _ in range(min(MOBA_TOPK, nb)):
        m = jnp.max(gate, axis=0, keepdims=True)
        first = jnp.min(jnp.where(gate == m, blk, nb), axis=0, keepdims=True)
        pick = (blk == first) & (m > NEG_INF)
        sel = sel | pick
        gate = jnp.where(pick, NEG_INF, gate)
    bias_ref[...] = jnp.where(sel, 0.0, NEG_INF)

    s = jnp.dot(k_ref[qi], qt, preferred_element_type=F32)
    kk = lax.broadcasted_iota(jnp.int32, s.shape, 0)
    qq = lax.broadcasted_iota(jnp.int32, s.shape, 1)
    s = jnp.where(kk <= qq, s, NEG_INF)
    m0 = jnp.max(s, axis=0, keepdims=True)
    p = jnp.exp2(s - m0)
    l0 = jnp.sum(p, axis=0, keepdims=True)
    acc0 = jnp.dot(vt_ref[qi], p.astype(BF16), preferred_element_type=F32)

    def past_block(j, carry):
        m, l, acc = carry
        s = jnp.dot(k_ref[j], qt, preferred_element_type=F32) + bias_ref[pl.ds(j, 1), :]
        m_new = jnp.maximum(m, jnp.max(s, axis=0, keepdims=True))
        alpha = jnp.exp2(m - m_new)
        p = jnp.exp2(s - m_new)
        l = alpha * l + jnp.sum(p, axis=0, keepdims=True)
        acc = alpha * acc + jnp.dot(vt_ref[j], p.astype(BF16), preferred_element_type=F32)
        return m_new, l, acc

    _, l, acc = lax.fori_loop(0, qi, past_block, (m0, l0, acc0))
    o_ref[...] = (acc / l).T.astype(o_ref.dtype)


def _moba(qt, kb, vt, kmean):
    heads, nb, hd, blk = qt.shape
    s = nb * blk
    return pl.pallas_call(
        _moba_kernel,
        grid=(heads, nb),
        in_specs=[
            pl.BlockSpec((None, None, hd, blk), lambda h, i: (h, i, 0, 0)),
            pl.BlockSpec((None, nb, blk, hd), lambda h, i: (h, 0, 0, 0)),
            pl.BlockSpec((None, nb, hd, blk), lambda h, i: (h, 0, 0, 0)),
            pl.BlockSpec((None, nb, hd), lambda h, i: (h, 0, 0)),
        ],
        out_specs=pl.BlockSpec((blk, hd), lambda h, i: (i, h)),
        out_shape=jax.ShapeDtypeStruct((s, heads * hd), BF16),
        scratch_shapes=[pltpu.VMEM((nb, blk), F32)],
        compiler_params=pltpu.CompilerParams(
            dimension_semantics=("arbitrary", "arbitrary"), vmem_limit_bytes=VMEM_LIMIT_BYTES),
        name="moba",
    )(qt, kb, vt, kmean)


def _out_mlp_kernel(x_ref, og_ref, om_ref, p_ref, wo_ref, gffn_ref, wup_ref, wdn_ref,
                    gpg_ref, wpg_ref, wpp_ref, gpe_ref, o_ref):
    h = x_ref[...]
    h = h + jnp.dot(og_ref[...], wo_ref[:GLA_V_W, :], preferred_element_type=F32)
    h = h + jnp.dot(om_ref[...], wo_ref[GLA_V_W:, :], preferred_element_type=F32)

    u = (_rms(h, -1) * gffn_ref[...]).astype(BF16)
    mlp = jnp.zeros_like(h)
    for c0 in range(0, D_FF, FF_CHUNK):
        f = jnp.dot(u, wup_ref[:, c0:c0 + FF_CHUNK], preferred_element_type=F32)
        f = jnp.square(jnp.maximum(f, 0.0)).astype(BF16)
        mlp = mlp + jnp.dot(f, wdn_ref[c0:c0 + FF_CHUNK, :], preferred_element_type=F32)
    h = h + mlp

    u = (_rms(h, -1) * gpg_ref[...]).astype(BF16)
    gate = jax.nn.sigmoid(jnp.dot(u, wpg_ref[...], preferred_element_type=F32))
    e = jnp.dot(p_ref[...].astype(BF16), wpp_ref[...], preferred_element_type=F32)
    e = _rms(e, -1) * gpe_ref[...]
    o_ref[...] = h + gate * e


def _out_mlp(x2, og, om, p2, wo, g_ffn, wup, wdn, g_pg, wpg, wpp, g_pe):
    s = x2.shape[0]
    tm = min(OUT_TM, s)
    const = lambda shape: pl.BlockSpec(shape, lambda i: (0,) * len(shape),
                                       pipeline_mode=pl.Buffered(1))
    row = lambda w: pl.BlockSpec((tm, w), lambda i: (i, 0))
    return pl.pallas_call(
        _out_mlp_kernel,
        grid=(s // tm,),
        in_specs=[
            row(D_MODEL), row(GLA_V_W), row(MOBA_W), row(PLE_DIM),
            const((GLA_V_W + MOBA_W, D_MODEL)),
            const((1, D_MODEL)),
            const((D_MODEL, D_FF)),
            const((D_FF, D_MODEL)),
            const((1, D_MODEL)),
            const((D_MODEL, D_MODEL)),
            const((PLE_DIM, D_MODEL)),
            const((1, D_MODEL)),
        ],
        out_specs=row(D_MODEL),
        out_shape=jax.ShapeDtypeStruct((s, D_MODEL), F32),
        compiler_params=pltpu.CompilerParams(
            dimension_semantics=("arbitrary",), vmem_limit_bytes=VMEM_LIMIT_BYTES),
        name="out_mlp",
    )(x2, og, om, p2, wo, g_ffn, wup, wdn, g_pg, wpg, wpp, g_pe)


def _layer(h2, p2, pos, g_mix, w_in, w_gla_a2, b_gla_a, g_gla_out, g_moba_q, g_moba_k,
           w_out, g_ffn, w_up, w_down, g_ple_gate, w_ple_gate, w_ple_proj, g_ple_emb):
    s = h2.shape[0]
    assert s % MOBA_BLOCK == 0 and s % GLA_CHUNK == 0
    assert s % min(IN_TM, s) == 0 and s % min(GLA_TM, s) == 0 and s % min(OUT_TM, s) == 0

    wa = jnp.concatenate(
        [w_in[:, :GLA_IN_W], jnp.zeros((D_MODEL, GA_PAD - GLA_LOWRANK), w_in.dtype)],
        axis=1).astype(BF16)
    wbt = w_in[:, GLA_IN_W:].T.astype(BF16)
    wa2 = jnp.concatenate(
        [w_gla_a2, jnp.zeros((GA_PAD - GLA_LOWRANK, GLA_QK_W), w_gla_a2.dtype)], axis=0)
    half = MOBA_HD // 2
    inv_col = (1.0 / (ROPE_THETA ** (jnp.arange(half, dtype=F32) / half))).reshape(half, 1)

    gq, gk, gv, gg, la, qt, kb, vt, kmean = _in_proj(
        h2, pos, g_mix.reshape(1, -1), wa, wbt, wa2, b_gla_a.reshape(1, -1),
        g_moba_q.reshape(-1, 1), g_moba_k.reshape(-1, 1), inv_col)
    o_gla = _gla(gq, gk, gv, gg, la, g_gla_out.reshape(1, -1))
    o_moba = _moba(qt, kb, vt, kmean.reshape(MOBA_HEADS, s // MOBA_BLOCK, MOBA_HD))
    return _out_mlp(
        h2, o_gla, o_moba, p2, w_out.astype(BF16), g_ffn.reshape(1, -1),
        w_up.astype(BF16), w_down.astype(BF16), g_ple_gate.reshape(1, -1),
        w_ple_gate.astype(BF16), w_ple_proj.astype(BF16), g_ple_emb.reshape(1, -1))


def kernel(x, p, positions, g_mix, w_in, w_gla_a2, b_gla_a, g_gla_out, g_moba_q, g_moba_k,
           w_out, g_ffn, w_up, w_down, g_ple_gate, w_ple_gate, w_ple_proj, g_ple_emb):
    depth = p.shape[0]
    batch = x.shape[0]
    outs = []
    for b in range(batch):
        h = x[b]
        pos = positions[b:b + 1]
        for i in range(depth):
            h = _layer(h, p[i, b], pos, g_mix[i], w_in[i], w_gla_a2[i], b_gla_a[i],
                       g_gla_out[i], g_moba_q[i], g_moba_k[i], w_out[i], g_ffn[i], w_up[i],
                       w_down[i], g_ple_gate[i], w_ple_gate[i], w_ple_proj[i], g_ple_emb[i])
        outs.append(h)
    return jnp.stack(outs, axis=0)
```

```python
import functools
import math

import numpy as np
import jax
import jax.numpy as jnp
from jax import lax
from jax.experimental import pallas as pl
from jax.experimental.pallas import tpu as pltpu

F32 = jnp.float32
BF16 = jnp.bfloat16
HIGHEST = lax.Precision.HIGHEST

D_MODEL = 1024
PLE_DIM = 256
GLA_HEADS = 4
GLA_DK = 64
GLA_DV = 128
GLA_LOWRANK = 16
GLA_TAU = 16.0
GLA_CHUNK = 64
MOBA_HEADS = 4
MOBA_HD = 128
MOBA_BLOCK = 256
MOBA_TOPK = 3
ROPE_THETA = 10000.0
D_FF = 4 * D_MODEL
EPS = 1e-6

GLA_QK_W = GLA_HEADS * GLA_DK
GLA_V_W = GLA_HEADS * GLA_DV
MOBA_W = MOBA_HEADS * MOBA_HD
GLA_IN_W = 2 * GLA_QK_W + 2 * GLA_V_W + GLA_LOWRANK
LANES = 128
GA_PAD = LANES
WA_W = 2 * GLA_QK_W + 2 * GLA_V_W + GA_PAD

VMEM_LIMIT_BYTES = 56 * 1024 * 1024

IN_TM = 512
GLA_TM = 512
OUT_TM = 512
FF_CHUNK = 1024

MOBA_QSCALE = (MOBA_HD ** -0.5) * math.log2(math.e)
NEG_INF = float("-inf")


def _rms(x, axis):
    return x * lax.rsqrt(jnp.mean(x * x, axis=axis, keepdims=True) + EPS)


def _log_sigmoid(a):
    return jnp.minimum(a, 0.0) - jnp.log1p(jnp.exp(-jnp.abs(a)))


def _dot_nt(a, b):
    return lax.dot_general(a, b, (((1,), (1,)), ((), ())), preferred_element_type=F32)


def _dot_tn(a, b):
    return lax.dot_general(a, b, (((0,), (0,)), ((), ())), preferred_element_type=F32)


def _in_proj_kernel(x_ref, pos_ref, gmix_ref, wa_ref, wbt_ref, wa2_ref, ba_ref,
                    gq_ref, gk_ref, inv_ref,
                    q_out, k_out, v_out, gg_out, la_out,
                    qt_out, kb_out, vt_out, kmean_out):
    tm = x_ref.shape[0]
    nblk = tm // MOBA_BLOCK
    u = (_rms(x_ref[...], -1) * gmix_ref[...]).astype(BF16)

    za = jnp.dot(u, wa_ref[...], preferred_element_type=F32)
    q_out[...] = za[:, 0:GLA_QK_W]
    k_out[...] = za[:, GLA_QK_W:2 * GLA_QK_W]
    v_out[...] = za[:, 2 * GLA_QK_W:2 * GLA_QK_W + GLA_V_W].astype(BF16)
    gg_out[...] = za[:, 2 * GLA_QK_W + GLA_V_W:2 * GLA_QK_W + 2 * GLA_V_W].astype(BF16)
    ga = za[:, 2 * GLA_QK_W + 2 * GLA_V_W:]
    a_pre = jnp.dot(ga, wa2_ref[...], precision=HIGHEST,
                    preferred_element_type=F32) + ba_ref[...]
    la_out[...] = _log_sigmoid(a_pre) * (1.0 / GLA_TAU)

    zt = _dot_nt(wbt_ref[...], u)
    ang = inv_ref[...] * pos_ref[...].astype(F32)
    cos = jnp.cos(ang)
    sin = jnp.sin(ang)
    half = MOBA_HD // 2

    def norm_rope(t, g_col):
        t = _rms(t, 0) * g_col
        t1, t2 = t[:half], t[half:]
        return jnp.concatenate([t1 * cos - t2 * sin, t2 * cos + t1 * sin], axis=0)

    for h in range(MOBA_HEADS):
        r0 = h * MOBA_HD
        qr = norm_rope(zt[r0:r0 + MOBA_HD], gq_ref[...]) * MOBA_QSCALE
        kr = norm_rope(zt[MOBA_W + r0:MOBA_W + r0 + MOBA_HD], gk_ref[...])
        krow = kr.T
        vt = zt[2 * MOBA_W + r0:2 * MOBA_W + r0 + MOBA_HD]
        for t in range(nblk):
            c0 = t * MOBA_BLOCK
            qt_out[h, t] = qr[:, c0:c0 + MOBA_BLOCK].astype(BF16)
            vt_out[h, t] = vt[:, c0:c0 + MOBA_BLOCK].astype(BF16)
            kblk = krow[c0:c0 + MOBA_BLOCK]
            kb_out[h, t] = kblk.astype(BF16)
            kmean_out[h, t] = jnp.mean(kblk, axis=0, keepdims=True)


def _in_proj(x2, pos, g_mix, wa, wbt, wa2, ba, gq_col, gk_col, inv_col):
    s = x2.shape[0]
    tm = min(IN_TM, s)
    nb = s // MOBA_BLOCK
    nblk = tm // MOBA_BLOCK
    const = lambda shape: pl.BlockSpec(shape, lambda i: (0,) * len(shape),
                                       pipeline_mode=pl.Buffered(1))
    row = lambda w: pl.BlockSpec((tm, w), lambda i: (i, 0))
    hblk = lambda a, b: pl.BlockSpec((MOBA_HEADS, nblk, a, b), lambda i: (0, i, 0, 0))
    return pl.pallas_call(
        _in_proj_kernel,
        grid=(s // tm,),
        in_specs=[
            row(D_MODEL),
            pl.BlockSpec((1, tm), lambda i: (0, i)),
            const((1, D_MODEL)),
            const((D_MODEL, WA_W)),
            const((3 * MOBA_W, D_MODEL)),
            const((GA_PAD, GLA_QK_W)),
            const((1, GLA_QK_W)),
            const((MOBA_HD, 1)),
            const((MOBA_HD, 1)),
            const((MOBA_HD // 2, 1)),
        ],
        out_specs=[
            row(GLA_QK_W), row(GLA_QK_W), row(GLA_V_W), row(GLA_V_W), row(GLA_QK_W),
            hblk(MOBA_HD, MOBA_BLOCK), hblk(MOBA_BLOCK, MOBA_HD), hblk(MOBA_HD, MOBA_BLOCK),
            hblk(1, MOBA_HD),
        ],
        out_shape=[
            jax.ShapeDtypeStruct((s, GLA_QK_W), F32),
            jax.ShapeDtypeStruct((s, GLA_QK_W), F32),
            jax.ShapeDtypeStruct((s, GLA_V_W), BF16),
            jax.ShapeDtypeStruct((s, GLA_V_W), BF16),
            jax.ShapeDtypeStruct((s, GLA_QK_W), F32),
            jax.ShapeDtypeStruct((MOBA_HEADS, nb, MOBA_HD, MOBA_BLOCK), BF16),
            jax.ShapeDtypeStruct((MOBA_HEADS, nb, MOBA_BLOCK, MOBA_HD), BF16),
            jax.ShapeDtypeStruct((MOBA_HEADS, nb, MOBA_HD, MOBA_BLOCK), BF16),
            jax.ShapeDtypeStruct((MOBA_HEADS, nb, 1, MOBA_HD), F32),
        ],
        compiler_params=pltpu.CompilerParams(
            dimension_semantics=("arbitrary",), vmem_limit_bytes=VMEM_LIMIT_BYTES),
        name="in_proj",
    )(x2, pos, g_mix, wa, wbt, wa2, ba, gq_col, gk_col, inv_col)


def _gla_kernel(q_ref, k_ref, v_ref, gg_ref, la_ref, gout_ref, o_ref, st_ref):
    c = GLA_CHUNK
    hc = GLA_HEADS * c

    @pl.when(pl.program_id(0) == 0)
    def _():
        st_ref[...] = jnp.zeros_like(st_ref)

    r = lax.broadcasted_iota(jnp.int32, (hc, GLA_QK_W), 0)
    l = lax.broadcasted_iota(jnp.int32, (hc, GLA_QK_W), 1)
    same_head = (r // c) == (l // GLA_DK)
    causal = same_head & ((l % c) <= (r % c))
    tr = lax.broadcasted_iota(jnp.int32, (c, c), 0)
    tc = lax.broadcasted_iota(jnp.int32, (c, c), 1)
    tri = (tc <= tr).astype(F32)
    scale = GLA_DK ** -0.5

    def stack_heads(a, width):
        return jnp.concatenate([a[:, h * width:(h + 1) * width] for h in range(GLA_HEADS)], axis=0)

    def block_diag(a):
        return jnp.where(same_head, jnp.concatenate([a] * GLA_HEADS, axis=0), 0.0).astype(BF16)

    def chunk(ci, carry):
        rows = pl.ds(pl.multiple_of(ci * c, c), c)
        b = jnp.dot(tri, la_ref[rows, :], precision=HIGHEST, preferred_element_type=F32)
        b_last = b[c - 1:c, :]
        q = q_ref[rows, :]
        k = k_ref[rows, :]
        qb = block_diag(q * jnp.exp(b) * scale)
        kb = block_diag(k * jnp.exp(-b))
        keb = block_diag(k * jnp.exp(b_last - b))
        vst = stack_heads(v_ref[rows, :], GLA_DV)
        a = jnp.where(causal, _dot_nt(qb, kb), 0.0).astype(BF16)
        st = st_ref[...]
        o = jnp.dot(a, vst, preferred_element_type=F32) + _dot_nt(qb, st.astype(BF16))
        st_ref[...] = st * jnp.exp(b_last) + _dot_tn(vst, keb)
        o = _rms(o, -1) * gout_ref[...]
        gg = stack_heads(gg_ref[rows, :], GLA_DV).astype(F32)
        o = o * (gg * jax.nn.sigmoid(gg))
        for h in range(GLA_HEADS):
            o_ref[rows, h * GLA_DV:(h + 1) * GLA_DV] = o[h * c:(h + 1) * c].astype(o_ref.dtype)
        return carry

    lax.fori_loop(0, q_ref.shape[0] // c, chunk, 0)


def _gla(q, k, v, gg, la, g_out):
    s = q.shape[0]
    tm = min(GLA_TM, s)
    row = lambda w: pl.BlockSpec((tm, w), lambda i: (i, 0))
    return pl.pallas_call(
        _gla_kernel,
        grid=(s // tm,),
        in_specs=[row(GLA_QK_W), row(GLA_QK_W), row(GLA_V_W), row(GLA_V_W), row(GLA_QK_W),
                  pl.BlockSpec((1, GLA_DV), lambda i: (0, 0))],
        out_specs=row(GLA_V_W),
        out_shape=jax.ShapeDtypeStruct((s, GLA_V_W), BF16),
        scratch_shapes=[pltpu.VMEM((GLA_DV, GLA_QK_W), F32)],
        compiler_params=pltpu.CompilerParams(dimension_semantics=("arbitrary",)),
        name="gla",
    )(q, k, v, gg, la, g_out)


def _moba_kernel(qt_ref, k_ref, vt_ref, kmean_ref, o_ref, bias_ref, acc_ref):
    qi = pl.program_id(0)
    heads, nb, _ = kmean_ref.shape
    tq = qt_ref.shape[2]

    blk = lax.broadcasted_iota(jnp.int32, (nb, tq), 0)
    for h in range(heads):
        gate = jnp.dot(kmean_ref[h], qt_ref[h].astype(F32), precision=HIGHEST,
                       preferred_element_type=F32)
        gate = jnp.where(blk < qi, gate, NEG_INF)
        sel = jnp.zeros((nb, tq), jnp.bool_)
        for _ in range(min(MOBA_TOPK, nb)):
            m = jnp.max(gate, axis=0, keepdims=True)
            first = jnp.min(jnp.where(gate == m, blk, nb), axis=0, keepdims=True)
            pick = (blk == first) & (m > NEG_INF)
            sel = sel | pick
            gate = jnp.where(pick, NEG_INF, gate)
        bias_ref[h] = jnp.where(sel, 0.0, NEG_INF)

    kk = lax.broadcasted_iota(jnp.int32, (k_ref.shape[2], tq), 0)
    qq = lax.broadcasted_iota(jnp.int32, (k_ref.shape[2], tq), 1)
    causal = kk <= qq
    stats = []
    for h in range(heads):
        s = jnp.dot(k_ref[h, qi], qt_ref[h], preferred_element_type=F32)
        s = jnp.where(causal, s, NEG_INF)
        m = jnp.max(s, axis=0, keepdims=True)
        p = jnp.exp2(s - m)
        stats += [m, jnp.sum(p, axis=0, keepdims=True)]
        acc_ref[h] = jnp.dot(vt_ref[h, qi], p.astype(BF16), preferred_element_type=F32)

    def past_block(j, carry):
        scores = [jnp.dot(k_ref[h, j], qt_ref[h], preferred_element_type=F32)
                  for h in range(heads)]
        probs, out = [], []
        for h in range(heads):
            m, l = carry[2 * h], carry[2 * h + 1]
            s = scores[h]
            bias = bias_ref[h, pl.ds(j, 1), :]
            m_new = jnp.maximum(m, jnp.max(s, axis=0, keepdims=True) + bias)
            alpha = jnp.exp2(m - m_new)
            p = jnp.exp2(s + (bias - m_new))
            l = alpha * l + jnp.sum(p, axis=0, keepdims=True)
            probs.append((alpha, p.astype(BF16)))
            out += [m_new, l]
        for h in range(heads):
            alpha, p = probs[h]
            acc_ref[h] = alpha * acc_ref[h] + jnp.dot(vt_ref[h, j], p,
                                                      preferred_element_type=F32)
        return tuple(out)

    stats = lax.fori_loop(0, qi, past_block, tuple(stats))
    hd = acc_ref.shape[1]
    for h in range(heads):
        o_ref[:, h * hd:(h + 1) * hd] = (acc_ref[h] / stats[2 * h + 1]).T.astype(o_ref.dtype)


def _moba(qt, kb, vt, kmean):
    heads, nb, hd, blk = qt.shape
    s = nb * blk
    const = lambda shape: pl.BlockSpec(shape, lambda i: (0,) * len(shape),
                                       pipeline_mode=pl.Buffered(1))
    return pl.pallas_call(
        _moba_kernel,
        grid=(nb,),
        in_specs=[
            pl.BlockSpec((heads, None, hd, blk), lambda i: (0, i, 0, 0)),
            const((heads, nb, blk, hd)),
            const((heads, nb, hd, blk)),
            const((heads, nb, hd)),
        ],
        out_specs=pl.BlockSpec((blk, heads * hd), lambda i: (i, 0)),
        out_shape=jax.ShapeDtypeStruct((s, heads * hd), BF16),
        scratch_shapes=[pltpu.VMEM((heads, nb, blk), F32), pltpu.VMEM((heads, hd, blk), F32)],
        compiler_params=pltpu.CompilerParams(
            dimension_semantics=("arbitrary",), vmem_limit_bytes=VMEM_LIMIT_BYTES),
        name="moba",
    )(qt, kb, vt, kmean)


def _out_mlp_kernel(x_ref, og_ref, om_ref, p_ref, wo_ref, gffn_ref, wup_ref, wdn_ref,
                    gpg_ref, wpg_ref, wpp_ref, gpe_ref, o_ref):
    h = x_ref[...]
    h = h + jnp.dot(og_ref[...], wo_ref[:GLA_V_W, :], preferred_element_type=F32)
    h = h + jnp.dot(om_ref[...], wo_ref[GLA_V_W:, :], preferred_element_type=F32)

    u = (_rms(h, -1) * gffn_ref[...]).astype(BF16)
    mlp = jnp.zeros_like(h)
    for c0 in range(0, D_FF, FF_CHUNK):
        f = jnp.dot(u, wup_ref[:, c0:c0 + FF_CHUNK], preferred_element_type=F32)
        f = jnp.square(jnp.maximum(f, 0.0)).astype(BF16)
        mlp = mlp + jnp.dot(f, wdn_ref[c0:c0 + FF_CHUNK, :], preferred_element_type=F32)
    h = h + mlp

    u = (_rms(h, -1) * gpg_ref[...]).astype(BF16)
    gate = jax.nn.sigmoid(jnp.dot(u, wpg_ref[...], preferred_element_type=F32))
    e = jnp.dot(p_ref[...].astype(BF16), wpp_ref[...], preferred_element_type=F32)
    e = _rms(e, -1) * gpe_ref[...]
    o_ref[...] = h + gate * e


def _out_mlp(x2, og, om, p2, wo, g_ffn, wup, wdn, g_pg, wpg, wpp, g_pe):
    s = x2.shape[0]
    tm = min(OUT_TM, s)
    const = lambda shape: pl.BlockSpec(shape, lambda i: (0,) * len(shape),
                                       pipeline_mode=pl.Buffered(1))
    row = lambda w: pl.BlockSpec((tm, w), lambda i: (i, 0))
    return pl.pallas_call(
        _out_mlp_kernel,
        grid=(s // tm,),
        in_specs=[
            row(D_MODEL), row(GLA_V_W), row(MOBA_W), row(PLE_DIM),
            const((GLA_V_W + MOBA_W, D_MODEL)),
            const((1, D_MODEL)),
            const((D_MODEL, D_FF)),
            const((D_FF, D_MODEL)),
            const((1, D_MODEL)),
            const((D_MODEL, D_MODEL)),
            const((PLE_DIM, D_MODEL)),
            const((1, D_MODEL)),
        ],
        out_specs=row(D_MODEL),
        out_shape=jax.ShapeDtypeStruct((s, D_MODEL), F32),
        compiler_params=pltpu.CompilerParams(
            dimension_semantics=("arbitrary",), vmem_limit_bytes=VMEM_LIMIT_BYTES),
        name="out_mlp",
    )(x2, og, om, p2, wo, g_ffn, wup, wdn, g_pg, wpg, wpp, g_pe)


def _layer(h2, p2, pos, g_mix, w_in, w_gla_a2, b_gla_a, g_gla_out, g_moba_q, g_moba_k,
           w_out, g_ffn, w_up, w_down, g_ple_gate, w_ple_gate, w_ple_proj, g_ple_emb):
    s = h2.shape[0]
    assert s % MOBA_BLOCK == 0 and s % GLA_CHUNK == 0
    assert s % min(IN_TM, s) == 0 and s % min(GLA_TM, s) == 0 and s % min(OUT_TM, s) == 0

    wa = jnp.concatenate(
        [w_in[:, :GLA_IN_W], jnp.zeros((D_MODEL, GA_PAD - GLA_LOWRANK), w_in.dtype)],
        axis=1).astype(BF16)
    wbt = w_in[:, GLA_IN_W:].T.astype(BF16)
    wa2 = jnp.concatenate(
        [w_gla_a2, jnp.zeros((GA_PAD - GLA_LOWRANK, GLA_QK_W), w_gla_a2.dtype)], axis=0)
    half = MOBA_HD // 2
    inv_col = (1.0 / (ROPE_THETA ** (jnp.arange(half, dtype=F32) / half))).reshape(half, 1)

    gq, gk, gv, gg, la, qt, kb, vt, kmean = _in_proj(
        h2, pos, g_mix.reshape(1, -1), wa, wbt, wa2, b_gla_a.reshape(1, -1),
        g_moba_q.reshape(-1, 1), g_moba_k.reshape(-1, 1), inv_col)
    o_gla = _gla(gq, gk, gv, gg, la, g_gla_out.reshape(1, -1))
    o_moba = _moba(qt, kb, vt, kmean.reshape(MOBA_HEADS, s // MOBA_BLOCK, MOBA_HD))
    return _out_mlp(
        h2, o_gla, o_moba, p2, w_out.astype(BF16), g_ffn.reshape(1, -1),
        w_up.astype(BF16), w_down.astype(BF16), g_ple_gate.reshape(1, -1),
        w_ple_gate.astype(BF16), w_ple_proj.astype(BF16), g_ple_emb.reshape(1, -1))


def kernel(x, p, positions, g_mix, w_in, w_gla_a2, b_gla_a, g_gla_out, g_moba_q, g_moba_k,
           w_out, g_ffn, w_up, w_down, g_ple_gate, w_ple_gate, w_ple_proj, g_ple_emb):
    depth = p.shape[0]
    batch = x.shape[0]
    outs = []
    for b in range(batch):
        h = x[b]
        pos = positions[b:b + 1]
        for i in range(depth):
            h = _layer(h, p[i, b], pos, g_mix[i], w_in[i], w_gla_a2[i], b_gla_a[i],
                       g_gla_out[i], g_moba_q[i], g_moba_k[i], w_out[i], g_ffn[i], w_up[i],
                       w_down[i], g_ple_gate[i], w_ple_gate[i], w_ple_proj[i], g_ple_emb[i])
        outs.append(h)
    return jnp.stack(outs, axis=0)
```

```python
import functools
import math

import numpy as np
import jax
import jax.numpy as jnp
from jax import lax
from jax.experimental import pallas as pl
from jax.experimental.pallas import tpu as pltpu

F32 = jnp.float32
BF16 = jnp.bfloat16
HIGHEST = lax.Precision.HIGHEST

D_MODEL = 1024
PLE_DIM = 256
GLA_HEADS = 4
GLA_DK = 64
GLA_DV = 128
GLA_LOWRANK = 16
GLA_TAU = 16.0
GLA_CHUNK = 64
MOBA_HEADS = 4
MOBA_HD = 128
MOBA_BLOCK = 256
MOBA_TOPK = 3
ROPE_THETA = 10000.0
D_FF = 4 * D_MODEL
EPS = 1e-6

GLA_QK_W = GLA_HEADS * GLA_DK
GLA_V_W = GLA_HEADS * GLA_DV
MOBA_W = MOBA_HEADS * MOBA_HD
GLA_IN_W = 2 * GLA_QK_W + 2 * GLA_V_W + GLA_LOWRANK
LANES = 128
GA_PAD = LANES
WA_W = 2 * GLA_QK_W + 2 * GLA_V_W + GA_PAD

VMEM_LIMIT_BYTES = 56 * 1024 * 1024

IN_TM = 512
GLA_TM = 512
OUT_TM = 512
FF_CHUNK = 1024
MOBA_BLOCKS_PER_STEP = 2
MOBA_STEPS_PER_ITER = 2
MOBA_MATMUL_LAG = 2

MOBA_QSCALE = (MOBA_HD ** -0.5) * math.log2(math.e)
NEG_INF = float("-inf")


def _rms(x, axis):
    return x * lax.rsqrt(jnp.mean(x * x, axis=axis, keepdims=True) + EPS)


def _log_sigmoid(a):
    return jnp.minimum(a, 0.0) - jnp.log1p(jnp.exp(-jnp.abs(a)))


def _dot_nt(a, b):
    return lax.dot_general(a, b, (((1,), (1,)), ((), ())), preferred_element_type=F32)


def _dot_tn(a, b):
    return lax.dot_general(a, b, (((0,), (0,)), ((), ())), preferred_element_type=F32)


def _in_proj_kernel(x_ref, pos_ref, gmix_ref, wa_ref, wbt_ref, wa2_ref, ba_ref,
                    gq_ref, gk_ref, inv_ref,
                    q_out, k_out, v_out, gg_out, la_out,
                    qt_out, kb_out, vt_out, kmean_out):
    tm = x_ref.shape[0]
    nblk = tm // MOBA_BLOCK
    u = (_rms(x_ref[...], -1) * gmix_ref[...]).astype(BF16)

    za = jnp.dot(u, wa_ref[...], preferred_element_type=F32)
    q_out[...] = za[:, 0:GLA_QK_W]
    k_out[...] = za[:, GLA_QK_W:2 * GLA_QK_W]
    v_out[...] = za[:, 2 * GLA_QK_W:2 * GLA_QK_W + GLA_V_W].astype(BF16)
    gg_out[...] = za[:, 2 * GLA_QK_W + GLA_V_W:2 * GLA_QK_W + 2 * GLA_V_W].astype(BF16)
    ga = za[:, 2 * GLA_QK_W + 2 * GLA_V_W:]
    a_pre = jnp.dot(ga, wa2_ref[...], precision=HIGHEST,
                    preferred_element_type=F32) + ba_ref[...]
    la_out[...] = _log_sigmoid(a_pre) * (1.0 / GLA_TAU)

    zt = _dot_nt(wbt_ref[...], u)
    ang = inv_ref[...] * pos_ref[...].astype(F32)
    cos = jnp.cos(ang)
    sin = jnp.sin(ang)
    half = MOBA_HD // 2

    def norm_rope(t, g_col):
        t = _rms(t, 0) * g_col
        t1, t2 = t[:half], t[half:]
        return jnp.concatenate([t1 * cos - t2 * sin, t2 * cos + t1 * sin], axis=0)

    for h in range(MOBA_HEADS):
        r0 = h * MOBA_HD
        qr = norm_rope(zt[r0:r0 + MOBA_HD], gq_ref[...]) * MOBA_QSCALE
        kr = norm_rope(zt[MOBA_W + r0:MOBA_W + r0 + MOBA_HD], gk_ref[...])
        krow = kr.T
        vt = zt[2 * MOBA_W + r0:2 * MOBA_W + r0 + MOBA_HD]
        for t in range(nblk):
            c0 = t * MOBA_BLOCK
            qt_out[h, t] = qr[:, c0:c0 + MOBA_BLOCK].astype(BF16)
            vt_out[h, t] = vt[:, c0:c0 + MOBA_BLOCK].astype(BF16)
            kblk = krow[c0:c0 + MOBA_BLOCK]
            kb_out[h, t] = kblk.astype(BF16)
            kmean_out[h, t] = jnp.mean(kblk, axis=0, keepdims=True)


def _in_proj(x2, pos, g_mix, wa, wbt, wa2, ba, gq_col, gk_col, inv_col):
    s = x2.shape[0]
    tm = min(IN_TM, s)
    nb = s // MOBA_BLOCK
    nblk = tm // MOBA_BLOCK
    const = lambda shape: pl.BlockSpec(shape, lambda i: (0,) * len(shape),
                                       pipeline_mode=pl.Buffered(1))
    row = lambda w: pl.BlockSpec((tm, w), lambda i: (i, 0))
    hblk = lambda a, b: pl.BlockSpec((MOBA_HEADS, nblk, a, b), lambda i: (0, i, 0, 0))
    return pl.pallas_call(
        _in_proj_kernel,
        grid=(s // tm,),
        in_specs=[
            row(D_MODEL),
            pl.BlockSpec((1, tm), lambda i: (0, i)),
            const((1, D_MODEL)),
            const((D_MODEL, WA_W)),
            const((3 * MOBA_W, D_MODEL)),
            const((GA_PAD, GLA_QK_W)),
            const((1, GLA_QK_W)),
            const((MOBA_HD, 1)),
            const((MOBA_HD, 1)),
            const((MOBA_HD // 2, 1)),
        ],
        out_specs=[
            row(GLA_QK_W), row(GLA_QK_W), row(GLA_V_W), row(GLA_V_W), row(GLA_QK_W),
            hblk(MOBA_HD, MOBA_BLOCK), hblk(MOBA_BLOCK, MOBA_HD), hblk(MOBA_HD, MOBA_BLOCK),
            hblk(1, MOBA_HD),
        ],
        out_shape=[
            jax.ShapeDtypeStruct((s, GLA_QK_W), F32),
            jax.ShapeDtypeStruct((s, GLA_QK_W), F32),
            jax.ShapeDtypeStruct((s, GLA_V_W), BF16),
            jax.ShapeDtypeStruct((s, GLA_V_W), BF16),
            jax.ShapeDtypeStruct((s, GLA_QK_W), F32),
            jax.ShapeDtypeStruct((MOBA_HEADS, nb, MOBA_HD, MOBA_BLOCK), BF16),
            jax.ShapeDtypeStruct((MOBA_HEADS, nb, MOBA_BLOCK, MOBA_HD), BF16),
            jax.ShapeDtypeStruct((MOBA_HEADS, nb, MOBA_HD, MOBA_BLOCK), BF16),
            jax.ShapeDtypeStruct((MOBA_HEADS, nb, 1, MOBA_HD), F32),
        ],
        compiler_params=pltpu.CompilerParams(
            dimension_semantics=("arbitrary",), vmem_limit_bytes=VMEM_LIMIT_BYTES),
        name="in_proj",
    )(x2, pos, g_mix, wa, wbt, wa2, ba, gq_col, gk_col, inv_col)


def _gla_kernel(q_ref, k_ref, v_ref, gg_ref, la_ref, gout_ref, o_ref, st_ref):
    c = GLA_CHUNK
    hc = GLA_HEADS * c

    @pl.when(pl.program_id(0) == 0)
    def _():
        st_ref[...] = jnp.zeros_like(st_ref)

    r = lax.broadcasted_iota(jnp.int32, (hc, GLA_QK_W), 0)
    l = lax.broadcasted_iota(jnp.int32, (hc, GLA_QK_W), 1)
    same_head = (r // c) == (l // GLA_DK)
    causal = same_head & ((l % c) <= (r % c))
    tr = lax.broadcasted_iota(jnp.int32, (c, c), 0)
    tc = lax.broadcasted_iota(jnp.int32, (c, c), 1)
    tri = (tc <= tr).astype(F32)
    scale = GLA_DK ** -0.5

    def stack_heads(a, width):
        return jnp.concatenate([a[:, h * width:(h + 1) * width] for h in range(GLA_HEADS)], axis=0)

    def block_diag(a):
        return jnp.where(same_head, jnp.concatenate([a] * GLA_HEADS, axis=0), 0.0).astype(BF16)

    def chunk(ci, carry):
        rows = pl.ds(pl.multiple_of(ci * c, c), c)
        b = jnp.dot(tri, la_ref[rows, :], precision=HIGHEST, preferred_element_type=F32)
        b_last = b[c - 1:c, :]
        q = q_ref[rows, :]
        k = k_ref[rows, :]
        qb = block_diag(q * jnp.exp(b) * scale)
        kb = block_diag(k * jnp.exp(-b))
        keb = block_diag(k * jnp.exp(b_last - b))
        vst = stack_heads(v_ref[rows, :], GLA_DV)
        a = jnp.where(causal, _dot_nt(qb, kb), 0.0).astype(BF16)
        st = st_ref[...]
        o = jnp.dot(a, vst, preferred_element_type=F32) + _dot_nt(qb, st.astype(BF16))
        st_ref[...] = st * jnp.exp(b_last) + _dot_tn(vst, keb)
        o = _rms(o, -1) * gout_ref[...]
        gg = stack_heads(gg_ref[rows, :], GLA_DV).astype(F32)
        o = o * (gg * jax.nn.sigmoid(gg))
        for h in range(GLA_HEADS):
            o_ref[rows, h * GLA_DV:(h + 1) * GLA_DV] = o[h * c:(h + 1) * c].astype(o_ref.dtype)
        return carry

    lax.fori_loop(0, q_ref.shape[0] // c, chunk, 0)


def _gla(q, k, v, gg, la, g_out):
    s = q.shape[0]
    tm = min(GLA_TM, s)
    row = lambda w: pl.BlockSpec((tm, w), lambda i: (i, 0))
    return pl.pallas_call(
        _gla_kernel,
        grid=(s // tm,),
        in_specs=[row(GLA_QK_W), row(GLA_QK_W), row(GLA_V_W), row(GLA_V_W), row(GLA_QK_W),
                  pl.BlockSpec((1, GLA_DV), lambda i: (0, 0))],
        out_specs=row(GLA_V_W),
        out_shape=jax.ShapeDtypeStruct((s, GLA_V_W), BF16),
        scratch_shapes=[pltpu.VMEM((GLA_DV, GLA_QK_W), F32)],
        compiler_params=pltpu.CompilerParams(dimension_semantics=("arbitrary",)),
        name="gla",
    )(q, k, v, gg, la, g_out)


def _moba_kernel(qt_ref, k_ref, vt_ref, kmean_ref, o_ref, bias_ref, acc_ref, s_ref):
    qi = pl.program_id(0)
    heads, nb, _ = kmean_ref.shape
    tq = qt_ref.shape[2]

    blk = lax.broadcasted_iota(jnp.int32, (nb, tq), 0)
    for h in range(heads):
        gate = jnp.dot(kmean_ref[h], qt_ref[h].astype(F32), precision=HIGHEST,
                       preferred_element_type=F32)
        gate = jnp.where(blk < qi, gate, NEG_INF)
        sel = jnp.zeros((nb, tq), jnp.bool_)
        for _ in range(min(MOBA_TOPK, nb)):
            m = jnp.max(gate, axis=0, keepdims=True)
            first = jnp.min(jnp.where(gate == m, blk, nb), axis=0, keepdims=True)
            pick = (blk == first) & (m > NEG_INF)
            sel = sel | pick
            gate = jnp.where(pick, NEG_INF, gate)
        bias_ref[h] = jnp.where(sel, 0.0, NEG_INF)

    kk = lax.broadcasted_iota(jnp.int32, (k_ref.shape[2], tq), 0)
    qq = lax.broadcasted_iota(jnp.int32, (k_ref.shape[2], tq), 1)
    causal = kk <= qq
    stats = []
    for h in range(heads):
        s = jnp.dot(k_ref[h, qi], qt_ref[h], preferred_element_type=F32)
        s = jnp.where(causal, s, NEG_INF)
        m = jnp.max(s, axis=0, keepdims=True)
        p = jnp.exp2(s - m)
        stats += [m, jnp.sum(p, axis=0, keepdims=True)]
        acc_ref[h] = jnp.dot(vt_ref[h, qi], p.astype(BF16), preferred_element_type=F32)

    grp = MOBA_BLOCKS_PER_STEP
    ngrp = MOBA_STEPS_PER_ITER
    lag = MOBA_MATMUL_LAG
    span = grp * ngrp
    kblk = k_ref.shape[2]

    steps = [(h, g) for g in range(ngrp) for h in range(heads)]
    nsteps = len(steps)

    nslots = s_ref.shape[0]
    assert nsteps % nslots == 0 and nslots >= 2 * lag

    def score_matmul(base, n):
        h, g = steps[n % nsteps]
        blk0 = jnp.minimum(base + (n // nsteps) * span, nb - span) + g * grp
        kg = k_ref[h, pl.ds(blk0, grp)].reshape(grp * kblk, -1)
        s_ref[n % nslots] = jnp.dot(kg, qt_ref[h], preferred_element_type=F32)

    def past_blocks(it, carry):
        base = pl.multiple_of(it * span, span)
        stats = list(carry)
        for n, (h, g) in enumerate(steps):
            score_matmul(base, n + lag)
            j0 = base + g * grp
            m, l = stats[2 * h], stats[2 * h + 1]
            s = [s_ref[n % nslots, u * kblk:(u + 1) * kblk] for u in range(grp)]
            bias = [bias_ref[h, pl.ds(j0 + u, 1), :] for u in range(grp)]
            m_new = m
            for u in range(grp):
                m_new = jnp.maximum(m_new, jnp.max(s[u], axis=0, keepdims=True) + bias[u])
            alpha = jnp.exp2(m - m_new)
            p = [jnp.exp2(s[u] + (bias[u] - m_new)) for u in range(grp)]
            l = alpha * l
            for u in range(grp):
                l = l + jnp.sum(p[u], axis=0, keepdims=True)
            pb = jnp.concatenate([pu.astype(BF16) for pu in p], axis=0)
            vt = jnp.concatenate([vt_ref[h, j0 + u] for u in range(grp)], axis=1)
            acc_ref[h] = alpha * acc_ref[h] + jnp.dot(vt, pb, preferred_element_type=F32)
            stats[2 * h], stats[2 * h + 1] = m_new, l
        return tuple(stats)

    for n in range(lag):
        score_matmul(0, n)
    stats = lax.fori_loop(0, (qi + span - 1) // span, past_blocks, tuple(stats))
    hd = acc_ref.shape[1]
    for h in range(heads):
        o_ref[:, h * hd:(h + 1) * hd] = (acc_ref[h] / stats[2 * h + 1]).T.astype(o_ref.dtype)


def _moba(qt, kb, vt, kmean):
    heads, nb, hd, blk = qt.shape
    s = nb * blk
    assert nb % (MOBA_BLOCKS_PER_STEP * MOBA_STEPS_PER_ITER) == 0
    const = lambda shape: pl.BlockSpec(shape, lambda i: (0,) * len(shape),
                                       pipeline_mode=pl.Buffered(1))
    return pl.pallas_call(
        _moba_kernel,
        grid=(nb,),
        in_specs=[
            pl.BlockSpec((heads, None, hd, blk), lambda i: (0, i, 0, 0)),
            const((heads, nb, blk, hd)),
            const((heads, nb, hd, blk)),
            const((heads, nb, hd)),
        ],
        out_specs=pl.BlockSpec((blk, heads * hd), lambda i: (i, 0)),
        out_shape=jax.ShapeDtypeStruct((s, heads * hd), BF16),
        scratch_shapes=[
            pltpu.VMEM((heads, nb, blk), F32),
            pltpu.VMEM((heads, hd, blk), F32),
            pltpu.VMEM((2 * MOBA_MATMUL_LAG, MOBA_BLOCKS_PER_STEP * blk, blk), F32),
        ],
        compiler_params=pltpu.CompilerParams(
            dimension_semantics=("arbitrary",), vmem_limit_bytes=VMEM_LIMIT_BYTES),
        name="moba",
    )(qt, kb, vt, kmean)


def _out_mlp_kernel(x_ref, og_ref, om_ref, p_ref, wo_ref, gffn_ref, wup_ref, wdn_ref,
                    gpg_ref, wpg_ref, wpp_ref, gpe_ref, o_ref):
    h = x_ref[...]
    h = h + jnp.dot(og_ref[...], wo_ref[:GLA_V_W, :], preferred_element_type=F32)
    h = h + jnp.dot(om_ref[...], wo_ref[GLA_V_W:, :], preferred_element_type=F32)

    u = (_rms(h, -1) * gffn_ref[...]).astype(BF16)
    mlp = jnp.zeros_like(h)
    for c0 in range(0, D_FF, FF_CHUNK):
        f = jnp.dot(u, wup_ref[:, c0:c0 + FF_CHUNK], preferred_element_type=F32)
        f = jnp.square(jnp.maximum(f, 0.0)).astype(BF16)
        mlp = mlp + jnp.dot(f, wdn_ref[c0:c0 + FF_CHUNK, :], preferred_element_type=F32)
    h = h + mlp

    u = (_rms(h, -1) * gpg_ref[...]).astype(BF16)
    gate = jax.nn.sigmoid(jnp.dot(u, wpg_ref[...], preferred_element_type=F32))
    e = jnp.dot(p_ref[...].astype(BF16), wpp_ref[...], preferred_element_type=F32)
    e = _rms(e, -1) * gpe_ref[...]
    o_ref[...] = h + gate * e


def _out_mlp(x2, og, om, p2, wo, g_ffn, wup, wdn, g_pg, wpg, wpp, g_pe):
    s = x2.shape[0]
    tm = min(OUT_TM, s)
    const = lambda shape: pl.BlockSpec(shape, lambda i: (0,) * len(shape),
                                       pipeline_mode=pl.Buffered(1))
    row = lambda w: pl.BlockSpec((tm, w), lambda i: (i, 0))
    return pl.pallas_call(
        _out_mlp_kernel,
        grid=(s // tm,),
        in_specs=[
            row(D_MODEL), row(GLA_V_W), row(MOBA_W), row(PLE_DIM),
            const((GLA_V_W + MOBA_W, D_MODEL)),
            const((1, D_MODEL)),
            const((D_MODEL, D_FF)),
            const((D_FF, D_MODEL)),
            const((1, D_MODEL)),
            const((D_MODEL, D_MODEL)),
            const((PLE_DIM, D_MODEL)),
            const((1, D_MODEL)),
        ],
        out_specs=row(D_MODEL),
        out_shape=jax.ShapeDtypeStruct((s, D_MODEL), F32),
        compiler_params=pltpu.CompilerParams(
            dimension_semantics=("arbitrary",), vmem_limit_bytes=VMEM_LIMIT_BYTES),
        name="out_mlp",
    )(x2, og, om, p2, wo, g_ffn, wup, wdn, g_pg, wpg, wpp, g_pe)


def _layer(h2, p2, pos, g_mix, w_in, w_gla_a2, b_gla_a, g_gla_out, g_moba_q, g_moba_k,
           w_out, g_ffn, w_up, w_down, g_ple_gate, w_ple_gate, w_ple_proj, g_ple_emb):
    s = h2.shape[0]
    assert s % MOBA_BLOCK == 0 and s % GLA_CHUNK == 0
    assert s % min(IN_TM, s) == 0 and s % min(GLA_TM, s) == 0 and s % min(OUT_TM, s) == 0

    wa = jnp.concatenate(
        [w_in[:, :GLA_IN_W], jnp.zeros((D_MODEL, GA_PAD - GLA_LOWRANK), w_in.dtype)],
        axis=1).astype(BF16)
    wbt = w_in[:, GLA_IN_W:].T.astype(BF16)
    wa2 = jnp.concatenate(
        [w_gla_a2, jnp.zeros((GA_PAD - GLA_LOWRANK, GLA_QK_W), w_gla_a2.dtype)], axis=0)
    half = MOBA_HD // 2
    inv_col = (1.0 / (ROPE_THETA ** (jnp.arange(half, dtype=F32) / half))).reshape(half, 1)

    gq, gk, gv, gg, la, qt, kb, vt, kmean = _in_proj(
        h2, pos, g_mix.reshape(1, -1), wa, wbt, wa2, b_gla_a.reshape(1, -1),
        g_moba_q.reshape(-1, 1), g_moba_k.reshape(-1, 1), inv_col)
    o_gla = _gla(gq, gk, gv, gg, la, g_gla_out.reshape(1, -1))
    o_moba = _moba(qt, kb, vt, kmean.reshape(MOBA_HEADS, s // MOBA_BLOCK, MOBA_HD))
    return _out_mlp(
        h2, o_gla, o_moba, p2, w_out.astype(BF16), g_ffn.reshape(1, -1),
        w_up.astype(BF16), w_down.astype(BF16), g_ple_gate.reshape(1, -1),
        w_ple_gate.astype(BF16), w_ple_proj.astype(BF16), g_ple_emb.reshape(1, -1))


def kernel(x, p, positions, g_mix, w_in, w_gla_a2, b_gla_a, g_gla_out, g_moba_q, g_moba_k,
           w_out, g_ffn, w_up, w_down, g_ple_gate, w_ple_gate, w_ple_proj, g_ple_emb):
    depth = p.shape[0]
    batch = x.shape[0]
    outs = []
    for b in range(batch):
        h = x[b]
        pos = positions[b:b + 1]
        for i in range(depth):
            h = _layer(h, p[i, b], pos, g_mix[i], w_in[i], w_gla_a2[i], b_gla_a[i],
                       g_gla_out[i], g_moba_q[i], g_moba_k[i], w_out[i], g_ffn[i], w_up[i],
                       w_down[i], g_ple_gate[i], w_ple_gate[i], w_ple_proj[i], g_ple_emb[i])
        outs.append(h)
    return jnp.stack(outs, axis=0)
```

```python
import functools
import math

import numpy as np
import jax
import jax.numpy as jnp
from jax import lax
from jax.experimental import pallas as pl
from jax.experimental.pallas import tpu as pltpu

F32 = jnp.float32
BF16 = jnp.bfloat16
HIGHEST = lax.Precision.HIGHEST

D_MODEL = 1024
PLE_DIM = 256
GLA_HEADS = 4
GLA_DK = 64
GLA_DV = 128
GLA_LOWRANK = 16
GLA_TAU = 16.0
GLA_CHUNK = 64
MOBA_HEADS = 4
MOBA_HD = 128
MOBA_BLOCK = 256
MOBA_TOPK = 3
ROPE_THETA = 10000.0
D_FF = 4 * D_MODEL
EPS = 1e-6

GLA_QK_W = GLA_HEADS * GLA_DK
GLA_V_W = GLA_HEADS * GLA_DV
MOBA_W = MOBA_HEADS * MOBA_HD
GLA_IN_W = 2 * GLA_QK_W + 2 * GLA_V_W + GLA_LOWRANK
LANES = 128
BF16_SUBLANES = 16
GA_PAD = LANES
WA_W = 2 * GLA_QK_W + 2 * GLA_V_W + GA_PAD

VMEM_LIMIT_BYTES = 56 * 1024 * 1024

IN_TM = 512
GLA_TM = 512
OUT_TM = 512
FF_CHUNK = 1024
MOBA_BLOCKS_PER_STEP = 2
MOBA_STEPS_PER_ITER = 2
MOBA_MATMUL_LAG = 2

MOBA_QSCALE = (MOBA_HD ** -0.5) * math.log2(math.e)
NEG_INF = float("-inf")


def _rms(x, axis):
    return x * lax.rsqrt(jnp.mean(x * x, axis=axis, keepdims=True) + EPS)


def _log_sigmoid(a):
    return jnp.minimum(a, 0.0) - jnp.log1p(jnp.exp(-jnp.abs(a)))


def _dot_nt(a, b):
    return lax.dot_general(a, b, (((1,), (1,)), ((), ())), preferred_element_type=F32)


def _dot_tn(a, b):
    return lax.dot_general(a, b, (((0,), (0,)), ((), ())), preferred_element_type=F32)


def _in_proj_kernel(x_ref, pos_ref, gmix_ref, wa_ref, wbt_ref, wa2_ref, ba_ref,
                    gq_ref, gk_ref, inv_ref,
                    q_out, k_out, v_out, gg_out, la_out,
                    qt_out, kb_out, vt_out, kmean_out):
    tm = x_ref.shape[0]
    nblk = tm // MOBA_BLOCK
    u = (_rms(x_ref[...], -1) * gmix_ref[...]).astype(BF16)

    za = jnp.dot(u, wa_ref[...], preferred_element_type=F32)
    q_out[...] = za[:, 0:GLA_QK_W]
    k_out[...] = za[:, GLA_QK_W:2 * GLA_QK_W]
    v_out[...] = za[:, 2 * GLA_QK_W:2 * GLA_QK_W + GLA_V_W].astype(BF16)
    gg_out[...] = za[:, 2 * GLA_QK_W + GLA_V_W:2 * GLA_QK_W + 2 * GLA_V_W].astype(BF16)
    ga = za[:, 2 * GLA_QK_W + 2 * GLA_V_W:]
    a_pre = jnp.dot(ga, wa2_ref[...], precision=HIGHEST,
                    preferred_element_type=F32) + ba_ref[...]
    la_out[...] = _log_sigmoid(a_pre) * (1.0 / GLA_TAU)

    zt = _dot_nt(wbt_ref[...], u)
    ang = inv_ref[...] * pos_ref[...].astype(F32)
    cos = jnp.cos(ang)
    sin = jnp.sin(ang)
    half = MOBA_HD // 2

    def norm_rope(t, g_col):
        t = _rms(t, 0) * g_col
        t1, t2 = t[:half], t[half:]
        return jnp.concatenate([t1 * cos - t2 * sin, t2 * cos + t1 * sin], axis=0)

    for h in range(MOBA_HEADS):
        r0 = h * MOBA_HD
        qr = norm_rope(zt[r0:r0 + MOBA_HD], gq_ref[...]) * MOBA_QSCALE
        kr = norm_rope(zt[MOBA_W + r0:MOBA_W + r0 + MOBA_HD], gk_ref[...])
        krow = kr.T
        vt = zt[2 * MOBA_W + r0:2 * MOBA_W + r0 + MOBA_HD]
        for t in range(nblk):
            c0 = t * MOBA_BLOCK
            qt_out[h, t] = qr[:, c0:c0 + MOBA_BLOCK].astype(BF16)
            vt_out[h, t] = vt[:, c0:c0 + MOBA_BLOCK].astype(BF16)
            kblk = krow[c0:c0 + MOBA_BLOCK]
            kb_out[h, t] = kblk.astype(BF16)
            kmean_out[h, t] = jnp.mean(kblk, axis=0, keepdims=True)


def _in_proj(x2, pos, g_mix, wa, wbt, wa2, ba, gq_col, gk_col, inv_col):
    s = x2.shape[0]
    tm = min(IN_TM, s)
    nb = s // MOBA_BLOCK
    nblk = tm // MOBA_BLOCK
    const = lambda shape: pl.BlockSpec(shape, lambda i: (0,) * len(shape),
                                       pipeline_mode=pl.Buffered(1))
    row = lambda w: pl.BlockSpec((tm, w), lambda i: (i, 0))
    hblk = lambda a, b: pl.BlockSpec((MOBA_HEADS, nblk, a, b), lambda i: (0, i, 0, 0))
    return pl.pallas_call(
        _in_proj_kernel,
        grid=(s // tm,),
        in_specs=[
            row(D_MODEL),
            pl.BlockSpec((1, tm), lambda i: (0, i)),
            const((1, D_MODEL)),
            const((D_MODEL, WA_W)),
            const((3 * MOBA_W, D_MODEL)),
            const((GA_PAD, GLA_QK_W)),
            const((1, GLA_QK_W)),
            const((MOBA_HD, 1)),
            const((MOBA_HD, 1)),
            const((MOBA_HD // 2, 1)),
        ],
        out_specs=[
            row(GLA_QK_W), row(GLA_QK_W), row(GLA_V_W), row(GLA_V_W), row(GLA_QK_W),
            hblk(MOBA_HD, MOBA_BLOCK), hblk(MOBA_BLOCK, MOBA_HD), hblk(MOBA_HD, MOBA_BLOCK),
            hblk(1, MOBA_HD),
        ],
        out_shape=[
            jax.ShapeDtypeStruct((s, GLA_QK_W), F32),
            jax.ShapeDtypeStruct((s, GLA_QK_W), F32),
            jax.ShapeDtypeStruct((s, GLA_V_W), BF16),
            jax.ShapeDtypeStruct((s, GLA_V_W), BF16),
            jax.ShapeDtypeStruct((s, GLA_QK_W), F32),
            jax.ShapeDtypeStruct((MOBA_HEADS, nb, MOBA_HD, MOBA_BLOCK), BF16),
            jax.ShapeDtypeStruct((MOBA_HEADS, nb, MOBA_BLOCK, MOBA_HD), BF16),
            jax.ShapeDtypeStruct((MOBA_HEADS, nb, MOBA_HD, MOBA_BLOCK), BF16),
            jax.ShapeDtypeStruct((MOBA_HEADS, nb, 1, MOBA_HD), F32),
        ],
        compiler_params=pltpu.CompilerParams(
            dimension_semantics=("arbitrary",), vmem_limit_bytes=VMEM_LIMIT_BYTES),
        name="in_proj",
    )(x2, pos, g_mix, wa, wbt, wa2, ba, gq_col, gk_col, inv_col)


def _gla_kernel(q_ref, k_ref, v_ref, gg_ref, la_ref, gout_ref, o_ref, st_ref):
    c = GLA_CHUNK
    hc = GLA_HEADS * c

    @pl.when(pl.program_id(0) == 0)
    def _():
        st_ref[...] = jnp.zeros_like(st_ref)

    r = lax.broadcasted_iota(jnp.int32, (hc, GLA_QK_W), 0)
    l = lax.broadcasted_iota(jnp.int32, (hc, GLA_QK_W), 1)
    same_head = (r // c) == (l // GLA_DK)
    causal = same_head & ((l % c) <= (r % c))
    tr = lax.broadcasted_iota(jnp.int32, (c, c), 0)
    tc = lax.broadcasted_iota(jnp.int32, (c, c), 1)
    tri = (tc <= tr).astype(F32)
    scale = GLA_DK ** -0.5

    def stack_heads(a, width):
        return jnp.concatenate([a[:, h * width:(h + 1) * width] for h in range(GLA_HEADS)], axis=0)

    def block_diag(a):
        return jnp.where(same_head, jnp.concatenate([a] * GLA_HEADS, axis=0), 0.0).astype(BF16)

    def chunk(ci, carry):
        rows = pl.ds(pl.multiple_of(ci * c, c), c)
        b = jnp.dot(tri, la_ref[rows, :], precision=HIGHEST, preferred_element_type=F32)
        b_last = b[c - 1:c, :]
        q = q_ref[rows, :]
        k = k_ref[rows, :]
        qb = block_diag(q * jnp.exp(b) * scale)
        kb = block_diag(k * jnp.exp(-b))
        keb = block_diag(k * jnp.exp(b_last - b))
        vst = stack_heads(v_ref[rows, :], GLA_DV)
        a = jnp.where(causal, _dot_nt(qb, kb), 0.0).astype(BF16)
        st = st_ref[...]
        o = jnp.dot(a, vst, preferred_element_type=F32) + _dot_nt(qb, st.astype(BF16))
        st_ref[...] = st * jnp.exp(b_last) + _dot_tn(vst, keb)
        o = _rms(o, -1) * gout_ref[...]
        gg = stack_heads(gg_ref[rows, :], GLA_DV).astype(F32)
        o = o * (gg * jax.nn.sigmoid(gg))
        for h in range(GLA_HEADS):
            o_ref[rows, h * GLA_DV:(h + 1) * GLA_DV] = o[h * c:(h + 1) * c].astype(o_ref.dtype)
        return carry

    lax.fori_loop(0, q_ref.shape[0] // c, chunk, 0)


def _gla(q, k, v, gg, la, g_out):
    s = q.shape[0]
    tm = min(GLA_TM, s)
    row = lambda w: pl.BlockSpec((tm, w), lambda i: (i, 0))
    return pl.pallas_call(
        _gla_kernel,
        grid=(s // tm,),
        in_specs=[row(GLA_QK_W), row(GLA_QK_W), row(GLA_V_W), row(GLA_V_W), row(GLA_QK_W),
                  pl.BlockSpec((1, GLA_DV), lambda i: (0, 0))],
        out_specs=row(GLA_V_W),
        out_shape=jax.ShapeDtypeStruct((s, GLA_V_W), BF16),
        scratch_shapes=[pltpu.VMEM((GLA_DV, GLA_QK_W), F32)],
        compiler_params=pltpu.CompilerParams(dimension_semantics=("arbitrary",)),
        name="gla",
    )(q, k, v, gg, la, g_out)


def _moba_kernel(qt_ref, k_ref, vt_ref, kmean_ref, o_ref, bias_ref, acc_ref, s_ref):
    qi = pl.program_id(0)
    heads, nb, hd = kmean_ref.shape
    tq = qt_ref.shape[2]
    kblk = k_ref.shape[2]
    grp = MOBA_BLOCKS_PER_STEP
    ngrp = MOBA_STEPS_PER_ITER
    lag = MOBA_MATMUL_LAG
    span = grp * ngrp

    steps = [(h, g) for g in range(ngrp) for h in range(heads)]
    nsteps = len(steps)

    nslots = s_ref.shape[0]
    assert nsteps % nslots == 0 and nslots >= 2 * lag

    def score_matmul(base, n):
        h, g = steps[n % nsteps]
        blk0 = jnp.minimum(base + (n // nsteps) * span, nb - span) + g * grp
        kg = k_ref[h, pl.ds(blk0, grp)].reshape(grp * kblk, -1)
        s_ref[n % nslots] = jnp.dot(kg, qt_ref[h], preferred_element_type=F32)

    pad = acc_ref.shape[1] - hd

    def values_t(blocks):
        keys = len(blocks) * kblk
        first_row = lax.broadcasted_iota(jnp.int32, (pad, keys), 0) == 0
        ones_rows = jnp.where(first_row, 1.0, 0.0).astype(BF16)
        return jnp.concatenate([jnp.concatenate(blocks, axis=1), ones_rows], axis=0)

    gates = []
    for h in range(heads):
        km = kmean_ref[h]
        hi = km.astype(BF16)
        r1 = km - hi.astype(F32)
        mid = r1.astype(BF16)
        lo = (r1 - mid.astype(F32)).astype(BF16)
        qt = qt_ref[h]
        gates.append(jnp.dot(hi, qt, preferred_element_type=F32)
                     + jnp.dot(mid, qt, preferred_element_type=F32)
                     + jnp.dot(lo, qt, preferred_element_type=F32))
    own = [jnp.dot(k_ref[h, qi], qt_ref[h], preferred_element_type=F32) for h in range(heads)]
    for n in range(lag):
        score_matmul(0, n)

    blk = lax.broadcasted_iota(jnp.int32, (nb, tq), 0)
    for h in range(heads):
        gate = jnp.where(blk < qi, gates[h], NEG_INF)
        sel = jnp.zeros((nb, tq), jnp.bool_)
        for _ in range(min(MOBA_TOPK, nb)):
            m = jnp.max(gate, axis=0, keepdims=True)
            first = jnp.min(jnp.where(gate == m, blk, nb), axis=0, keepdims=True)
            pick = (blk == first) & (m > NEG_INF)
            sel = sel | pick
            gate = jnp.where(pick, NEG_INF, gate)
        bias_ref[h] = jnp.where(sel, 0.0, NEG_INF)

    kk = lax.broadcasted_iota(jnp.int32, (kblk, tq), 0)
    qq = lax.broadcasted_iota(jnp.int32, (kblk, tq), 1)
    causal = kk <= qq
    stats, probs = [], []
    for h in range(heads):
        s = jnp.where(causal, own[h], NEG_INF)
        m = jnp.max(s, axis=0, keepdims=True)
        stats.append(m)
        probs.append(jnp.exp2(s - m).astype(BF16))
    for h in range(heads):
        acc_ref[h] = jnp.dot(values_t([vt_ref[h, qi]]), probs[h], preferred_element_type=F32)

    def past_blocks(it, carry):
        base = pl.multiple_of(it * span, span)
        stats = list(carry)
        for n, (h, g) in enumerate(steps):
            score_matmul(base, n + lag)
            j0 = base + g * grp
            m = stats[h]
            s = [s_ref[n % nslots, u * kblk:(u + 1) * kblk] for u in range(grp)]
            bias = [bias_ref[h, pl.ds(j0 + u, 1), :] for u in range(grp)]
            m_new = m
            for u in range(grp):
                m_new = jnp.maximum(m_new, jnp.max(s[u], axis=0, keepdims=True) + bias[u])
            alpha = jnp.exp2(m - m_new)
            pb = jnp.concatenate(
                [jnp.exp2(s[u] + (bias[u] - m_new)).astype(BF16) for u in range(grp)], axis=0)
            vt = values_t([vt_ref[h, j0 + u] for u in range(grp)])
            acc_ref[h] = alpha * acc_ref[h] + jnp.dot(vt, pb, preferred_element_type=F32)
            stats[h] = m_new
        return tuple(stats)

    lax.fori_loop(0, (qi + span - 1) // span, past_blocks, tuple(stats))
    for h in range(heads):
        acc = acc_ref[h]
        o_ref[:, h * hd:(h + 1) * hd] = (acc[:hd] / acc[hd:hd + 1]).T.astype(o_ref.dtype)


def _moba(qt, kb, vt, kmean):
    heads, nb, hd, blk = qt.shape
    s = nb * blk
    assert nb % (MOBA_BLOCKS_PER_STEP * MOBA_STEPS_PER_ITER) == 0
    const = lambda shape: pl.BlockSpec(shape, lambda i: (0,) * len(shape),
                                       pipeline_mode=pl.Buffered(1))
    return pl.pallas_call(
        _moba_kernel,
        grid=(nb,),
        in_specs=[
            pl.BlockSpec((heads, None, hd, blk), lambda i: (0, i, 0, 0)),
            const((heads, nb, blk, hd)),
            const((heads, nb, hd, blk)),
            const((heads, nb, hd)),
        ],
        out_specs=pl.BlockSpec((blk, heads * hd), lambda i: (i, 0)),
        out_shape=jax.ShapeDtypeStruct((s, heads * hd), BF16),
        scratch_shapes=[
            pltpu.VMEM((heads, nb, blk), F32),
            pltpu.VMEM((heads, hd + BF16_SUBLANES, blk), F32),
            pltpu.VMEM((2 * MOBA_MATMUL_LAG, MOBA_BLOCKS_PER_STEP * blk, blk), F32),
        ],
        compiler_params=pltpu.CompilerParams(
            dimension_semantics=("arbitrary",), vmem_limit_bytes=VMEM_LIMIT_BYTES),
        name="moba",
    )(qt, kb, vt, kmean)


def _out_mlp_kernel(x_ref, og_ref, om_ref, p_ref, wo_ref, gffn_ref, wup_ref, wdn_ref,
                    gpg_ref, wpg_ref, wpp_ref, gpe_ref, o_ref):
    h = x_ref[...]
    h = h + jnp.dot(og_ref[...], wo_ref[:GLA_V_W, :], preferred_element_type=F32)
    h = h + jnp.dot(om_ref[...], wo_ref[GLA_V_W:, :], preferred_element_type=F32)

    u = (_rms(h, -1) * gffn_ref[...]).astype(BF16)
    mlp = jnp.zeros_like(h)
    for c0 in range(0, D_FF, FF_CHUNK):
        f = jnp.dot(u, wup_ref[:, c0:c0 + FF_CHUNK], preferred_element_type=F32)
        f = jnp.square(jnp.maximum(f, 0.0)).astype(BF16)
        mlp = mlp + jnp.dot(f, wdn_ref[c0:c0 + FF_CHUNK, :], preferred_element_type=F32)
    h = h + mlp

    u = (_rms(h, -1) * gpg_ref[...]).astype(BF16)
    gate = jax.nn.sigmoid(jnp.dot(u, wpg_ref[...], preferred_element_type=F32))
    e = jnp.dot(p_ref[...].astype(BF16), wpp_ref[...], preferred_element_type=F32)
    e = _rms(e, -1) * gpe_ref[...]
    o_ref[...] = h + gate * e


def _out_mlp(x2, og, om, p2, wo, g_ffn, wup, wdn, g_pg, wpg, wpp, g_pe):
    s = x2.shape[0]
    tm = min(OUT_TM, s)
    const = lambda shape: pl.BlockSpec(shape, lambda i: (0,) * len(shape),
                                       pipeline_mode=pl.Buffered(1))
    row = lambda w: pl.BlockSpec((tm, w), lambda i: (i, 0))
    return pl.pallas_call(
        _out_mlp_kernel,
        grid=(s // tm,),
        in_specs=[
            row(D_MODEL), row(GLA_V_W), row(MOBA_W), row(PLE_DIM),
            const((GLA_V_W + MOBA_W, D_MODEL)),
            const((1, D_MODEL)),
            const((D_MODEL, D_FF)),
            const((D_FF, D_MODEL)),
            const((1, D_MODEL)),
            const((D_MODEL, D_MODEL)),
            const((PLE_DIM, D_MODEL)),
            const((1, D_MODEL)),
        ],
        out_specs=row(D_MODEL),
        out_shape=jax.ShapeDtypeStruct((s, D_MODEL), F32),
        compiler_params=pltpu.CompilerParams(
            dimension_semantics=("arbitrary",), vmem_limit_bytes=VMEM_LIMIT_BYTES),
        name="out_mlp",
    )(x2, og, om, p2, wo, g_ffn, wup, wdn, g_pg, wpg, wpp, g_pe)


def _layer(h2, p2, pos, g_mix, w_in, w_gla_a2, b_gla_a, g_gla_out, g_moba_q, g_moba_k,
           w_out, g_ffn, w_up, w_down, g_ple_gate, w_ple_gate, w_ple_proj, g_ple_emb):
    s = h2.shape[0]
    assert s % MOBA_BLOCK == 0 and s % GLA_CHUNK == 0
    assert s % min(IN_TM, s) == 0 and s % min(GLA_TM, s) == 0 and s % min(OUT_TM, s) == 0

    wa = jnp.concatenate(
        [w_in[:, :GLA_IN_W], jnp.zeros((D_MODEL, GA_PAD - GLA_LOWRANK), w_in.dtype)],
        axis=1).astype(BF16)
    wbt = w_in[:, GLA_IN_W:].T.astype(BF16)
    wa2 = jnp.concatenate(
        [w_gla_a2, jnp.zeros((GA_PAD - GLA_LOWRANK, GLA_QK_W), w_gla_a2.dtype)], axis=0)
    half = MOBA_HD // 2
    inv_col = (1.0 / (ROPE_THETA ** (jnp.arange(half, dtype=F32) / half))).reshape(half, 1)

    gq, gk, gv, gg, la, qt, kb, vt, kmean = _in_proj(
        h2, pos, g_mix.reshape(1, -1), wa, wbt, wa2, b_gla_a.reshape(1, -1),
        g_moba_q.reshape(-1, 1), g_moba_k.reshape(-1, 1), inv_col)
    o_gla = _gla(gq, gk, gv, gg, la, g_gla_out.reshape(1, -1))
    o_moba = _moba(qt, kb, vt, kmean.reshape(MOBA_HEADS, s // MOBA_BLOCK, MOBA_HD))
    return _out_mlp(
        h2, o_gla, o_moba, p2, w_out.astype(BF16), g_ffn.reshape(1, -1),
        w_up.astype(BF16), w_down.astype(BF16), g_ple_gate.reshape(1, -1),
        w_ple_gate.astype(BF16), w_ple_proj.astype(BF16), g_ple_emb.reshape(1, -1))


def kernel(x, p, positions, g_mix, w_in, w_gla_a2, b_gla_a, g_gla_out, g_moba_q, g_moba_k,
           w_out, g_ffn, w_up, w_down, g_ple_gate, w_ple_gate, w_ple_proj, g_ple_emb):
    depth = p.shape[0]
    batch = x.shape[0]
    outs = []
    for b in range(batch):
        h = x[b]
        pos = positions[b:b + 1]
        for i in range(depth):
            h = _layer(h, p[i, b], pos, g_mix[i], w_in[i], w_gla_a2[i], b_gla_a[i],
                       g_gla_out[i], g_moba_q[i], g_moba_k[i], w_out[i], g_ffn[i], w_up[i],
                       w_down[i], g_ple_gate[i], w_ple_gate[i], w_ple_proj[i], g_ple_emb[i])
        outs.append(h)
    return jnp.stack(outs, axis=0)
```

```python
import functools
import math

import numpy as np
import jax
import jax.numpy as jnp
from jax import lax
from jax.experimental import pallas as pl
from jax.experimental.pallas import tpu as pltpu

F32 = jnp.float32
BF16 = jnp.bfloat16
HIGHEST = lax.Precision.HIGHEST

D_MODEL = 1024
PLE_DIM = 256
GLA_HEADS = 4
GLA_DK = 64
GLA_DV = 128
GLA_LOWRANK = 16
GLA_TAU = 16.0
GLA_CHUNK = 64
MOBA_HEADS = 4
MOBA_HD = 128
MOBA_BLOCK = 256
MOBA_TOPK = 3
ROPE_THETA = 10000.0
D_FF = 4 * D_MODEL
EPS = 1e-6

GLA_QK_W = GLA_HEADS * GLA_DK
GLA_V_W = GLA_HEADS * GLA_DV
MOBA_W = MOBA_HEADS * MOBA_HD
GLA_IN_W = 2 * GLA_QK_W + 2 * GLA_V_W + GLA_LOWRANK
LANES = 128
BF16_SUBLANES = 16
GA_PAD = LANES
WA_W = 2 * GLA_QK_W + 2 * GLA_V_W + GA_PAD

VMEM_LIMIT_BYTES = 56 * 1024 * 1024

IN_TM = 512
GLA_CHUNKS_PER_STEP = 4
GLA_TM = 1024
OUT_TM = 512
FF_CHUNK = 1024
MOBA_BLOCKS_PER_STEP = 2
MOBA_STEPS_PER_ITER = 2
MOBA_MATMUL_LAG = 2

MOBA_QSCALE = (MOBA_HD ** -0.5) * math.log2(math.e)
NEG_INF = float("-inf")


def _rms(x, axis):
    return x * lax.rsqrt(jnp.mean(x * x, axis=axis, keepdims=True) + EPS)


def _log_sigmoid(a):
    return jnp.minimum(a, 0.0) - jnp.log1p(jnp.exp(-jnp.abs(a)))


def _dot_nt(a, b):
    return lax.dot_general(a, b, (((1,), (1,)), ((), ())), preferred_element_type=F32)


def _dot_tn(a, b):
    return lax.dot_general(a, b, (((0,), (0,)), ((), ())), preferred_element_type=F32)


def _in_proj_kernel(x_ref, pos_ref, gmix_ref, wa_ref, wbt_ref, wa2_ref, ba_ref,
                    gq_ref, gk_ref, inv_ref,
                    q_out, k_out, v_out, gg_out, la_out,
                    qt_out, kb_out, vt_out, kmean_out):
    tm = x_ref.shape[0]
    nblk = tm // MOBA_BLOCK
    u = (_rms(x_ref[...], -1) * gmix_ref[...]).astype(BF16)

    za = jnp.dot(u, wa_ref[...], preferred_element_type=F32)
    q_out[...] = za[:, 0:GLA_QK_W]
    k_out[...] = za[:, GLA_QK_W:2 * GLA_QK_W]
    v_out[...] = za[:, 2 * GLA_QK_W:2 * GLA_QK_W + GLA_V_W].astype(BF16)
    gg_out[...] = za[:, 2 * GLA_QK_W + GLA_V_W:2 * GLA_QK_W + 2 * GLA_V_W].astype(BF16)
    ga = za[:, 2 * GLA_QK_W + 2 * GLA_V_W:]
    a_pre = jnp.dot(ga, wa2_ref[...], precision=HIGHEST,
                    preferred_element_type=F32) + ba_ref[...]
    la_out[...] = _log_sigmoid(a_pre) * (1.0 / GLA_TAU)

    zt = _dot_nt(wbt_ref[...], u)
    ang = inv_ref[...] * pos_ref[...].astype(F32)
    cos = jnp.cos(ang)
    sin = jnp.sin(ang)
    half = MOBA_HD // 2

    def norm_rope(t, g_col):
        t = _rms(t, 0) * g_col
        t1, t2 = t[:half], t[half:]
        return jnp.concatenate([t1 * cos - t2 * sin, t2 * cos + t1 * sin], axis=0)

    for h in range(MOBA_HEADS):
        r0 = h * MOBA_HD
        qr = norm_rope(zt[r0:r0 + MOBA_HD], gq_ref[...]) * MOBA_QSCALE
        kr = norm_rope(zt[MOBA_W + r0:MOBA_W + r0 + MOBA_HD], gk_ref[...])
        krow = kr.T
        vt = zt[2 * MOBA_W + r0:2 * MOBA_W + r0 + MOBA_HD]
        for t in range(nblk):
            c0 = t * MOBA_BLOCK
            qt_out[h, t] = qr[:, c0:c0 + MOBA_BLOCK].astype(BF16)
            vt_out[h, t] = vt[:, c0:c0 + MOBA_BLOCK].astype(BF16)
            kblk = krow[c0:c0 + MOBA_BLOCK]
            kb_out[h, t] = kblk.astype(BF16)
            kmean_out[h, t] = jnp.mean(kblk, axis=0, keepdims=True)


def _in_proj(x2, pos, g_mix, wa, wbt, wa2, ba, gq_col, gk_col, inv_col):
    s = x2.shape[0]
    tm = min(IN_TM, s)
    nb = s // MOBA_BLOCK
    nblk = tm // MOBA_BLOCK
    const = lambda shape: pl.BlockSpec(shape, lambda i: (0,) * len(shape),
                                       pipeline_mode=pl.Buffered(1))
    row = lambda w: pl.BlockSpec((tm, w), lambda i: (i, 0))
    hblk = lambda a, b: pl.BlockSpec((MOBA_HEADS, nblk, a, b), lambda i: (0, i, 0, 0))
    return pl.pallas_call(
        _in_proj_kernel,
        grid=(s // tm,),
        in_specs=[
            row(D_MODEL),
            pl.BlockSpec((1, tm), lambda i: (0, i)),
            const((1, D_MODEL)),
            const((D_MODEL, WA_W)),
            const((3 * MOBA_W, D_MODEL)),
            const((GA_PAD, GLA_QK_W)),
            const((1, GLA_QK_W)),
            const((MOBA_HD, 1)),
            const((MOBA_HD, 1)),
            const((MOBA_HD // 2, 1)),
        ],
        out_specs=[
            row(GLA_QK_W), row(GLA_QK_W), row(GLA_V_W), row(GLA_V_W), row(GLA_QK_W),
            hblk(MOBA_HD, MOBA_BLOCK), hblk(MOBA_BLOCK, MOBA_HD), hblk(MOBA_HD, MOBA_BLOCK),
            hblk(1, MOBA_HD),
        ],
        out_shape=[
            jax.ShapeDtypeStruct((s, GLA_QK_W), F32),
            jax.ShapeDtypeStruct((s, GLA_QK_W), F32),
            jax.ShapeDtypeStruct((s, GLA_V_W), BF16),
            jax.ShapeDtypeStruct((s, GLA_V_W), BF16),
            jax.ShapeDtypeStruct((s, GLA_QK_W), F32),
            jax.ShapeDtypeStruct((MOBA_HEADS, nb, MOBA_HD, MOBA_BLOCK), BF16),
            jax.ShapeDtypeStruct((MOBA_HEADS, nb, MOBA_BLOCK, MOBA_HD), BF16),
            jax.ShapeDtypeStruct((MOBA_HEADS, nb, MOBA_HD, MOBA_BLOCK), BF16),
            jax.ShapeDtypeStruct((MOBA_HEADS, nb, 1, MOBA_HD), F32),
        ],
        compiler_params=pltpu.CompilerParams(
            dimension_semantics=("arbitrary",), vmem_limit_bytes=VMEM_LIMIT_BYTES),
        name="in_proj",
    )(x2, pos, g_mix, wa, wbt, wa2, ba, gq_col, gk_col, inv_col)


def _gla_kernel(q_ref, k_ref, v_ref, gg_ref, la_ref, gout_ref, o_ref, st_ref):
    c = GLA_CHUNK
    hc = GLA_HEADS * c

    @pl.when(pl.program_id(0) == 0)
    def _():
        st_ref[...] = jnp.zeros_like(st_ref)

    r = lax.broadcasted_iota(jnp.int32, (hc, GLA_QK_W), 0)
    l = lax.broadcasted_iota(jnp.int32, (hc, GLA_QK_W), 1)
    same_head = (r // c) == (l // GLA_DK)
    causal = same_head & ((l % c) <= (r % c))
    ng = GLA_CHUNKS_PER_STEP
    gr = ng * c
    tr = lax.broadcasted_iota(jnp.int32, (gr, gr), 0)
    tc = lax.broadcasted_iota(jnp.int32, (gr, gr), 1)
    tri = ((tr // c == tc // c) & (tc <= tr)).astype(BF16)
    scale = GLA_DK ** -0.5

    def stack_heads(a, width):
        return jnp.concatenate([a[:, h * width:(h + 1) * width] for h in range(GLA_HEADS)], axis=0)

    def block_diag(a):
        return jnp.where(same_head, jnp.concatenate([a] * GLA_HEADS, axis=0), 0.0).astype(BF16)

    chunks = [slice(i * c, (i + 1) * c) for i in range(ng)]

    def cumsum_stage(g):
        la = la_ref[g * gr:(g + 1) * gr, :]
        hi = la.astype(BF16)
        r1 = la - hi.astype(F32)
        mid = r1.astype(BF16)
        lo = (r1 - mid.astype(F32)).astype(BF16)
        return (jnp.dot(tri, hi, preferred_element_type=F32)
                + jnp.dot(tri, mid, preferred_element_type=F32)
                + jnp.dot(tri, lo, preferred_element_type=F32))

    def score_stage(g, b):
        q = q_ref[g * gr:(g + 1) * gr, :]
        k = k_ref[g * gr:(g + 1) * gr, :]
        v = v_ref[g * gr:(g + 1) * gr, :]
        b_last = [b[(i + 1) * c - 1:(i + 1) * c, :] for i in range(ng)]
        qb = [block_diag(q[ch] * jnp.exp(b[ch]) * scale) for ch in chunks]
        kb = [block_diag(k[ch] * jnp.exp(-b[ch])) for ch in chunks]
        keb = [block_diag(k[ch] * jnp.exp(b_last[i] - b[ch])) for i, ch in enumerate(chunks)]
        vst = [stack_heads(v[ch], GLA_DV) for ch in chunks]
        att = [_dot_nt(qb[i], kb[i]) for i in range(ng)]
        dst = [_dot_tn(vst[i], keb[i]) for i in range(ng)]
        decay = [jnp.exp(bl) for bl in b_last]
        return qb, vst, att, dst, decay

    def output_stage(qb, vst, att, dst, decay):
        att = [jnp.where(causal, a, 0.0).astype(BF16) for a in att]
        intra = [jnp.dot(att[i], vst[i], preferred_element_type=F32) for i in range(ng)]
        st = st_ref[...]
        inter = []
        for i in range(ng):
            inter.append(_dot_nt(qb[i], st.astype(BF16)))
            st = st * decay[i] + dst[i]
        st_ref[...] = st
        return [intra[i] + inter[i] for i in range(ng)]

    def store_stage(g, outs):
        gg = gg_ref[g * gr:(g + 1) * gr, :]
        for i, ch in enumerate(chunks):
            o = _rms(outs[i], -1) * gout_ref[...]
            gate = stack_heads(gg[ch], GLA_DV).astype(F32)
            o = o * (gate * jax.nn.sigmoid(gate))
            for h in range(GLA_HEADS):
                o_ref[g * gr + i * c:g * gr + (i + 1) * c, h * GLA_DV:(h + 1) * GLA_DV] = (
                    o[h * c:(h + 1) * c].astype(o_ref.dtype))

    ngroups = q_ref.shape[0] // gr
    cums, scored, outs = {}, {}, {}
    for t in range(ngroups + 3):
        if t < ngroups:
            cums[t] = cumsum_stage(t)
        if 0 <= t - 1 < ngroups:
            scored[t - 1] = score_stage(t - 1, cums.pop(t - 1))
        if 0 <= t - 2 < ngroups:
            outs[t - 2] = output_stage(*scored.pop(t - 2))
        if 0 <= t - 3 < ngroups:
            store_stage(t - 3, outs.pop(t - 3))


def _gla(q, k, v, gg, la, g_out):
    s = q.shape[0]
    tm = min(GLA_TM, s)
    row = lambda w: pl.BlockSpec((tm, w), lambda i: (i, 0))
    return pl.pallas_call(
        _gla_kernel,
        grid=(s // tm,),
        in_specs=[row(GLA_QK_W), row(GLA_QK_W), row(GLA_V_W), row(GLA_V_W), row(GLA_QK_W),
                  pl.BlockSpec((1, GLA_DV), lambda i: (0, 0))],
        out_specs=row(GLA_V_W),
        out_shape=jax.ShapeDtypeStruct((s, GLA_V_W), BF16),
        scratch_shapes=[pltpu.VMEM((GLA_DV, GLA_QK_W), F32)],
        compiler_params=pltpu.CompilerParams(dimension_semantics=("arbitrary",)),
        name="gla",
    )(q, k, v, gg, la, g_out)


def _moba_kernel(qt_ref, k_ref, vt_ref, kmean_ref, o_ref, bias_ref, acc_ref, s_ref):
    qi = pl.program_id(0)
    heads, nb, hd = kmean_ref.shape
    tq = qt_ref.shape[2]
    kblk = k_ref.shape[2]
    grp = MOBA_BLOCKS_PER_STEP
    ngrp = MOBA_STEPS_PER_ITER
    lag = MOBA_MATMUL_LAG
    span = grp * ngrp

    steps = [(h, g) for g in range(ngrp) for h in range(heads)]
    nsteps = len(steps)

    nslots = s_ref.shape[0]
    assert nsteps % nslots == 0 and nslots >= 2 * lag

    def score_matmul(base, n):
        h, g = steps[n % nsteps]
        blk0 = jnp.minimum(base + (n // nsteps) * span, nb - span) + g * grp
        kg = k_ref[h, pl.ds(blk0, grp)].reshape(grp * kblk, -1)
        s_ref[n % nslots] = jnp.dot(kg, qt_ref[h], preferred_element_type=F32)

    pad = acc_ref.shape[1] - hd

    def values_t(blocks):
        keys = len(blocks) * kblk
        first_row = lax.broadcasted_iota(jnp.int32, (pad, keys), 0) == 0
        ones_rows = jnp.where(first_row, 1.0, 0.0).astype(BF16)
        return jnp.concatenate([jnp.concatenate(blocks, axis=1), ones_rows], axis=0)

    gates = []
    for h in range(heads):
        km = kmean_ref[h]
        hi = km.astype(BF16)
        r1 = km - hi.astype(F32)
        mid = r1.astype(BF16)
        lo = (r1 - mid.astype(F32)).astype(BF16)
        qt = qt_ref[h]
        gates.append(jnp.dot(hi, qt, preferred_element_type=F32)
                     + jnp.dot(mid, qt, preferred_element_type=F32)
                     + jnp.dot(lo, qt, preferred_element_type=F32))
    own = [jnp.dot(k_ref[h, qi], qt_ref[h], preferred_element_type=F32) for h in range(heads)]
    for n in range(lag):
        score_matmul(0, n)

    blk = lax.broadcasted_iota(jnp.int32, (nb, tq), 0)
    for h in range(heads):
        gate = jnp.where(blk < qi, gates[h], NEG_INF)
        sel = jnp.zeros((nb, tq), jnp.bool_)
        for _ in range(min(MOBA_TOPK, nb)):
            m = jnp.max(gate, axis=0, keepdims=True)
            first = jnp.min(jnp.where(gate == m, blk, nb), axis=0, keepdims=True)
            pick = (blk == first) & (m > NEG_INF)
            sel = sel | pick
            gate = jnp.where(pick, NEG_INF, gate)
        bias_ref[h] = jnp.where(sel, 0.0, NEG_INF)

    kk = lax.broadcasted_iota(jnp.int32, (kblk, tq), 0)
    qq = lax.broadcasted_iota(jnp.int32, (kblk, tq), 1)
    causal = kk <= qq
    stats, probs = [], []
    for h in range(heads):
        s = jnp.where(causal, own[h], NEG_INF)
        m = jnp.max(s, axis=0, keepdims=True)
        stats.append(m)
        probs.append(jnp.exp2(s - m).astype(BF16))
    for h in range(heads):
        acc_ref[h] = jnp.dot(values_t([vt_ref[h, qi]]), probs[h], preferred_element_type=F32)

    def past_blocks(it, carry):
        base = pl.multiple_of(it * span, span)
        stats = list(carry)
        for n, (h, g) in enumerate(steps):
            score_matmul(base, n + lag)
            j0 = base + g * grp
            m = stats[h]
            s = [s_ref[n % nslots, u * kblk:(u + 1) * kblk] for u in range(grp)]
            bias = [bias_ref[h, pl.ds(j0 + u, 1), :] for u in range(grp)]
            m_new = m
            for u in range(grp):
                m_new = jnp.maximum(m_new, jnp.max(s[u], axis=0, keepdims=True) + bias[u])
            alpha = jnp.exp2(m - m_new)
            pb = jnp.concatenate(
                [jnp.exp2(s[u] + (bias[u] - m_new)).astype(BF16) for u in range(grp)], axis=0)
            vt = values_t([vt_ref[h, j0 + u] for u in range(grp)])
            acc_ref[h] = alpha * acc_ref[h] + jnp.dot(vt, pb, preferred_element_type=F32)
            stats[h] = m_new
        return tuple(stats)

    lax.fori_loop(0, (qi + span - 1) // span, past_blocks, tuple(stats))
    for h in range(heads):
        acc = acc_ref[h]
        o_ref[:, h * hd:(h + 1) * hd] = (acc[:hd] / acc[hd:hd + 1]).T.astype(o_ref.dtype)


def _moba(qt, kb, vt, kmean):
    heads, nb, hd, blk = qt.shape
    s = nb * blk
    assert nb % (MOBA_BLOCKS_PER_STEP * MOBA_STEPS_PER_ITER) == 0
    const = lambda shape: pl.BlockSpec(shape, lambda i: (0,) * len(shape),
                                       pipeline_mode=pl.Buffered(1))
    return pl.pallas_call(
        _moba_kernel,
        grid=(nb,),
        in_specs=[
            pl.BlockSpec((heads, None, hd, blk), lambda i: (0, i, 0, 0)),
            const((heads, nb, blk, hd)),
            const((heads, nb, hd, blk)),
            const((heads, nb, hd)),
        ],
        out_specs=pl.BlockSpec((blk, heads * hd), lambda i: (i, 0)),
        out_shape=jax.ShapeDtypeStruct((s, heads * hd), BF16),
        scratch_shapes=[
            pltpu.VMEM((heads, nb, blk), F32),
            pltpu.VMEM((heads, hd + BF16_SUBLANES, blk), F32),
            pltpu.VMEM((2 * MOBA_MATMUL_LAG, MOBA_BLOCKS_PER_STEP * blk, blk), F32),
        ],
        compiler_params=pltpu.CompilerParams(
            dimension_semantics=("arbitrary",), vmem_limit_bytes=VMEM_LIMIT_BYTES),
        name="moba",
    )(qt, kb, vt, kmean)


def _out_mlp_kernel(x_ref, og_ref, om_ref, p_ref, wo_ref, gffn_ref, wup_ref, wdn_ref,
                    gpg_ref, wpg_ref, wpp_ref, gpe_ref, o_ref):
    h = x_ref[...]
    h = h + jnp.dot(og_ref[...], wo_ref[:GLA_V_W, :], preferred_element_type=F32)
    h = h + jnp.dot(om_ref[...], wo_ref[GLA_V_W:, :], preferred_element_type=F32)

    u = (_rms(h, -1) * gffn_ref[...]).astype(BF16)
    mlp = jnp.zeros_like(h)
    for c0 in range(0, D_FF, FF_CHUNK):
        f = jnp.dot(u, wup_ref[:, c0:c0 + FF_CHUNK], preferred_element_type=F32)
        f = jnp.square(jnp.maximum(f, 0.0)).astype(BF16)
        mlp = mlp + jnp.dot(f, wdn_ref[c0:c0 + FF_CHUNK, :], preferred_element_type=F32)
    h = h + mlp

    u = (_rms(h, -1) * gpg_ref[...]).astype(BF16)
    gate = jax.nn.sigmoid(jnp.dot(u, wpg_ref[...], preferred_element_type=F32))
    e = jnp.dot(p_ref[...].astype(BF16), wpp_ref[...], preferred_element_type=F32)
    e = _rms(e, -1) * gpe_ref[...]
    o_ref[...] = h + gate * e


def _out_mlp(x2, og, om, p2, wo, g_ffn, wup, wdn, g_pg, wpg, wpp, g_pe):
    s = x2.shape[0]
    tm = min(OUT_TM, s)
    const = lambda shape: pl.BlockSpec(shape, lambda i: (0,) * len(shape),
                                       pipeline_mode=pl.Buffered(1))
    row = lambda w: pl.BlockSpec((tm, w), lambda i: (i, 0))
    return pl.pallas_call(
        _out_mlp_kernel,
        grid=(s // tm,),
        in_specs=[
            row(D_MODEL), row(GLA_V_W), row(MOBA_W), row(PLE_DIM),
            const((GLA_V_W + MOBA_W, D_MODEL)),
            const((1, D_MODEL)),
            const((D_MODEL, D_FF)),
            const((D_FF, D_MODEL)),
            const((1, D_MODEL)),
            const((D_MODEL, D_MODEL)),
            const((PLE_DIM, D_MODEL)),
            const((1, D_MODEL)),
        ],
        out_specs=row(D_MODEL),
        out_shape=jax.ShapeDtypeStruct((s, D_MODEL), F32),
        compiler_params=pltpu.CompilerParams(
            dimension_semantics=("arbitrary",), vmem_limit_bytes=VMEM_LIMIT_BYTES),
        name="out_mlp",
    )(x2, og, om, p2, wo, g_ffn, wup, wdn, g_pg, wpg, wpp, g_pe)


def _layer(h2, p2, pos, g_mix, w_in, w_gla_a2, b_gla_a, g_gla_out, g_moba_q, g_moba_k,
           w_out, g_ffn, w_up, w_down, g_ple_gate, w_ple_gate, w_ple_proj, g_ple_emb):
    s = h2.shape[0]
    assert s % MOBA_BLOCK == 0 and s % GLA_CHUNK == 0
    assert s % min(IN_TM, s) == 0 and s % min(GLA_TM, s) == 0 and s % min(OUT_TM, s) == 0
    assert min(GLA_TM, s) % (GLA_CHUNKS_PER_STEP * GLA_CHUNK) == 0

    wa = jnp.concatenate(
        [w_in[:, :GLA_IN_W], jnp.zeros((D_MODEL, GA_PAD - GLA_LOWRANK), w_in.dtype)],
        axis=1).astype(BF16)
    wbt = w_in[:, GLA_IN_W:].T.astype(BF16)
    wa2 = jnp.concatenate(
        [w_gla_a2, jnp.zeros((GA_PAD - GLA_LOWRANK, GLA_QK_W), w_gla_a2.dtype)], axis=0)
    half = MOBA_HD // 2
    inv_col = (1.0 / (ROPE_THETA ** (jnp.arange(half, dtype=F32) / half))).reshape(half, 1)

    gq, gk, gv, gg, la, qt, kb, vt, kmean = _in_proj(
        h2, pos, g_mix.reshape(1, -1), wa, wbt, wa2, b_gla_a.reshape(1, -1),
        g_moba_q.reshape(-1, 1), g_moba_k.reshape(-1, 1), inv_col)
    o_gla = _gla(gq, gk, gv, gg, la, g_gla_out.reshape(1, -1))
    o_moba = _moba(qt, kb, vt, kmean.reshape(MOBA_HEADS, s // MOBA_BLOCK, MOBA_HD))
    return _out_mlp(
        h2, o_gla, o_moba, p2, w_out.astype(BF16), g_ffn.reshape(1, -1),
        w_up.astype(BF16), w_down.astype(BF16), g_ple_gate.reshape(1, -1),
        w_ple_gate.astype(BF16), w_ple_proj.astype(BF16), g_ple_emb.reshape(1, -1))


def kernel(x, p, positions, g_mix, w_in, w_gla_a2, b_gla_a, g_gla_out, g_moba_q, g_moba_k,
           w_out, g_ffn, w_up, w_down, g_ple_gate, w_ple_gate, w_ple_proj, g_ple_emb):
    depth = p.shape[0]
    batch = x.shape[0]
    outs = []
    for b in range(batch):
        h = x[b]
        pos = positions[b:b + 1]
        for i in range(depth):
            h = _layer(h, p[i, b], pos, g_mix[i], w_in[i], w_gla_a2[i], b_gla_a[i],
                       g_gla_out[i], g_moba_q[i], g_moba_k[i], w_out[i], g_ffn[i], w_up[i],
                       w_down[i], g_ple_gate[i], w_ple_gate[i], w_ple_proj[i], g_ple_emb[i])
        outs.append(h)
    return jnp.stack(outs, axis=0)
```

```python
import functools
import math

import numpy as np
import jax
import jax.numpy as jnp
from jax import lax
from jax.experimental import pallas as pl
from jax.experimental.pallas import tpu as pltpu

F32 = jnp.float32
BF16 = jnp.bfloat16
HIGHEST = lax.Precision.HIGHEST

D_MODEL = 1024
PLE_DIM = 256
GLA_HEADS = 4
GLA_DK = 64
GLA_DV = 128
GLA_LOWRANK = 16
GLA_TAU = 16.0
GLA_CHUNK = 64
MOBA_HEADS = 4
MOBA_HD = 128
MOBA_BLOCK = 256
MOBA_TOPK = 3
ROPE_THETA = 10000.0
D_FF = 4 * D_MODEL
EPS = 1e-6

GLA_QK_W = GLA_HEADS * GLA_DK
GLA_V_W = GLA_HEADS * GLA_DV
MOBA_W = MOBA_HEADS * MOBA_HD
GLA_IN_W = 2 * GLA_QK_W + 2 * GLA_V_W + GLA_LOWRANK
LANES = 128
BF16_SUBLANES = 16
GA_PAD = LANES
WA_W = 2 * GLA_QK_W + 2 * GLA_V_W + GA_PAD

VMEM_LIMIT_BYTES = 56 * 1024 * 1024

IN_TM = 512
GLA_CHUNKS_PER_STEP = 4
GLA_TM = 1024
OUT_TM = 512
FF_CHUNK = 1024
MOBA_BLOCKS_PER_STEP = 2
MOBA_STEPS_PER_ITER = 2
MOBA_STEPS_PER_LONG_ITER = 4
MOBA_MATMUL_LAG = 3
MOBA_SCORE_SLOTS = 8

MOBA_QSCALE = (MOBA_HD ** -0.5) * math.log2(math.e)
NEG_INF = float("-inf")


def _rms(x, axis):
    return x * lax.rsqrt(jnp.mean(x * x, axis=axis, keepdims=True) + EPS)


def _log_sigmoid(a):
    return jnp.minimum(a, 0.0) - jnp.log1p(jnp.exp(-jnp.abs(a)))


def _dot_nt(a, b):
    return lax.dot_general(a, b, (((1,), (1,)), ((), ())), preferred_element_type=F32)


def _dot_tn(a, b):
    return lax.dot_general(a, b, (((0,), (0,)), ((), ())), preferred_element_type=F32)


def _in_proj_kernel(x_ref, pos_ref, gmix_ref, wa_ref, wbt_ref, wa2_ref, ba_ref,
                    gq_ref, gk_ref, inv_ref,
                    q_out, k_out, v_out, gg_out, la_out,
                    qt_out, kb_out, vt_out, kmean_out):
    tm = x_ref.shape[0]
    nblk = tm // MOBA_BLOCK
    u = (_rms(x_ref[...], -1) * gmix_ref[...]).astype(BF16)

    za = jnp.dot(u, wa_ref[...], preferred_element_type=F32)
    q_out[...] = za[:, 0:GLA_QK_W]
    k_out[...] = za[:, GLA_QK_W:2 * GLA_QK_W]
    v_out[...] = za[:, 2 * GLA_QK_W:2 * GLA_QK_W + GLA_V_W].astype(BF16)
    gg_out[...] = za[:, 2 * GLA_QK_W + GLA_V_W:2 * GLA_QK_W + 2 * GLA_V_W].astype(BF16)
    ga = za[:, 2 * GLA_QK_W + 2 * GLA_V_W:]
    wa2 = wa2_ref[...]
    ga_hi, wa2_hi = ga.astype(BF16), wa2.astype(BF16)
    ga_lo = (ga - ga_hi.astype(F32)).astype(BF16)
    wa2_lo = (wa2 - wa2_hi.astype(F32)).astype(BF16)
    a_pre = (jnp.dot(ga_hi, wa2_hi, preferred_element_type=F32)
             + jnp.dot(ga_hi, wa2_lo, preferred_element_type=F32)
             + jnp.dot(ga_lo, wa2_hi, preferred_element_type=F32)) + ba_ref[...]
    la_out[...] = _log_sigmoid(a_pre) * (1.0 / GLA_TAU)

    zt = _dot_nt(wbt_ref[...], u)
    ang = inv_ref[...] * pos_ref[...].astype(F32)
    cos = jnp.cos(ang)
    sin = jnp.sin(ang)
    half = MOBA_HD // 2

    def norm_rope(t, g_col):
        t = _rms(t, 0) * g_col
        t1, t2 = t[:half], t[half:]
        return jnp.concatenate([t1 * cos - t2 * sin, t2 * cos + t1 * sin], axis=0)

    for h in range(MOBA_HEADS):
        r0 = h * MOBA_HD
        qr = norm_rope(zt[r0:r0 + MOBA_HD], gq_ref[...]) * MOBA_QSCALE
        kr = norm_rope(zt[MOBA_W + r0:MOBA_W + r0 + MOBA_HD], gk_ref[...])
        krow = kr.T
        vt = zt[2 * MOBA_W + r0:2 * MOBA_W + r0 + MOBA_HD]
        for t in range(nblk):
            c0 = t * MOBA_BLOCK
            qt_out[h, t] = qr[:, c0:c0 + MOBA_BLOCK].astype(BF16)
            vt_out[h, t] = vt[:, c0:c0 + MOBA_BLOCK].astype(BF16)
            kblk = krow[c0:c0 + MOBA_BLOCK]
            kb_out[h, t] = kblk.astype(BF16)
            kmean_out[h, t] = jnp.mean(kblk, axis=0, keepdims=True)


def _in_proj(x2, pos, g_mix, wa, wbt, wa2, ba, gq_col, gk_col, inv_col):
    s = x2.shape[0]
    tm = min(IN_TM, s)
    nb = s // MOBA_BLOCK
    nblk = tm // MOBA_BLOCK
    const = lambda shape: pl.BlockSpec(shape, lambda i: (0,) * len(shape),
                                       pipeline_mode=pl.Buffered(1))
    row = lambda w: pl.BlockSpec((tm, w), lambda i: (i, 0))
    hblk = lambda a, b: pl.BlockSpec((MOBA_HEADS, nblk, a, b), lambda i: (0, i, 0, 0))
    return pl.pallas_call(
        _in_proj_kernel,
        grid=(s // tm,),
        in_specs=[
            row(D_MODEL),
            pl.BlockSpec((1, tm), lambda i: (0, i)),
            const((1, D_MODEL)),
            const((D_MODEL, WA_W)),
            const((3 * MOBA_W, D_MODEL)),
            const((GA_PAD, GLA_QK_W)),
            const((1, GLA_QK_W)),
            const((MOBA_HD, 1)),
            const((MOBA_HD, 1)),
            const((MOBA_HD // 2, 1)),
        ],
        out_specs=[
            row(GLA_QK_W), row(GLA_QK_W), row(GLA_V_W), row(GLA_V_W), row(GLA_QK_W),
            hblk(MOBA_HD, MOBA_BLOCK), hblk(MOBA_BLOCK, MOBA_HD), hblk(MOBA_HD, MOBA_BLOCK),
            hblk(1, MOBA_HD),
        ],
        out_shape=[
            jax.ShapeDtypeStruct((s, GLA_QK_W), F32),
            jax.ShapeDtypeStruct((s, GLA_QK_W), F32),
            jax.ShapeDtypeStruct((s, GLA_V_W), BF16),
            jax.ShapeDtypeStruct((s, GLA_V_W), BF16),
            jax.ShapeDtypeStruct((s, GLA_QK_W), F32),
            jax.ShapeDtypeStruct((MOBA_HEADS, nb, MOBA_HD, MOBA_BLOCK), BF16),
            jax.ShapeDtypeStruct((MOBA_HEADS, nb, MOBA_BLOCK, MOBA_HD), BF16),
            jax.ShapeDtypeStruct((MOBA_HEADS, nb, MOBA_HD, MOBA_BLOCK), BF16),
            jax.ShapeDtypeStruct((MOBA_HEADS, nb, 1, MOBA_HD), F32),
        ],
        compiler_params=pltpu.CompilerParams(
            dimension_semantics=("arbitrary",), vmem_limit_bytes=VMEM_LIMIT_BYTES),
        name="in_proj",
    )(x2, pos, g_mix, wa, wbt, wa2, ba, gq_col, gk_col, inv_col)


def _gla_kernel(q_ref, k_ref, v_ref, gg_ref, la_ref, gout_ref, o_ref, st_ref):
    c = GLA_CHUNK
    hc = GLA_HEADS * c

    @pl.when(pl.program_id(0) == 0)
    def _():
        st_ref[...] = jnp.zeros_like(st_ref)

    r = lax.broadcasted_iota(jnp.int32, (hc, GLA_QK_W), 0)
    l = lax.broadcasted_iota(jnp.int32, (hc, GLA_QK_W), 1)
    same_head = (r // c) == (l // GLA_DK)
    causal = same_head & ((l % c) <= (r % c))
    ng = GLA_CHUNKS_PER_STEP
    gr = ng * c
    tr = lax.broadcasted_iota(jnp.int32, (gr, gr), 0)
    tc = lax.broadcasted_iota(jnp.int32, (gr, gr), 1)
    tri = ((tr // c == tc // c) & (tc <= tr)).astype(BF16)
    scale = GLA_DK ** -0.5

    def stack_heads(a, width):
        return jnp.concatenate([a[:, h * width:(h + 1) * width] for h in range(GLA_HEADS)], axis=0)

    def block_diag(a):
        return jnp.where(same_head, jnp.concatenate([a] * GLA_HEADS, axis=0), 0.0).astype(BF16)

    chunks = [slice(i * c, (i + 1) * c) for i in range(ng)]

    def cumsum_stage(g):
        la = la_ref[g * gr:(g + 1) * gr, :]
        hi = la.astype(BF16)
        r1 = la - hi.astype(F32)
        mid = r1.astype(BF16)
        lo = (r1 - mid.astype(F32)).astype(BF16)
        return (jnp.dot(tri, hi, preferred_element_type=F32)
                + jnp.dot(tri, mid, preferred_element_type=F32)
                + jnp.dot(tri, lo, preferred_element_type=F32))

    def score_stage(g, b):
        q = q_ref[g * gr:(g + 1) * gr, :]
        k = k_ref[g * gr:(g + 1) * gr, :]
        v = v_ref[g * gr:(g + 1) * gr, :]
        b_last = [b[(i + 1) * c - 1:(i + 1) * c, :] for i in range(ng)]
        qb = [block_diag(q[ch] * jnp.exp(b[ch]) * scale) for ch in chunks]
        kb = [block_diag(k[ch] * jnp.exp(-b[ch])) for ch in chunks]
        keb = [block_diag(k[ch] * jnp.exp(b_last[i] - b[ch])) for i, ch in enumerate(chunks)]
        vst = [stack_heads(v[ch], GLA_DV) for ch in chunks]
        att = [_dot_nt(qb[i], kb[i]) for i in range(ng)]
        dst = [_dot_tn(vst[i], keb[i]) for i in range(ng)]
        decay = [jnp.exp(bl) for bl in b_last]
        return qb, vst, att, dst, decay

    def output_stage(qb, vst, att, dst, decay):
        att = [jnp.where(causal, a, 0.0).astype(BF16) for a in att]
        intra = [jnp.dot(att[i], vst[i], preferred_element_type=F32) for i in range(ng)]
        st = st_ref[...]
        inter = []
        for i in range(ng):
            inter.append(_dot_nt(qb[i], st.astype(BF16)))
            st = st * decay[i] + dst[i]
        st_ref[...] = st
        return [intra[i] + inter[i] for i in range(ng)]

    def store_stage(g, outs):
        gg = gg_ref[g * gr:(g + 1) * gr, :]
        for i, ch in enumerate(chunks):
            o = _rms(outs[i], -1) * gout_ref[...]
            gate = stack_heads(gg[ch], GLA_DV).astype(F32)
            o = o * (gate * jax.nn.sigmoid(gate))
            for h in range(GLA_HEADS):
                o_ref[g * gr + i * c:g * gr + (i + 1) * c, h * GLA_DV:(h + 1) * GLA_DV] = (
                    o[h * c:(h + 1) * c].astype(o_ref.dtype))

    ngroups = q_ref.shape[0] // gr
    cums, scored, outs = {}, {}, {}
    for t in range(ngroups + 3):
        if t < ngroups:
            cums[t] = cumsum_stage(t)
        if 0 <= t - 1 < ngroups:
            scored[t - 1] = score_stage(t - 1, cums.pop(t - 1))
        if 0 <= t - 2 < ngroups:
            outs[t - 2] = output_stage(*scored.pop(t - 2))
        if 0 <= t - 3 < ngroups:
            store_stage(t - 3, outs.pop(t - 3))


def _gla(q, k, v, gg, la, g_out):
    s = q.shape[0]
    tm = min(GLA_TM, s)
    row = lambda w: pl.BlockSpec((tm, w), lambda i: (i, 0))
    return pl.pallas_call(
        _gla_kernel,
        grid=(s // tm,),
        in_specs=[row(GLA_QK_W), row(GLA_QK_W), row(GLA_V_W), row(GLA_V_W), row(GLA_QK_W),
                  pl.BlockSpec((1, GLA_DV), lambda i: (0, 0))],
        out_specs=row(GLA_V_W),
        out_shape=jax.ShapeDtypeStruct((s, GLA_V_W), BF16),
        scratch_shapes=[pltpu.VMEM((GLA_DV, GLA_QK_W), F32)],
        compiler_params=pltpu.CompilerParams(dimension_semantics=("arbitrary",)),
        name="gla",
    )(q, k, v, gg, la, g_out)


def _moba_kernel(qt_ref, k_ref, vt_ref, kmean_ref, o_ref, bias_ref, acc_ref, s_ref):
    qi = pl.program_id(0)
    heads, nb, hd = kmean_ref.shape
    tq = qt_ref.shape[2]
    kblk = k_ref.shape[2]
    grp = MOBA_BLOCKS_PER_STEP
    lag = MOBA_MATMUL_LAG
    nslots = s_ref.shape[0]

    def step_list(ngrp):
        steps = [(h, g) for g in range(ngrp) for h in range(heads)]
        assert len(steps) % nslots == 0 and nslots >= 2 * lag
        return steps

    def score_matmul(ngrp, base, n):
        steps = step_list(ngrp)
        span = grp * ngrp
        h, g = steps[n % len(steps)]
        blk0 = jnp.minimum(base + (n // len(steps)) * span, nb - span) + g * grp
        kg = k_ref[h, pl.ds(blk0, grp)].reshape(grp * kblk, -1)
        s_ref[n % nslots] = jnp.dot(kg, qt_ref[h], preferred_element_type=F32)

    pad = acc_ref.shape[1] - hd

    def values_t(blocks):
        keys = len(blocks) * kblk
        first_row = lax.broadcasted_iota(jnp.int32, (pad, keys), 0) == 0
        ones_rows = jnp.where(first_row, 1.0, 0.0).astype(BF16)
        return jnp.concatenate([jnp.concatenate(blocks, axis=1), ones_rows], axis=0)

    gates = []
    for h in range(heads):
        km = kmean_ref[h]
        hi = km.astype(BF16)
        r1 = km - hi.astype(F32)
        mid = r1.astype(BF16)
        lo = (r1 - mid.astype(F32)).astype(BF16)
        qt = qt_ref[h]
        gates.append(jnp.dot(hi, qt, preferred_element_type=F32)
                     + jnp.dot(mid, qt, preferred_element_type=F32)
                     + jnp.dot(lo, qt, preferred_element_type=F32))
    own = [jnp.dot(k_ref[h, qi], qt_ref[h], preferred_element_type=F32) for h in range(heads)]
    for n in range(lag):
        score_matmul(MOBA_STEPS_PER_ITER, 0, n)

    blk = lax.broadcasted_iota(jnp.int32, (nb, tq), 0)
    for h in range(heads):
        gate = jnp.where(blk < qi, gates[h], NEG_INF)
        sel = jnp.zeros((nb, tq), jnp.bool_)
        for _ in range(min(MOBA_TOPK, nb)):
            m = jnp.max(gate, axis=0, keepdims=True)
            first = jnp.min(jnp.where(gate == m, blk, nb), axis=0, keepdims=True)
            pick = (blk == first) & (m > NEG_INF)
            sel = sel | pick
            gate = jnp.where(pick, NEG_INF, gate)
        bias_ref[h] = jnp.where(sel, 0.0, NEG_INF)

    kk = lax.broadcasted_iota(jnp.int32, (kblk, tq), 0)
    qq = lax.broadcasted_iota(jnp.int32, (kblk, tq), 1)
    causal = kk <= qq
    stats, probs = [], []
    for h in range(heads):
        s = jnp.where(causal, own[h], NEG_INF)
        m = jnp.max(s, axis=0, keepdims=True)
        stats.append(m)
        probs.append(jnp.exp2(s - m).astype(BF16))
    for h in range(heads):
        acc_ref[h] = jnp.dot(values_t([vt_ref[h, qi]]), probs[h], preferred_element_type=F32)

    def past_blocks(ngrp, first_block):
        span = grp * ngrp

        def body(it, carry):
            base = pl.multiple_of(first_block + it * span, grp * MOBA_STEPS_PER_ITER)
            stats = list(carry)
            for n, (h, g) in enumerate(step_list(ngrp)):
                score_matmul(ngrp, base, n + lag)
                j0 = base + g * grp
                m = stats[h]
                s = [s_ref[n % nslots, u * kblk:(u + 1) * kblk] for u in range(grp)]
                bias = [bias_ref[h, pl.ds(j0 + u, 1), :] for u in range(grp)]
                m_new = m
                for u in range(grp):
                    m_new = jnp.maximum(m_new, jnp.max(s[u], axis=0, keepdims=True) + bias[u])
                alpha = jnp.exp2(m - m_new)
                pb = jnp.concatenate(
                    [jnp.exp2(s[u] + (bias[u] - m_new)).astype(BF16) for u in range(grp)], axis=0)
                vt = values_t([vt_ref[h, j0 + u] for u in range(grp)])
                acc_ref[h] = alpha * acc_ref[h] + jnp.dot(vt, pb, preferred_element_type=F32)
                stats[h] = m_new
            return tuple(stats)

        return body

    long_ngrp = MOBA_STEPS_PER_LONG_ITER
    short_span, long_span = grp * MOBA_STEPS_PER_ITER, grp * long_ngrp
    assert step_list(long_ngrp)[:lag] == step_list(MOBA_STEPS_PER_ITER)[:lag]
    n_long = qi // long_span
    n_short = (qi - n_long * long_span + short_span - 1) // short_span
    stats = lax.fori_loop(0, n_long, past_blocks(long_ngrp, 0), tuple(stats))
    lax.fori_loop(0, n_short, past_blocks(MOBA_STEPS_PER_ITER, n_long * long_span), stats)
    for h in range(heads):
        acc = acc_ref[h]
        o_ref[:, h * hd:(h + 1) * hd] = (acc[:hd] / acc[hd:hd + 1]).T.astype(o_ref.dtype)


def _moba(qt, kb, vt, kmean):
    heads, nb, hd, blk = qt.shape
    s = nb * blk
    assert nb % (MOBA_BLOCKS_PER_STEP * MOBA_STEPS_PER_ITER) == 0
    assert nb % (MOBA_BLOCKS_PER_STEP * MOBA_STEPS_PER_LONG_ITER) == 0
    const = lambda shape: pl.BlockSpec(shape, lambda i: (0,) * len(shape),
                                       pipeline_mode=pl.Buffered(1))
    return pl.pallas_call(
        _moba_kernel,
        grid=(nb,),
        in_specs=[
            pl.BlockSpec((heads, None, hd, blk), lambda i: (0, i, 0, 0)),
            const((heads, nb, blk, hd)),
            const((heads, nb, hd, blk)),
            const((heads, nb, hd)),
        ],
        out_specs=pl.BlockSpec((blk, heads * hd), lambda i: (i, 0)),
        out_shape=jax.ShapeDtypeStruct((s, heads * hd), BF16),
        scratch_shapes=[
            pltpu.VMEM((heads, nb, blk), F32),
            pltpu.VMEM((heads, hd + BF16_SUBLANES, blk), F32),
            pltpu.VMEM((MOBA_SCORE_SLOTS, MOBA_BLOCKS_PER_STEP * blk, blk), F32),
        ],
        compiler_params=pltpu.CompilerParams(
            dimension_semantics=("arbitrary",), vmem_limit_bytes=VMEM_LIMIT_BYTES),
        name="moba",
    )(qt, kb, vt, kmean)


def _out_mlp_kernel(x_ref, og_ref, om_ref, p_ref, wo_ref, gffn_ref, wup_ref, wdn_ref,
                    gpg_ref, wpg_ref, wpp_ref, gpe_ref, o_ref):
    h = x_ref[...]
    h = h + jnp.dot(og_ref[...], wo_ref[:GLA_V_W, :], preferred_element_type=F32)
    h = h + jnp.dot(om_ref[...], wo_ref[GLA_V_W:, :], preferred_element_type=F32)

    u = (_rms(h, -1) * gffn_ref[...]).astype(BF16)
    mlp = jnp.zeros_like(h)
    for c0 in range(0, D_FF, FF_CHUNK):
        f = jnp.dot(u, wup_ref[:, c0:c0 + FF_CHUNK], preferred_element_type=F32)
        f = jnp.square(jnp.maximum(f, 0.0)).astype(BF16)
        mlp = mlp + jnp.dot(f, wdn_ref[c0:c0 + FF_CHUNK, :], preferred_element_type=F32)
    h = h + mlp

    u = (_rms(h, -1) * gpg_ref[...]).astype(BF16)
    gate = jax.nn.sigmoid(jnp.dot(u, wpg_ref[...], preferred_element_type=F32))
    e = jnp.dot(p_ref[...].astype(BF16), wpp_ref[...], preferred_element_type=F32)
    e = _rms(e, -1) * gpe_ref[...]
    o_ref[...] = h + gate * e


def _out_mlp(x2, og, om, p2, wo, g_ffn, wup, wdn, g_pg, wpg, wpp, g_pe):
    s = x2.shape[0]
    tm = min(OUT_TM, s)
    const = lambda shape: pl.BlockSpec(shape, lambda i: (0,) * len(shape),
                                       pipeline_mode=pl.Buffered(1))
    row = lambda w: pl.BlockSpec((tm, w), lambda i: (i, 0))
    return pl.pallas_call(
        _out_mlp_kernel,
        grid=(s // tm,),
        in_specs=[
            row(D_MODEL), row(GLA_V_W), row(MOBA_W), row(PLE_DIM),
            const((GLA_V_W + MOBA_W, D_MODEL)),
            const((1, D_MODEL)),
            const((D_MODEL, D_FF)),
            const((D_FF, D_MODEL)),
            const((1, D_MODEL)),
            const((D_MODEL, D_MODEL)),
            const((PLE_DIM, D_MODEL)),
            const((1, D_MODEL)),
        ],
        out_specs=row(D_MODEL),
        out_shape=jax.ShapeDtypeStruct((s, D_MODEL), F32),
        compiler_params=pltpu.CompilerParams(
            dimension_semantics=("arbitrary",), vmem_limit_bytes=VMEM_LIMIT_BYTES),
        name="out_mlp",
    )(x2, og, om, p2, wo, g_ffn, wup, wdn, g_pg, wpg, wpp, g_pe)


def _layer(h2, p2, pos, g_mix, w_in, w_gla_a2, b_gla_a, g_gla_out, g_moba_q, g_moba_k,
           w_out, g_ffn, w_up, w_down, g_ple_gate, w_ple_gate, w_ple_proj, g_ple_emb):
    s = h2.shape[0]
    assert s % MOBA_BLOCK == 0 and s % GLA_CHUNK == 0
    assert s % min(IN_TM, s) == 0 and s % min(GLA_TM, s) == 0 and s % min(OUT_TM, s) == 0
    assert min(GLA_TM, s) % (GLA_CHUNKS_PER_STEP * GLA_CHUNK) == 0

    wa = jnp.concatenate(
        [w_in[:, :GLA_IN_W], jnp.zeros((D_MODEL, GA_PAD - GLA_LOWRANK), w_in.dtype)],
        axis=1).astype(BF16)
    wbt = w_in[:, GLA_IN_W:].T.astype(BF16)
    wa2 = jnp.concatenate(
        [w_gla_a2, jnp.zeros((GA_PAD - GLA_LOWRANK, GLA_QK_W), w_gla_a2.dtype)], axis=0)
    half = MOBA_HD // 2
    inv_col = (1.0 / (ROPE_THETA ** (jnp.arange(half, dtype=F32) / half))).reshape(half, 1)

    gq, gk, gv, gg, la, qt, kb, vt, kmean = _in_proj(
        h2, pos, g_mix.reshape(1, -1), wa, wbt, wa2, b_gla_a.reshape(1, -1),
        g_moba_q.reshape(-1, 1), g_moba_k.reshape(-1, 1), inv_col)
    o_gla = _gla(gq, gk, gv, gg, la, g_gla_out.reshape(1, -1))
    o_moba = _moba(qt, kb, vt, kmean.reshape(MOBA_HEADS, s // MOBA_BLOCK, MOBA_HD))
    return _out_mlp(
        h2, o_gla, o_moba, p2, w_out.astype(BF16), g_ffn.reshape(1, -1),
        w_up.astype(BF16), w_down.astype(BF16), g_ple_gate.reshape(1, -1),
        w_ple_gate.astype(BF16), w_ple_proj.astype(BF16), g_ple_emb.reshape(1, -1))


def kernel(x, p, positions, g_mix, w_in, w_gla_a2, b_gla_a, g_gla_out, g_moba_q, g_moba_k,
           w_out, g_ffn, w_up, w_down, g_ple_gate, w_ple_gate, w_ple_proj, g_ple_emb):
    depth = p.shape[0]
    batch = x.shape[0]
    outs = []
    for b in range(batch):
        h = x[b]
        pos = positions[b:b + 1]
        for i in range(depth):
            h = _layer(h, p[i, b], pos, g_mix[i], w_in[i], w_gla_a2[i], b_gla_a[i],
                       g_gla_out[i], g_moba_q[i], g_moba_k[i], w_out[i], g_ffn[i], w_up[i],
                       w_down[i], g_ple_gate[i], w_ple_gate[i], w_ple_proj[i], g_ple_emb[i])
        outs.append(h)
    return jnp.stack(outs, axis=0)
```

```python
import functools
import math

import numpy as np
import jax
import jax.numpy as jnp
from jax import lax
from jax.experimental import pallas as pl
from jax.experimental.pallas import tpu as pltpu

F32 = jnp.float32
BF16 = jnp.bfloat16
HIGHEST = lax.Precision.HIGHEST

D_MODEL = 1024
PLE_DIM = 256
GLA_HEADS = 4
GLA_DK = 64
GLA_DV = 128
GLA_LOWRANK = 16
GLA_TAU = 16.0
GLA_CHUNK = 64
MOBA_HEADS = 4
MOBA_HD = 128
MOBA_BLOCK = 256
MOBA_TOPK = 3
ROPE_THETA = 10000.0
D_FF = 4 * D_MODEL
EPS = 1e-6

GLA_QK_W = GLA_HEADS * GLA_DK
GLA_V_W = GLA_HEADS * GLA_DV
MOBA_W = MOBA_HEADS * MOBA_HD
GLA_IN_W = 2 * GLA_QK_W + 2 * GLA_V_W + GLA_LOWRANK
LANES = 128
BF16_SUBLANES = 16
GA_PAD = LANES
WA_W = 2 * GLA_QK_W + 2 * GLA_V_W + GA_PAD

VMEM_LIMIT_BYTES = 56 * 1024 * 1024

IN_TM = 1024
GLA_CHUNKS_PER_STEP = 4
GLA_TM = 1024
OUT_TM = 512
OUT_SUBTILES = 2
FF_CHUNK = 1024
MOBA_BLOCKS_PER_STEP = 2
MOBA_STEPS_PER_ITER = 2
MOBA_STEPS_PER_LONG_ITER = 4
MOBA_MATMUL_LAG = 3
MOBA_SCORE_SLOTS = 8

MOBA_QSCALE = (MOBA_HD ** -0.5) * math.log2(math.e)
NEG_INF = float("-inf")


def _rms(x, axis):
    return x * lax.rsqrt(jnp.mean(x * x, axis=axis, keepdims=True) + EPS)


def _log_sigmoid(a):
    return jnp.minimum(a, 0.0) - jnp.log1p(jnp.exp(-jnp.abs(a)))


def _dot_nt(a, b):
    return lax.dot_general(a, b, (((1,), (1,)), ((), ())), preferred_element_type=F32)


def _dot_tn(a, b):
    return lax.dot_general(a, b, (((0,), (0,)), ((), ())), preferred_element_type=F32)


def _in_proj_kernel(x_ref, pos_ref, gmix_ref, wa_ref, wbt_ref, wa2_ref, ba_ref,
                    gq_ref, gk_ref, inv_ref,
                    q_out, k_out, v_out, gg_out, la_out,
                    qt_out, kb_out, vt_out, kmean_out):
    tm = x_ref.shape[0]
    sub = MOBA_BLOCK
    nsub = tm // sub
    half = MOBA_HD // 2
    wa2 = wa2_ref[...]
    wa2_hi = wa2.astype(BF16)
    wa2_lo = (wa2 - wa2_hi.astype(F32)).astype(BF16)

    us = [(_rms(x_ref[t * sub:(t + 1) * sub, :], -1) * gmix_ref[...]).astype(BF16)
          for t in range(nsub)]
    zts, zas = [], []
    for t in range(nsub):
        zts.append(_dot_nt(wbt_ref[...], us[t]))
        zas.append(jnp.dot(us[t], wa_ref[...], preferred_element_type=F32))

    for t in range(nsub):
        rows = slice(t * sub, (t + 1) * sub)

        zt = zts[t]
        ang = inv_ref[...] * pos_ref[:, rows].astype(F32)
        cos = jnp.cos(ang)
        sin = jnp.sin(ang)

        def norm_rope(a, g_col):
            a = _rms(a, 0) * g_col
            a1, a2 = a[:half], a[half:]
            return jnp.concatenate([a1 * cos - a2 * sin, a2 * cos + a1 * sin], axis=0)

        for h in range(MOBA_HEADS):
            r0 = h * MOBA_HD
            qr = norm_rope(zt[r0:r0 + MOBA_HD], gq_ref[...]) * MOBA_QSCALE
            kr = norm_rope(zt[MOBA_W + r0:MOBA_W + r0 + MOBA_HD], gk_ref[...])
            kblk = kr.T
            qt_out[h, t] = qr.astype(BF16)
            vt_out[h, t] = zt[2 * MOBA_W + r0:2 * MOBA_W + r0 + MOBA_HD].astype(BF16)
            kb_out[h, t] = kblk.astype(BF16)
            kmean_out[h, t] = jnp.mean(kblk, axis=0, keepdims=True)

        za = zas[t]
        q_out[rows, :] = za[:, 0:GLA_QK_W]
        k_out[rows, :] = za[:, GLA_QK_W:2 * GLA_QK_W]
        v_out[rows, :] = za[:, 2 * GLA_QK_W:2 * GLA_QK_W + GLA_V_W].astype(BF16)
        gg_out[rows, :] = za[:, 2 * GLA_QK_W + GLA_V_W:2 * GLA_QK_W + 2 * GLA_V_W].astype(BF16)

    for t in range(nsub):
        ga = zas[t][:, 2 * GLA_QK_W + 2 * GLA_V_W:]
        ga_hi = ga.astype(BF16)
        ga_lo = (ga - ga_hi.astype(F32)).astype(BF16)
        a_pre = (jnp.dot(ga_hi, wa2_hi, preferred_element_type=F32)
                 + jnp.dot(ga_hi, wa2_lo, preferred_element_type=F32)
                 + jnp.dot(ga_lo, wa2_hi, preferred_element_type=F32)) + ba_ref[...]
        la_out[t * sub:(t + 1) * sub, :] = _log_sigmoid(a_pre) * (1.0 / GLA_TAU)


def _in_proj(x2, pos, g_mix, wa, wbt, wa2, ba, gq_col, gk_col, inv_col):
    s = x2.shape[0]
    tm = min(IN_TM, s)
    nb = s // MOBA_BLOCK
    nblk = tm // MOBA_BLOCK
    const = lambda shape: pl.BlockSpec(shape, lambda i: (0,) * len(shape),
                                       pipeline_mode=pl.Buffered(1))
    row = lambda w: pl.BlockSpec((tm, w), lambda i: (i, 0))
    hblk = lambda a, b: pl.BlockSpec((MOBA_HEADS, nblk, a, b), lambda i: (0, i, 0, 0))
    return pl.pallas_call(
        _in_proj_kernel,
        grid=(s // tm,),
        in_specs=[
            row(D_MODEL),
            pl.BlockSpec((1, tm), lambda i: (0, i)),
            const((1, D_MODEL)),
            const((D_MODEL, WA_W)),
            const((3 * MOBA_W, D_MODEL)),
            const((GA_PAD, GLA_QK_W)),
            const((1, GLA_QK_W)),
            const((MOBA_HD, 1)),
            const((MOBA_HD, 1)),
            const((MOBA_HD // 2, 1)),
        ],
        out_specs=[
            row(GLA_QK_W), row(GLA_QK_W), row(GLA_V_W), row(GLA_V_W), row(GLA_QK_W),
            hblk(MOBA_HD, MOBA_BLOCK), hblk(MOBA_BLOCK, MOBA_HD), hblk(MOBA_HD, MOBA_BLOCK),
            hblk(1, MOBA_HD),
        ],
        out_shape=[
            jax.ShapeDtypeStruct((s, GLA_QK_W), F32),
            jax.ShapeDtypeStruct((s, GLA_QK_W), F32),
            jax.ShapeDtypeStruct((s, GLA_V_W), BF16),
            jax.ShapeDtypeStruct((s, GLA_V_W), BF16),
            jax.ShapeDtypeStruct((s, GLA_QK_W), F32),
            jax.ShapeDtypeStruct((MOBA_HEADS, nb, MOBA_HD, MOBA_BLOCK), BF16),
            jax.ShapeDtypeStruct((MOBA_HEADS, nb, MOBA_BLOCK, MOBA_HD), BF16),
            jax.ShapeDtypeStruct((MOBA_HEADS, nb, MOBA_HD, MOBA_BLOCK), BF16),
            jax.ShapeDtypeStruct((MOBA_HEADS, nb, 1, MOBA_HD), F32),
        ],
        compiler_params=pltpu.CompilerParams(
            dimension_semantics=("arbitrary",), vmem_limit_bytes=VMEM_LIMIT_BYTES),
        name="in_proj",
    )(x2, pos, g_mix, wa, wbt, wa2, ba, gq_col, gk_col, inv_col)


def _gla_kernel(q_ref, k_ref, v_ref, gg_ref, la_ref, gout_ref, o_ref, st_ref):
    c = GLA_CHUNK
    hc = GLA_HEADS * c

    @pl.when(pl.program_id(0) == 0)
    def _():
        st_ref[...] = jnp.zeros_like(st_ref)

    r = lax.broadcasted_iota(jnp.int32, (hc, GLA_QK_W), 0)
    l = lax.broadcasted_iota(jnp.int32, (hc, GLA_QK_W), 1)
    same_head = (r // c) == (l // GLA_DK)
    causal = same_head & ((l % c) <= (r % c))
    ng = GLA_CHUNKS_PER_STEP
    gr = ng * c
    tr = lax.broadcasted_iota(jnp.int32, (gr, gr), 0)
    tc = lax.broadcasted_iota(jnp.int32, (gr, gr), 1)
    tri = ((tr // c == tc // c) & (tc <= tr)).astype(BF16)
    scale = GLA_DK ** -0.5

    def stack_heads(a, width):
        return jnp.concatenate([a[:, h * width:(h + 1) * width] for h in range(GLA_HEADS)], axis=0)

    def block_diag(a):
        return jnp.where(same_head, jnp.concatenate([a] * GLA_HEADS, axis=0), 0.0).astype(BF16)

    chunks = [slice(i * c, (i + 1) * c) for i in range(ng)]

    def cumsum_stage(g):
        la = la_ref[g * gr:(g + 1) * gr, :]
        hi = la.astype(BF16)
        r1 = la - hi.astype(F32)
        mid = r1.astype(BF16)
        lo = (r1 - mid.astype(F32)).astype(BF16)
        return (jnp.dot(tri, hi, preferred_element_type=F32)
                + jnp.dot(tri, mid, preferred_element_type=F32)
                + jnp.dot(tri, lo, preferred_element_type=F32))

    def score_stage(g, b):
        q = q_ref[g * gr:(g + 1) * gr, :]
        k = k_ref[g * gr:(g + 1) * gr, :]
        v = v_ref[g * gr:(g + 1) * gr, :]
        b_last = [b[(i + 1) * c - 1:(i + 1) * c, :] for i in range(ng)]
        qb = [block_diag(q[ch] * jnp.exp(b[ch]) * scale) for ch in chunks]
        kb = [block_diag(k[ch] * jnp.exp(-b[ch])) for ch in chunks]
        keb = [block_diag(k[ch] * jnp.exp(b_last[i] - b[ch])) for i, ch in enumerate(chunks)]
        vst = [stack_heads(v[ch], GLA_DV) for ch in chunks]
        att = [_dot_nt(qb[i], kb[i]) for i in range(ng)]
        dst = [_dot_tn(vst[i], keb[i]) for i in range(ng)]
        decay = [jnp.exp(bl) for bl in b_last]
        return qb, vst, att, dst, decay

    def output_stage(qb, vst, att, dst, decay):
        att = [jnp.where(causal, a, 0.0).astype(BF16) for a in att]
        intra = [jnp.dot(att[i], vst[i], preferred_element_type=F32) for i in range(ng)]
        st = st_ref[...]
        inter = []
        for i in range(ng):
            inter.append(_dot_nt(qb[i], st.astype(BF16)))
            st = st * decay[i] + dst[i]
        st_ref[...] = st
        return [intra[i] + inter[i] for i in range(ng)]

    def store_stage(g, outs):
        gg = gg_ref[g * gr:(g + 1) * gr, :]
        for i, ch in enumerate(chunks):
            o = _rms(outs[i], -1) * gout_ref[...]
            gate = stack_heads(gg[ch], GLA_DV).astype(F32)
            o = o * (gate * jax.nn.sigmoid(gate))
            for h in range(GLA_HEADS):
                o_ref[g * gr + i * c:g * gr + (i + 1) * c, h * GLA_DV:(h + 1) * GLA_DV] = (
                    o[h * c:(h + 1) * c].astype(o_ref.dtype))

    ngroups = q_ref.shape[0] // gr
    cums, scored, outs = {}, {}, {}
    for t in range(ngroups + 3):
        if t < ngroups:
            cums[t] = cumsum_stage(t)
        if 0 <= t - 1 < ngroups:
            scored[t - 1] = score_stage(t - 1, cums.pop(t - 1))
        if 0 <= t - 2 < ngroups:
            outs[t - 2] = output_stage(*scored.pop(t - 2))
        if 0 <= t - 3 < ngroups:
            store_stage(t - 3, outs.pop(t - 3))


def _gla(q, k, v, gg, la, g_out):
    s = q.shape[0]
    tm = min(GLA_TM, s)
    row = lambda w: pl.BlockSpec((tm, w), lambda i: (i, 0))
    return pl.pallas_call(
        _gla_kernel,
        grid=(s // tm,),
        in_specs=[row(GLA_QK_W), row(GLA_QK_W), row(GLA_V_W), row(GLA_V_W), row(GLA_QK_W),
                  pl.BlockSpec((1, GLA_DV), lambda i: (0, 0))],
        out_specs=row(GLA_V_W),
        out_shape=jax.ShapeDtypeStruct((s, GLA_V_W), BF16),
        scratch_shapes=[pltpu.VMEM((GLA_DV, GLA_QK_W), F32)],
        compiler_params=pltpu.CompilerParams(dimension_semantics=("arbitrary",)),
        name="gla",
    )(q, k, v, gg, la, g_out)


def _moba_kernel(qt_ref, k_ref, vt_ref, kmean_ref, o_ref, bias_ref, acc_ref, s_ref):
    qi = pl.program_id(0)
    heads, nb, hd = kmean_ref.shape
    tq = qt_ref.shape[2]
    kblk = k_ref.shape[2]
    grp = MOBA_BLOCKS_PER_STEP
    lag = MOBA_MATMUL_LAG
    nslots = s_ref.shape[0]

    def step_list(ngrp):
        steps = [(h, g) for g in range(ngrp) for h in range(heads)]
        assert len(steps) % nslots == 0 and nslots >= 2 * lag
        return steps

    def score_matmul(ngrp, base, n):
        steps = step_list(ngrp)
        span = grp * ngrp
        h, g = steps[n % len(steps)]
        blk0 = jnp.minimum(base + (n // len(steps)) * span, nb - span) + g * grp
        kg = k_ref[h, pl.ds(blk0, grp)].reshape(grp * kblk, -1)
        s_ref[n % nslots] = jnp.dot(kg, qt_ref[h], preferred_element_type=F32)

    pad = acc_ref.shape[1] - hd

    def values_t(blocks):
        keys = len(blocks) * kblk
        first_row = lax.broadcasted_iota(jnp.int32, (pad, keys), 0) == 0
        ones_rows = jnp.where(first_row, 1.0, 0.0).astype(BF16)
        return jnp.concatenate([jnp.concatenate(blocks, axis=1), ones_rows], axis=0)

    gates = []
    for h in range(heads):
        km = kmean_ref[h]
        hi = km.astype(BF16)
        r1 = km - hi.astype(F32)
        mid = r1.astype(BF16)
        lo = (r1 - mid.astype(F32)).astype(BF16)
        qt = qt_ref[h]
        gates.append(jnp.dot(hi, qt, preferred_element_type=F32)
                     + jnp.dot(mid, qt, preferred_element_type=F32)
                     + jnp.dot(lo, qt, preferred_element_type=F32))
    own = [jnp.dot(k_ref[h, qi], qt_ref[h], preferred_element_type=F32) for h in range(heads)]
    for n in range(lag):
        score_matmul(MOBA_STEPS_PER_ITER, 0, n)

    blk = lax.broadcasted_iota(jnp.int32, (nb, tq), 0)
    for h in range(heads):
        gate = jnp.where(blk < qi, gates[h], NEG_INF)
        sel = jnp.zeros((nb, tq), jnp.bool_)
        for _ in range(min(MOBA_TOPK, nb)):
            m = jnp.max(gate, axis=0, keepdims=True)
            first = jnp.min(jnp.where(gate == m, blk, nb), axis=0, keepdims=True)
            pick = (blk == first) & (m > NEG_INF)
            sel = sel | pick
            gate = jnp.where(pick, NEG_INF, gate)
        bias_ref[h] = jnp.where(sel, 0.0, NEG_INF)

    kk = lax.broadcasted_iota(jnp.int32, (kblk, tq), 0)
    qq = lax.broadcasted_iota(jnp.int32, (kblk, tq), 1)
    causal = kk <= qq
    stats, probs = [], []
    for h in range(heads):
        s = jnp.where(causal, own[h], NEG_INF)
        m = jnp.max(s, axis=0, keepdims=True)
        stats.append(m)
        probs.append(jnp.exp2(s - m).astype(BF16))
    for h in range(heads):
        acc_ref[h] = jnp.dot(values_t([vt_ref[h, qi]]), probs[h], preferred_element_type=F32)

    def past_blocks(ngrp, first_block):
        span = grp * ngrp

        def body(it, carry):
            base = pl.multiple_of(first_block + it * span, grp * MOBA_STEPS_PER_ITER)
            stats = list(carry)
            for n, (h, g) in enumerate(step_list(ngrp)):
                score_matmul(ngrp, base, n + lag)
                j0 = base + g * grp
                m = stats[h]
                s = [s_ref[n % nslots, u * kblk:(u + 1) * kblk] for u in range(grp)]
                bias = [bias_ref[h, pl.ds(j0 + u, 1), :] for u in range(grp)]
                m_new = m
                for u in range(grp):
                    m_new = jnp.maximum(m_new, jnp.max(s[u], axis=0, keepdims=True) + bias[u])
                alpha = jnp.exp2(m - m_new)
                pb = jnp.concatenate(
                    [jnp.exp2(s[u] + (bias[u] - m_new)).astype(BF16) for u in range(grp)], axis=0)
                vt = values_t([vt_ref[h, j0 + u] for u in range(grp)])
                acc_ref[h] = alpha * acc_ref[h] + jnp.dot(vt, pb, preferred_element_type=F32)
                stats[h] = m_new
            return tuple(stats)

        return body

    long_ngrp = MOBA_STEPS_PER_LONG_ITER
    short_span, long_span = grp * MOBA_STEPS_PER_ITER, grp * long_ngrp
    assert step_list(long_ngrp)[:lag] == step_list(MOBA_STEPS_PER_ITER)[:lag]
    n_long = qi // long_span
    n_short = (qi - n_long * long_span + short_span - 1) // short_span
    stats = lax.fori_loop(0, n_long, past_blocks(long_ngrp, 0), tuple(stats))
    lax.fori_loop(0, n_short, past_blocks(MOBA_STEPS_PER_ITER, n_long * long_span), stats)
    for h in range(heads):
        acc = acc_ref[h]
        o_ref[:, h * hd:(h + 1) * hd] = (acc[:hd] / acc[hd:hd + 1]).T.astype(o_ref.dtype)


def _moba(qt, kb, vt, kmean):
    heads, nb, hd, blk = qt.shape
    s = nb * blk
    assert nb % (MOBA_BLOCKS_PER_STEP * MOBA_STEPS_PER_ITER) == 0
    assert nb % (MOBA_BLOCKS_PER_STEP * MOBA_STEPS_PER_LONG_ITER) == 0
    const = lambda shape: pl.BlockSpec(shape, lambda i: (0,) * len(shape),
                                       pipeline_mode=pl.Buffered(1))
    return pl.pallas_call(
        _moba_kernel,
        grid=(nb,),
        in_specs=[
            pl.BlockSpec((heads, None, hd, blk), lambda i: (0, i, 0, 0)),
            const((heads, nb, blk, hd)),
            const((heads, nb, hd, blk)),
            const((heads, nb, hd)),
        ],
        out_specs=pl.BlockSpec((blk, heads * hd), lambda i: (i, 0)),
        out_shape=jax.ShapeDtypeStruct((s, heads * hd), BF16),
        scratch_shapes=[
            pltpu.VMEM((heads, nb, blk), F32),
            pltpu.VMEM((heads, hd + BF16_SUBLANES, blk), F32),
            pltpu.VMEM((MOBA_SCORE_SLOTS, MOBA_BLOCKS_PER_STEP * blk, blk), F32),
        ],
        compiler_params=pltpu.CompilerParams(
            dimension_semantics=("arbitrary",), vmem_limit_bytes=VMEM_LIMIT_BYTES),
        name="moba",
    )(qt, kb, vt, kmean)


def _out_mlp_kernel(x_ref, og_ref, om_ref, p_ref, wo_ref, gffn_ref, wup_ref, wdn_ref,
                    gpg_ref, wpg_ref, wpp_ref, gpe_ref, o_ref):
    nsub = OUT_SUBTILES
    sub = x_ref.shape[0] // nsub
    rows = [slice(t * sub, (t + 1) * sub) for t in range(nsub)]

    e = [jnp.dot(p_ref[r, :].astype(BF16), wpp_ref[...], preferred_element_type=F32) for r in rows]
    h = [x_ref[r, :]
         + jnp.dot(og_ref[r, :], wo_ref[:GLA_V_W, :], preferred_element_type=F32)
         + jnp.dot(om_ref[r, :], wo_ref[GLA_V_W:, :], preferred_element_type=F32) for r in rows]

    u = [(_rms(ht, -1) * gffn_ref[...]).astype(BF16) for ht in h]
    mlp = [None] * nsub
    for c0 in range(0, D_FF, FF_CHUNK):
        f = [jnp.dot(ut, wup_ref[:, c0:c0 + FF_CHUNK], preferred_element_type=F32) for ut in u]
        f = [jnp.square(jnp.maximum(ft, 0.0)).astype(BF16) for ft in f]
        d = [jnp.dot(ft, wdn_ref[c0:c0 + FF_CHUNK, :], preferred_element_type=F32) for ft in f]
        mlp = [dt if mt is None else mt + dt for mt, dt in zip(mlp, d)]
    h = [ht + mt for ht, mt in zip(h, mlp)]

    u = [(_rms(ht, -1) * gpg_ref[...]).astype(BF16) for ht in h]
    gate = [jax.nn.sigmoid(jnp.dot(ut, wpg_ref[...], preferred_element_type=F32)) for ut in u]
    for t, r in enumerate(rows):
        o_ref[r, :] = h[t] + gate[t] * (_rms(e[t], -1) * gpe_ref[...])


def _out_mlp(x2, og, om, p2, wo, g_ffn, wup, wdn, g_pg, wpg, wpp, g_pe):
    s = x2.shape[0]
    tm = min(OUT_TM, s)
    const = lambda shape: pl.BlockSpec(shape, lambda i: (0,) * len(shape),
                                       pipeline_mode=pl.Buffered(1))
    row = lambda w: pl.BlockSpec((tm, w), lambda i: (i, 0))
    return pl.pallas_call(
        _out_mlp_kernel,
        grid=(s // tm,),
        in_specs=[
            row(D_MODEL), row(GLA_V_W), row(MOBA_W), row(PLE_DIM),
            const((GLA_V_W + MOBA_W, D_MODEL)),
            const((1, D_MODEL)),
            const((D_MODEL, D_FF)),
            const((D_FF, D_MODEL)),
            const((1, D_MODEL)),
            const((D_MODEL, D_MODEL)),
            const((PLE_DIM, D_MODEL)),
            const((1, D_MODEL)),
        ],
        out_specs=row(D_MODEL),
        out_shape=jax.ShapeDtypeStruct((s, D_MODEL), F32),
        compiler_params=pltpu.CompilerParams(
            dimension_semantics=("arbitrary",), vmem_limit_bytes=VMEM_LIMIT_BYTES),
        name="out_mlp",
    )(x2, og, om, p2, wo, g_ffn, wup, wdn, g_pg, wpg, wpp, g_pe)


def _layer(h2, p2, pos, g_mix, w_in, w_gla_a2, b_gla_a, g_gla_out, g_moba_q, g_moba_k,
           w_out, g_ffn, w_up, w_down, g_ple_gate, w_ple_gate, w_ple_proj, g_ple_emb):
    s = h2.shape[0]
    assert s % MOBA_BLOCK == 0 and s % GLA_CHUNK == 0
    assert s % min(IN_TM, s) == 0 and s % min(GLA_TM, s) == 0 and s % min(OUT_TM, s) == 0
    assert min(GLA_TM, s) % (GLA_CHUNKS_PER_STEP * GLA_CHUNK) == 0

    wa = jnp.concatenate(
        [w_in[:, :GLA_IN_W], jnp.zeros((D_MODEL, GA_PAD - GLA_LOWRANK), w_in.dtype)],
        axis=1).astype(BF16)
    wbt = w_in[:, GLA_IN_W:].T.astype(BF16)
    wa2 = jnp.concatenate(
        [w_gla_a2, jnp.zeros((GA_PAD - GLA_LOWRANK, GLA_QK_W), w_gla_a2.dtype)], axis=0)
    half = MOBA_HD // 2
    inv_col = (1.0 / (ROPE_THETA ** (jnp.arange(half, dtype=F32) / half))).reshape(half, 1)

    gq, gk, gv, gg, la, qt, kb, vt, kmean = _in_proj(
        h2, pos, g_mix.reshape(1, -1), wa, wbt, wa2, b_gla_a.reshape(1, -1),
        g_moba_q.reshape(-1, 1), g_moba_k.reshape(-1, 1), inv_col)
    o_gla = _gla(gq, gk, gv, gg, la, g_gla_out.reshape(1, -1))
    o_moba = _moba(qt, kb, vt, kmean.reshape(MOBA_HEADS, s // MOBA_BLOCK, MOBA_HD))
    return _out_mlp(
        h2, o_gla, o_moba, p2, w_out.astype(BF16), g_ffn.reshape(1, -1),
        w_up.astype(BF16), w_down.astype(BF16), g_ple_gate.reshape(1, -1),
        w_ple_gate.astype(BF16), w_ple_proj.astype(BF16), g_ple_emb.reshape(1, -1))


def kernel(x, p, positions, g_mix, w_in, w_gla_a2, b_gla_a, g_gla_out, g_moba_q, g_moba_k,
           w_out, g_ffn, w_up, w_down, g_ple_gate, w_ple_gate, w_ple_proj, g_ple_emb):
    depth = p.shape[0]
    batch = x.shape[0]
    outs = []
    for b in range(batch):
        h = x[b]
        pos = positions[b:b + 1]
        for i in range(depth):
            h = _layer(h, p[i, b], pos, g_mix[i], w_in[i], w_gla_a2[i], b_gla_a[i],
                       g_gla_out[i], g_moba_q[i], g_moba_k[i], w_out[i], g_ffn[i], w_up[i],
                       w_down[i], g_ple_gate[i], w_ple_gate[i], w_ple_proj[i], g_ple_emb[i])
        outs.append(h)
    return jnp.stack(outs, axis=0)
```

```python
import functools
import math

import numpy as np
import jax
import jax.numpy as jnp
from jax import lax
from jax.experimental import pallas as pl
from jax.experimental.pallas import tpu as pltpu

F32 = jnp.float32
BF16 = jnp.bfloat16
HIGHEST = lax.Precision.HIGHEST

D_MODEL = 1024
PLE_DIM = 256
GLA_HEADS = 4
GLA_DK = 64
GLA_DV = 128
GLA_LOWRANK = 16
GLA_TAU = 16.0
GLA_CHUNK = 64
MOBA_HEADS = 4
MOBA_HD = 128
MOBA_BLOCK = 256
MOBA_TOPK = 3
ROPE_THETA = 10000.0
D_FF = 4 * D_MODEL
EPS = 1e-6

GLA_QK_W = GLA_HEADS * GLA_DK
GLA_V_W = GLA_HEADS * GLA_DV
MOBA_W = MOBA_HEADS * MOBA_HD
GLA_IN_W = 2 * GLA_QK_W + 2 * GLA_V_W + GLA_LOWRANK
LANES = 128
BF16_SUBLANES = 16
GA_PAD = LANES
WA_W = 2 * GLA_QK_W + 2 * GLA_V_W + GA_PAD

VMEM_LIMIT_BYTES = 56 * 1024 * 1024

IN_TM = 1024
GLA_CHUNKS_PER_STEP = 4
GLA_TM = 1024
OUT_TM = 512
OUT_SUBTILES = 2
FF_CHUNK = 1024
MOBA_BLOCKS_PER_STEP = 2
MOBA_STEPS_PER_ITER = 2
MOBA_STEPS_PER_LONG_ITER = 4
MOBA_MATMUL_LAG = 3
MOBA_SCORE_SLOTS = 8

MOBA_QSCALE = (MOBA_HD ** -0.5) * math.log2(math.e)
NEG_INF = float("-inf")


def _rms(x, axis):
    return x * lax.rsqrt(jnp.mean(x * x, axis=axis, keepdims=True) + EPS)


def _log_sigmoid(a):
    return jnp.minimum(a, 0.0) - jnp.log1p(jnp.exp(-jnp.abs(a)))


def _dot_nt(a, b):
    return lax.dot_general(a, b, (((1,), (1,)), ((), ())), preferred_element_type=F32)


def _dot_tn(a, b):
    return lax.dot_general(a, b, (((0,), (0,)), ((), ())), preferred_element_type=F32)


def _in_proj_kernel(x_ref, pos_ref, gmix_ref, wa_ref, wbt_ref, wa2_ref, ba_ref,
                    gq_ref, gk_ref, inv_ref,
                    q_out, k_out, v_out, gg_out, la_out,
                    qt_out, kb_out, vt_out, kmean_out):
    tm = x_ref.shape[0]
    sub = MOBA_BLOCK
    nsub = tm // sub
    half = MOBA_HD // 2
    wa2 = wa2_ref[...]
    wa2_hi = wa2.astype(BF16)
    wa2_lo = (wa2 - wa2_hi.astype(F32)).astype(BF16)

    us = [(_rms(x_ref[t * sub:(t + 1) * sub, :], -1) * gmix_ref[...]).astype(BF16)
          for t in range(nsub)]
    zts, zas = [], []
    for t in range(nsub):
        zts.append(_dot_nt(wbt_ref[...], us[t]))
        zas.append(jnp.dot(us[t], wa_ref[...], preferred_element_type=F32))

    for t in range(nsub):
        rows = slice(t * sub, (t + 1) * sub)

        zt = zts[t]
        ang = inv_ref[...] * pos_ref[:, rows].astype(F32)
        cos = jnp.cos(ang)
        sin = jnp.sin(ang)

        def norm_rope(a, g_col):
            a = _rms(a, 0) * g_col
            a1, a2 = a[:half], a[half:]
            return jnp.concatenate([a1 * cos - a2 * sin, a2 * cos + a1 * sin], axis=0)

        for h in range(MOBA_HEADS):
            r0 = h * MOBA_HD
            qr = norm_rope(zt[r0:r0 + MOBA_HD], gq_ref[...]) * MOBA_QSCALE
            kr = norm_rope(zt[MOBA_W + r0:MOBA_W + r0 + MOBA_HD], gk_ref[...])
            kblk = kr.T
            qt_out[h, t] = qr.astype(BF16)
            vt_out[h, t] = zt[2 * MOBA_W + r0:2 * MOBA_W + r0 + MOBA_HD].astype(BF16)
            kb_out[h, t] = kblk.astype(BF16)
            kmean_out[h, t] = jnp.mean(kblk, axis=0, keepdims=True)

        za = zas[t]
        q_out[rows, :] = za[:, 0:GLA_QK_W]
        k_out[rows, :] = za[:, GLA_QK_W:2 * GLA_QK_W]
        v_out[rows, :] = za[:, 2 * GLA_QK_W:2 * GLA_QK_W + GLA_V_W].astype(BF16)
        gg_out[rows, :] = za[:, 2 * GLA_QK_W + GLA_V_W:2 * GLA_QK_W + 2 * GLA_V_W].astype(BF16)

    for t in range(nsub):
        ga = zas[t][:, 2 * GLA_QK_W + 2 * GLA_V_W:]
        ga_hi = ga.astype(BF16)
        ga_lo = (ga - ga_hi.astype(F32)).astype(BF16)
        a_pre = (jnp.dot(ga_hi, wa2_hi, preferred_element_type=F32)
                 + jnp.dot(ga_hi, wa2_lo, preferred_element_type=F32)
                 + jnp.dot(ga_lo, wa2_hi, preferred_element_type=F32)) + ba_ref[...]
        la_out[t * sub:(t + 1) * sub, :] = _log_sigmoid(a_pre) * (1.0 / GLA_TAU)


def _in_proj(x2, pos, g_mix, wa, wbt, wa2, ba, gq_col, gk_col, inv_col):
    s = x2.shape[0]
    tm = min(IN_TM, s)
    nb = s // MOBA_BLOCK
    nblk = tm // MOBA_BLOCK
    const = lambda shape: pl.BlockSpec(shape, lambda i: (0,) * len(shape),
                                       pipeline_mode=pl.Buffered(1))
    row = lambda w: pl.BlockSpec((tm, w), lambda i: (i, 0))
    hblk = lambda a, b: pl.BlockSpec((MOBA_HEADS, nblk, a, b), lambda i: (0, i, 0, 0))
    return pl.pallas_call(
        _in_proj_kernel,
        grid=(s // tm,),
        in_specs=[
            row(D_MODEL),
            pl.BlockSpec((1, tm), lambda i: (0, i)),
            const((1, D_MODEL)),
            const((D_MODEL, WA_W)),
            const((3 * MOBA_W, D_MODEL)),
            const((GA_PAD, GLA_QK_W)),
            const((1, GLA_QK_W)),
            const((MOBA_HD, 1)),
            const((MOBA_HD, 1)),
            const((MOBA_HD // 2, 1)),
        ],
        out_specs=[
            row(GLA_QK_W), row(GLA_QK_W), row(GLA_V_W), row(GLA_V_W), row(GLA_QK_W),
            hblk(MOBA_HD, MOBA_BLOCK), hblk(MOBA_BLOCK, MOBA_HD), hblk(MOBA_HD, MOBA_BLOCK),
            hblk(1, MOBA_HD),
        ],
        out_shape=[
            jax.ShapeDtypeStruct((s, GLA_QK_W), F32),
            jax.ShapeDtypeStruct((s, GLA_QK_W), F32),
            jax.ShapeDtypeStruct((s, GLA_V_W), BF16),
            jax.ShapeDtypeStruct((s, GLA_V_W), BF16),
            jax.ShapeDtypeStruct((s, GLA_QK_W), F32),
            jax.ShapeDtypeStruct((MOBA_HEADS, nb, MOBA_HD, MOBA_BLOCK), BF16),
            jax.ShapeDtypeStruct((MOBA_HEADS, nb, MOBA_BLOCK, MOBA_HD), BF16),
            jax.ShapeDtypeStruct((MOBA_HEADS, nb, MOBA_HD, MOBA_BLOCK), BF16),
            jax.ShapeDtypeStruct((MOBA_HEADS, nb, 1, MOBA_HD), F32),
        ],
        compiler_params=pltpu.CompilerParams(
            dimension_semantics=("arbitrary",), vmem_limit_bytes=VMEM_LIMIT_BYTES),
        name="in_proj",
    )(x2, pos, g_mix, wa, wbt, wa2, ba, gq_col, gk_col, inv_col)


def _gla_kernel(q_ref, k_ref, v_ref, gg_ref, la_ref, gout_ref, o_ref, st_ref):
    c = GLA_CHUNK
    hc = GLA_HEADS * c

    @pl.when(pl.program_id(0) == 0)
    def _():
        st_ref[...] = jnp.zeros_like(st_ref)

    r = lax.broadcasted_iota(jnp.int32, (hc, GLA_QK_W), 0)
    l = lax.broadcasted_iota(jnp.int32, (hc, GLA_QK_W), 1)
    same_head = (r // c) == (l // GLA_DK)
    causal = same_head & ((l % c) <= (r % c))
    ng = GLA_CHUNKS_PER_STEP
    gr = ng * c
    tr = lax.broadcasted_iota(jnp.int32, (gr, gr), 0)
    tc = lax.broadcasted_iota(jnp.int32, (gr, gr), 1)
    tri = ((tr // c == tc // c) & (tc <= tr)).astype(BF16)
    scale = GLA_DK ** -0.5

    def stack_heads(a, width):
        return jnp.concatenate([a[:, h * width:(h + 1) * width] for h in range(GLA_HEADS)], axis=0)

    def block_diag(a):
        return jnp.where(same_head, jnp.concatenate([a] * GLA_HEADS, axis=0), 0.0).astype(BF16)

    chunks = [slice(i * c, (i + 1) * c) for i in range(ng)]

    def cumsum_stage(g):
        la = la_ref[g * gr:(g + 1) * gr, :]
        hi = la.astype(BF16)
        r1 = la - hi.astype(F32)
        mid = r1.astype(BF16)
        lo = (r1 - mid.astype(F32)).astype(BF16)
        return (jnp.dot(tri, hi, preferred_element_type=F32)
                + jnp.dot(tri, mid, preferred_element_type=F32)
                + jnp.dot(tri, lo, preferred_element_type=F32))

    def score_stage(g, b):
        q = q_ref[g * gr:(g + 1) * gr, :]
        k = k_ref[g * gr:(g + 1) * gr, :]
        v = v_ref[g * gr:(g + 1) * gr, :]
        b_last = [b[(i + 1) * c - 1:(i + 1) * c, :] for i in range(ng)]
        qb = [block_diag(q[ch] * jnp.exp(b[ch]) * scale) for ch in chunks]
        kb = [block_diag(k[ch] * jnp.exp(-b[ch])) for ch in chunks]
        keb = [block_diag(k[ch] * jnp.exp(b_last[i] - b[ch])) for i, ch in enumerate(chunks)]
        vst = [stack_heads(v[ch], GLA_DV) for ch in chunks]
        att = [_dot_nt(qb[i], kb[i]) for i in range(ng)]
        dst = [_dot_tn(vst[i], keb[i]) for i in range(ng)]
        decay = [jnp.exp(bl) for bl in b_last]
        return qb, vst, att, dst, decay

    def output_stage(qb, vst, att, dst, decay):
        att = [jnp.where(causal, a, 0.0).astype(BF16) for a in att]
        intra = [jnp.dot(att[i], vst[i], preferred_element_type=F32) for i in range(ng)]
        st = st_ref[...]
        inter = []
        for i in range(ng):
            inter.append(_dot_nt(qb[i], st.astype(BF16)))
            st = st * decay[i] + dst[i]
        st_ref[...] = st
        return [intra[i] + inter[i] for i in range(ng)]

    def store_stage(g, outs):
        gg = gg_ref[g * gr:(g + 1) * gr, :]
        for i, ch in enumerate(chunks):
            o = _rms(outs[i], -1) * gout_ref[...]
            gate = stack_heads(gg[ch], GLA_DV).astype(F32)
            o = o * (gate * jax.nn.sigmoid(gate))
            for h in range(GLA_HEADS):
                o_ref[g * gr + i * c:g * gr + (i + 1) * c, h * GLA_DV:(h + 1) * GLA_DV] = (
                    o[h * c:(h + 1) * c].astype(o_ref.dtype))

    ngroups = q_ref.shape[0] // gr
    cums, scored, outs = {}, {}, {}
    for t in range(ngroups + 3):
        if t < ngroups:
            cums[t] = cumsum_stage(t)
        if 0 <= t - 1 < ngroups:
            scored[t - 1] = score_stage(t - 1, cums.pop(t - 1))
        if 0 <= t - 2 < ngroups:
            outs[t - 2] = output_stage(*scored.pop(t - 2))
        if 0 <= t - 3 < ngroups:
            store_stage(t - 3, outs.pop(t - 3))


def _gla(q, k, v, gg, la, g_out):
    s = q.shape[0]
    tm = min(GLA_TM, s)
    row = lambda w: pl.BlockSpec((tm, w), lambda i: (i, 0))
    return pl.pallas_call(
        _gla_kernel,
        grid=(s // tm,),
        in_specs=[row(GLA_QK_W), row(GLA_QK_W), row(GLA_V_W), row(GLA_V_W), row(GLA_QK_W),
                  pl.BlockSpec((1, GLA_DV), lambda i: (0, 0))],
        out_specs=row(GLA_V_W),
        out_shape=jax.ShapeDtypeStruct((s, GLA_V_W), BF16),
        scratch_shapes=[pltpu.VMEM((GLA_DV, GLA_QK_W), F32)],
        compiler_params=pltpu.CompilerParams(dimension_semantics=("arbitrary",)),
        name="gla",
    )(q, k, v, gg, la, g_out)


def _moba_kernel(qt_ref, k_ref, vt_ref, kmean_ref, o_ref, bias_ref, acc_ref, s_ref):
    qi = pl.program_id(0)
    heads, nb, hd = kmean_ref.shape
    tq = qt_ref.shape[2]
    kblk = k_ref.shape[2]
    grp = MOBA_BLOCKS_PER_STEP
    lag = MOBA_MATMUL_LAG
    nslots = s_ref.shape[0]

    def step_list(ngrp):
        steps = [(h, g) for g in range(ngrp) for h in range(heads)]
        assert len(steps) % nslots == 0 and nslots >= 2 * lag
        return steps

    def score_matmul(ngrp, base, n):
        steps = step_list(ngrp)
        span = grp * ngrp
        h, g = steps[n % len(steps)]
        blk0 = jnp.minimum(base + (n // len(steps)) * span, nb - span) + g * grp
        kg = k_ref[h, pl.ds(blk0, grp)].reshape(grp * kblk, -1)
        s_ref[n % nslots] = jnp.dot(kg, qt_ref[h], preferred_element_type=F32)

    pad = acc_ref.shape[1] - hd

    def values_t(blocks):
        keys = len(blocks) * kblk
        first_row = lax.broadcasted_iota(jnp.int32, (pad, keys), 0) == 0
        ones_rows = jnp.where(first_row, 1.0, 0.0).astype(BF16)
        return jnp.concatenate([jnp.concatenate(blocks, axis=1), ones_rows], axis=0)

    gates = []
    for h in range(heads):
        km = kmean_ref[h]
        hi = km.astype(BF16)
        r1 = km - hi.astype(F32)
        mid = r1.astype(BF16)
        lo = (r1 - mid.astype(F32)).astype(BF16)
        qt = qt_ref[h]
        gates.append(jnp.dot(hi, qt, preferred_element_type=F32)
                     + jnp.dot(mid, qt, preferred_element_type=F32)
                     + jnp.dot(lo, qt, preferred_element_type=F32))
    own = [jnp.dot(k_ref[h, qi], qt_ref[h], preferred_element_type=F32) for h in range(heads)]
    for n in range(lag):
        score_matmul(MOBA_STEPS_PER_ITER, 0, n)

    blk = lax.broadcasted_iota(jnp.int32, (nb, tq), 0)
    for h in range(heads):
        gate = jnp.where(blk < qi, gates[h], NEG_INF)
        sel = jnp.zeros((nb, tq), jnp.bool_)
        for _ in range(min(MOBA_TOPK, nb)):
            m = jnp.max(gate, axis=0, keepdims=True)
            first = jnp.min(jnp.where(gate == m, blk, nb), axis=0, keepdims=True)
            pick = (blk == first) & (m > NEG_INF)
            sel = sel | pick
            gate = jnp.where(pick, NEG_INF, gate)
        bias_ref[h] = jnp.where(sel, 0.0, NEG_INF)

    kk = lax.broadcasted_iota(jnp.int32, (kblk, tq), 0)
    qq = lax.broadcasted_iota(jnp.int32, (kblk, tq), 1)
    causal = kk <= qq
    stats, probs = [], []
    for h in range(heads):
        s = jnp.where(causal, own[h], NEG_INF)
        m = jnp.max(s, axis=0, keepdims=True)
        stats.append(m)
        probs.append(jnp.exp2(s - m).astype(BF16))
    for h in range(heads):
        acc_ref[h] = jnp.dot(values_t([vt_ref[h, qi]]), probs[h], preferred_element_type=F32)

    def past_blocks(ngrp, first_block):
        span = grp * ngrp

        def body(it, carry):
            base = pl.multiple_of(first_block + it * span, grp * MOBA_STEPS_PER_ITER)
            stats = list(carry)
            for n, (h, g) in enumerate(step_list(ngrp)):
                score_matmul(ngrp, base, n + lag)
                j0 = base + g * grp
                m = stats[h]
                s = [s_ref[n % nslots, u * kblk:(u + 1) * kblk] for u in range(grp)]
                bias = [bias_ref[h, pl.ds(j0 + u, 1), :] for u in range(grp)]
                m_new = m
                for u in range(grp):
                    m_new = jnp.maximum(m_new, jnp.max(s[u], axis=0, keepdims=True) + bias[u])
                alpha = jnp.exp2(m - m_new)
                pb = jnp.concatenate(
                    [jnp.exp2(s[u] + (bias[u] - m_new)).astype(BF16) for u in range(grp)], axis=0)
                vt = values_t([vt_ref[h, j0 + u] for u in range(grp)])
                acc_ref[h] = alpha * acc_ref[h] + jnp.dot(vt, pb, preferred_element_type=F32)
                stats[h] = m_new
            return tuple(stats)

        return body

    long_ngrp = MOBA_STEPS_PER_LONG_ITER
    short_span, long_span = grp * MOBA_STEPS_PER_ITER, grp * long_ngrp
    assert step_list(long_ngrp)[:lag] == step_list(MOBA_STEPS_PER_ITER)[:lag]
    n_long = qi // long_span
    n_short = (qi - n_long * long_span + short_span - 1) // short_span
    stats = lax.fori_loop(0, n_long, past_blocks(long_ngrp, 0), tuple(stats))
    lax.fori_loop(0, n_short, past_blocks(MOBA_STEPS_PER_ITER, n_long * long_span), stats)
    for h in range(heads):
        acc = acc_ref[h]
        o_ref[:, h * hd:(h + 1) * hd] = (acc[:hd] / acc[hd:hd + 1]).T.astype(o_ref.dtype)


def _moba(qt, kb, vt, kmean):
    heads, nb, hd, blk = qt.shape
    s = nb * blk
    assert nb % (MOBA_BLOCKS_PER_STEP * MOBA_STEPS_PER_ITER) == 0
    assert nb % (MOBA_BLOCKS_PER_STEP * MOBA_STEPS_PER_LONG_ITER) == 0
    const = lambda shape: pl.BlockSpec(shape, lambda i: (0,) * len(shape),
                                       pipeline_mode=pl.Buffered(1))
    return pl.pallas_call(
        _moba_kernel,
        grid=(nb,),
        in_specs=[
            pl.BlockSpec((heads, None, hd, blk), lambda i: (0, i, 0, 0)),
            const((heads, nb, blk, hd)),
            const((heads, nb, hd, blk)),
            const((heads, nb, hd)),
        ],
        out_specs=pl.BlockSpec((blk, heads * hd), lambda i: (i, 0)),
        out_shape=jax.ShapeDtypeStruct((s, heads * hd), BF16),
        scratch_shapes=[
            pltpu.VMEM((heads, nb, blk), F32),
            pltpu.VMEM((heads, hd + BF16_SUBLANES, blk), F32),
            pltpu.VMEM((MOBA_SCORE_SLOTS, MOBA_BLOCKS_PER_STEP * blk, blk), F32),
        ],
        compiler_params=pltpu.CompilerParams(
            dimension_semantics=("arbitrary",), vmem_limit_bytes=VMEM_LIMIT_BYTES),
        name="moba",
    )(qt, kb, vt, kmean)


def _out_mlp_kernel(x_ref, og_ref, om_ref, p_ref, wo_ref, gffn_ref, wup_ref, wdn_ref,
                    gpg_ref, wpg_ref, wpp_ref, gpe_ref, o_ref):
    nsub = OUT_SUBTILES
    sub = x_ref.shape[0] // nsub
    rows = [slice(t * sub, (t + 1) * sub) for t in range(nsub)]

    e = [jnp.dot(p_ref[r, :].astype(BF16), wpp_ref[...], preferred_element_type=F32) for r in rows]
    h = [x_ref[r, :]
         + jnp.dot(og_ref[r, :], wo_ref[:GLA_V_W, :], preferred_element_type=F32)
         + jnp.dot(om_ref[r, :], wo_ref[GLA_V_W:, :], preferred_element_type=F32) for r in rows]

    u = [(_rms(ht, -1) * gffn_ref[...]).astype(BF16) for ht in h]
    mlp = [None] * nsub
    for c0 in range(0, D_FF, FF_CHUNK):
        f = [jnp.dot(ut, wup_ref[:, c0:c0 + FF_CHUNK], preferred_element_type=F32) for ut in u]
        f = [jnp.square(jnp.maximum(ft, 0.0)).astype(BF16) for ft in f]
        d = [jnp.dot(ft, wdn_ref[c0:c0 + FF_CHUNK, :], preferred_element_type=F32) for ft in f]
        mlp = [dt if mt is None else mt + dt for mt, dt in zip(mlp, d)]
    h = [ht + mt for ht, mt in zip(h, mlp)]

    u = [(_rms(ht, -1) * gpg_ref[...]).astype(BF16) for ht in h]
    gate = [jax.nn.sigmoid(jnp.dot(ut, wpg_ref[...], preferred_element_type=F32)) for ut in u]
    for t, r in enumerate(rows):
        o_ref[r, :] = h[t] + gate[t] * (_rms(e[t], -1) * gpe_ref[...])


def _out_mlp(x2, og, om, p2, wo, g_ffn, wup, wdn, g_pg, wpg, wpp, g_pe):
    s = x2.shape[0]
    tm = min(OUT_TM, s)
    const = lambda shape: pl.BlockSpec(shape, lambda i: (0,) * len(shape),
                                       pipeline_mode=pl.Buffered(1))
    row = lambda w: pl.BlockSpec((tm, w), lambda i: (i, 0))
    return pl.pallas_call(
        _out_mlp_kernel,
        grid=(s // tm,),
        in_specs=[
            row(D_MODEL), row(GLA_V_W), row(MOBA_W), row(PLE_DIM),
            const((GLA_V_W + MOBA_W, D_MODEL)),
            const((1, D_MODEL)),
            const((D_MODEL, D_FF)),
            const((D_FF, D_MODEL)),
            const((1, D_MODEL)),
            const((D_MODEL, D_MODEL)),
            const((PLE_DIM, D_MODEL)),
            const((1, D_MODEL)),
        ],
        out_specs=row(D_MODEL),
        out_shape=jax.ShapeDtypeStruct((s, D_MODEL), F32),
        compiler_params=pltpu.CompilerParams(
            dimension_semantics=("arbitrary",), vmem_limit_bytes=VMEM_LIMIT_BYTES),
        name="out_mlp",
    )(x2, og, om, p2, wo, g_ffn, wup, wdn, g_pg, wpg, wpp, g_pe)


def _sc_gather_rows(table, idx, chunk=128):
    from jax.experimental.pallas import tpu_sc as plsc
    info = plsc.get_sparse_core_info()
    nc, ns = info.num_cores, info.num_subcores
    nw = nc * ns
    b, d = idx.shape[0], table.shape[1]
    per_w = b // nw
    assert b % (nw * chunk) == 0
    mesh = plsc.VectorSubcoreMesh(core_axis_name="c", subcore_axis_name="s")

    @functools.partial(
        pl.kernel, mesh=mesh,
        out_type=jax.ShapeDtypeStruct((b, d), table.dtype),
        scratch_types=[pltpu.VMEM((chunk,), jnp.int32), pltpu.VMEM((chunk, d), table.dtype),
                       pltpu.SemaphoreType.DMA],
        name="sc_gather_rows")
    def body(table_hbm, idx_hbm, out_hbm, idx_v, rows_v, sem):
        wid = lax.axis_index("s") * nc + lax.axis_index("c")

        @pl.loop(0, per_w // chunk)
        def _(ci):
            base = wid * per_w + ci * chunk
            pltpu.sync_copy(idx_hbm.at[pl.ds(base, chunk)], idx_v)
            pltpu.async_copy(table_hbm.at[idx_v], rows_v, sem).wait()
            pltpu.sync_copy(rows_v, out_hbm.at[pl.ds(base, chunk)])

    return body(table, idx)


def _layer(h2, p2, pos, g_mix, w_in, w_gla_a2, b_gla_a, g_gla_out, g_moba_q, g_moba_k,
           w_out, g_ffn, w_up, w_down, g_ple_gate, w_ple_gate, w_ple_proj, g_ple_emb):
    s = h2.shape[0]
    assert s % MOBA_BLOCK == 0 and s % GLA_CHUNK == 0
    assert s % min(IN_TM, s) == 0 and s % min(GLA_TM, s) == 0 and s % min(OUT_TM, s) == 0
    assert min(GLA_TM, s) % (GLA_CHUNKS_PER_STEP * GLA_CHUNK) == 0

    wa = jnp.concatenate(
        [w_in[:, :GLA_IN_W], jnp.zeros((D_MODEL, GA_PAD - GLA_LOWRANK), w_in.dtype)],
        axis=1).astype(BF16)
    wbt = w_in[:, GLA_IN_W:].T.astype(BF16)
    wa2 = jnp.concatenate(
        [w_gla_a2, jnp.zeros((GA_PAD - GLA_LOWRANK, GLA_QK_W), w_gla_a2.dtype)], axis=0)
    half = MOBA_HD // 2
    inv_col = (1.0 / (ROPE_THETA ** (jnp.arange(half, dtype=F32) / half))).reshape(half, 1)

    gq, gk, gv, gg, la, qt, kb, vt, kmean = _in_proj(
        h2, pos, g_mix.reshape(1, -1), wa, wbt, wa2, b_gla_a.reshape(1, -1),
        g_moba_q.reshape(-1, 1), g_moba_k.reshape(-1, 1), inv_col)
    gq = _sc_gather_rows(gq, jnp.arange(s, dtype=jnp.int32)[::-1])[::-1]
    o_gla = _gla(gq, gk, gv, gg, la, g_gla_out.reshape(1, -1))
    o_moba = _moba(qt, kb, vt, kmean.reshape(MOBA_HEADS, s // MOBA_BLOCK, MOBA_HD))
    return _out_mlp(
        h2, o_gla, o_moba, p2, w_out.astype(BF16), g_ffn.reshape(1, -1),
        w_up.astype(BF16), w_down.astype(BF16), g_ple_gate.reshape(1, -1),
        w_ple_gate.astype(BF16), w_ple_proj.astype(BF16), g_ple_emb.reshape(1, -1))


def kernel(x, p, positions, g_mix, w_in, w_gla_a2, b_gla_a, g_gla_out, g_moba_q, g_moba_k,
           w_out, g_ffn, w_up, w_down, g_ple_gate, w_ple_gate, w_ple_proj, g_ple_emb):
    depth = p.shape[0]
    batch = x.shape[0]
    outs = []
    for b in range(batch):
        h = x[b]
        pos = positions[b:b + 1]
        for i in range(depth):
            h = _layer(h, p[i, b], pos, g_mix[i], w_in[i], w_gla_a2[i], b_gla_a[i],
                       g_gla_out[i], g_moba_q[i], g_moba_k[i], w_out[i], g_ffn[i], w_up[i],
                       w_down[i], g_ple_gate[i], w_ple_gate[i], w_ple_proj[i], g_ple_emb[i])
        outs.append(h)
    return jnp.stack(outs, axis=0)
```

```python
import functools
import math

import numpy as np
import jax
import jax.numpy as jnp
from jax import lax
from jax.experimental import pallas as pl
from jax.experimental.pallas import tpu as pltpu

F32 = jnp.float32
BF16 = jnp.bfloat16
HIGHEST = lax.Precision.HIGHEST

D_MODEL = 1024
PLE_DIM = 256
GLA_HEADS = 4
GLA_DK = 64
GLA_DV = 128
GLA_LOWRANK = 16
GLA_TAU = 16.0
GLA_CHUNK = 64
MOBA_HEADS = 4
MOBA_HD = 128
MOBA_BLOCK = 256
MOBA_TOPK = 3
ROPE_THETA = 10000.0
D_FF = 4 * D_MODEL
EPS = 1e-6

GLA_QK_W = GLA_HEADS * GLA_DK
GLA_V_W = GLA_HEADS * GLA_DV
MOBA_W = MOBA_HEADS * MOBA_HD
GLA_IN_W = 2 * GLA_QK_W + 2 * GLA_V_W + GLA_LOWRANK
LANES = 128
BF16_SUBLANES = 16
GA_PAD = LANES
WA_W = 2 * GLA_QK_W + 2 * GLA_V_W + GA_PAD

VMEM_LIMIT_BYTES = 56 * 1024 * 1024

IN_TM = 1024
GLA_CHUNKS_PER_STEP = 4
GLA_TM = 2048
OUT_TM = 512
OUT_SUBTILES = 2
FF_CHUNK = 1024
MOBA_BLOCKS_PER_STEP = 2
MOBA_STEPS_PER_ITER = 2
MOBA_STEPS_PER_LONG_ITER = 4
MOBA_MATMUL_LAG = 3
MOBA_SCORE_SLOTS = 8

MOBA_QSCALE = (MOBA_HD ** -0.5) * math.log2(math.e)
NEG_INF = float("-inf")


def _rms(x, axis):
    return x * lax.rsqrt(jnp.mean(x * x, axis=axis, keepdims=True) + EPS)


def _log_sigmoid(a):
    return jnp.minimum(a, 0.0) - jnp.log1p(jnp.exp(-jnp.abs(a)))


def _dot_nt(a, b):
    return lax.dot_general(a, b, (((1,), (1,)), ((), ())), preferred_element_type=F32)


def _dot_tn(a, b):
    return lax.dot_general(a, b, (((0,), (0,)), ((), ())), preferred_element_type=F32)


def _in_proj_kernel(x_ref, pos_ref, gmix_ref, wa_ref, wbt_ref, wa2_ref, ba_ref,
                    gq_ref, gk_ref, inv_ref,
                    q_out, k_out, v_out, gg_out, la_out,
                    qt_out, kb_out, vt_out, kmean_out):
    tm = x_ref.shape[0]
    sub = MOBA_BLOCK
    nsub = tm // sub
    half = MOBA_HD // 2
    wa2 = wa2_ref[...]
    wa2_hi = wa2.astype(BF16)
    wa2_lo = (wa2 - wa2_hi.astype(F32)).astype(BF16)

    us = [(_rms(x_ref[t * sub:(t + 1) * sub, :], -1) * gmix_ref[...]).astype(BF16)
          for t in range(nsub)]
    zts, zas = [], []
    for t in range(nsub):
        zts.append(_dot_nt(wbt_ref[...], us[t]))
        zas.append(jnp.dot(us[t], wa_ref[...], preferred_element_type=F32))

    for t in range(nsub):
        rows = slice(t * sub, (t + 1) * sub)

        zt = zts[t]
        ang = inv_ref[...] * pos_ref[:, rows].astype(F32)
        cos = jnp.cos(ang)
        sin = jnp.sin(ang)

        def norm_rope(a, g_col):
            a = _rms(a, 0) * g_col
            a1, a2 = a[:half], a[half:]
            return jnp.concatenate([a1 * cos - a2 * sin, a2 * cos + a1 * sin], axis=0)

        for h in range(MOBA_HEADS):
            r0 = h * MOBA_HD
            qr = norm_rope(zt[r0:r0 + MOBA_HD], gq_ref[...]) * MOBA_QSCALE
            kr = norm_rope(zt[MOBA_W + r0:MOBA_W + r0 + MOBA_HD], gk_ref[...])
            kblk = kr.T
            qt_out[h, t] = qr.astype(BF16)
            vt_out[h, t] = zt[2 * MOBA_W + r0:2 * MOBA_W + r0 + MOBA_HD].astype(BF16)
            kb_out[h, t] = kblk.astype(BF16)
            kmean_out[h, t] = jnp.mean(kblk, axis=0, keepdims=True)

        za = zas[t]
        q_out[rows, :] = za[:, 0:GLA_QK_W]
        k_out[rows, :] = za[:, GLA_QK_W:2 * GLA_QK_W]
        v_out[rows, :] = za[:, 2 * GLA_QK_W:2 * GLA_QK_W + GLA_V_W].astype(BF16)
        gg_out[rows, :] = za[:, 2 * GLA_QK_W + GLA_V_W:2 * GLA_QK_W + 2 * GLA_V_W].astype(BF16)

    for t in range(nsub):
        ga = zas[t][:, 2 * GLA_QK_W + 2 * GLA_V_W:]
        ga_hi = ga.astype(BF16)
        ga_lo = (ga - ga_hi.astype(F32)).astype(BF16)
        a_pre = (jnp.dot(ga_hi, wa2_hi, preferred_element_type=F32)
                 + jnp.dot(ga_hi, wa2_lo, preferred_element_type=F32)
                 + jnp.dot(ga_lo, wa2_hi, preferred_element_type=F32)) + ba_ref[...]
        la_out[t * sub:(t + 1) * sub, :] = _log_sigmoid(a_pre) * (1.0 / GLA_TAU)


def _in_proj(x2, pos, g_mix, wa, wbt, wa2, ba, gq_col, gk_col, inv_col):
    s = x2.shape[0]
    tm = min(IN_TM, s)
    nb = s // MOBA_BLOCK
    nblk = tm // MOBA_BLOCK
    const = lambda shape: pl.BlockSpec(shape, lambda i: (0,) * len(shape),
                                       pipeline_mode=pl.Buffered(1))
    row = lambda w: pl.BlockSpec((tm, w), lambda i: (i, 0))
    hblk = lambda a, b: pl.BlockSpec((MOBA_HEADS, nblk, a, b), lambda i: (0, i, 0, 0))
    return pl.pallas_call(
        _in_proj_kernel,
        grid=(s // tm,),
        in_specs=[
            row(D_MODEL),
            pl.BlockSpec((1, tm), lambda i: (0, i)),
            const((1, D_MODEL)),
            const((D_MODEL, WA_W)),
            const((3 * MOBA_W, D_MODEL)),
            const((GA_PAD, GLA_QK_W)),
            const((1, GLA_QK_W)),
            const((MOBA_HD, 1)),
            const((MOBA_HD, 1)),
            const((MOBA_HD // 2, 1)),
        ],
        out_specs=[
            row(GLA_QK_W), row(GLA_QK_W), row(GLA_V_W), row(GLA_V_W), row(GLA_QK_W),
            hblk(MOBA_HD, MOBA_BLOCK), hblk(MOBA_BLOCK, MOBA_HD), hblk(MOBA_HD, MOBA_BLOCK),
            hblk(1, MOBA_HD),
        ],
        out_shape=[
            jax.ShapeDtypeStruct((s, GLA_QK_W), F32),
            jax.ShapeDtypeStruct((s, GLA_QK_W), F32),
            jax.ShapeDtypeStruct((s, GLA_V_W), BF16),
            jax.ShapeDtypeStruct((s, GLA_V_W), BF16),
            jax.ShapeDtypeStruct((s, GLA_QK_W), F32),
            jax.ShapeDtypeStruct((MOBA_HEADS, nb, MOBA_HD, MOBA_BLOCK), BF16),
            jax.ShapeDtypeStruct((MOBA_HEADS, nb, MOBA_BLOCK, MOBA_HD), BF16),
            jax.ShapeDtypeStruct((MOBA_HEADS, nb, MOBA_HD, MOBA_BLOCK), BF16),
            jax.ShapeDtypeStruct((MOBA_HEADS, nb, 1, MOBA_HD), F32),
        ],
        compiler_params=pltpu.CompilerParams(
            dimension_semantics=("arbitrary",), vmem_limit_bytes=VMEM_LIMIT_BYTES),
        name="in_proj",
    )(x2, pos, g_mix, wa, wbt, wa2, ba, gq_col, gk_col, inv_col)


def _gla_kernel(q_ref, k_ref, v_ref, gg_ref, la_ref, gout_ref, o_ref, st_ref):
    c = GLA_CHUNK
    hc = GLA_HEADS * c

    @pl.when(pl.program_id(0) == 0)
    def _():
        st_ref[...] = jnp.zeros_like(st_ref)

    r = lax.broadcasted_iota(jnp.int32, (hc, GLA_QK_W), 0)
    l = lax.broadcasted_iota(jnp.int32, (hc, GLA_QK_W), 1)
    same_head = (r // c) == (l // GLA_DK)
    causal = same_head & ((l % c) <= (r % c))
    ng = GLA_CHUNKS_PER_STEP
    gr = ng * c
    tr = lax.broadcasted_iota(jnp.int32, (gr, gr), 0)
    tc = lax.broadcasted_iota(jnp.int32, (gr, gr), 1)
    tri = ((tr // c == tc // c) & (tc <= tr)).astype(BF16)
    scale = GLA_DK ** -0.5

    def stack_heads(a, width):
        return jnp.concatenate([a[:, h * width:(h + 1) * width] for h in range(GLA_HEADS)], axis=0)

    def block_diag(a):
        return jnp.where(same_head, jnp.concatenate([a] * GLA_HEADS, axis=0), 0.0).astype(BF16)

    chunks = [slice(i * c, (i + 1) * c) for i in range(ng)]

    def cumsum_stage(g):
        la = la_ref[g * gr:(g + 1) * gr, :]
        hi = la.astype(BF16)
        r1 = la - hi.astype(F32)
        mid = r1.astype(BF16)
        lo = (r1 - mid.astype(F32)).astype(BF16)
        return (jnp.dot(tri, hi, preferred_element_type=F32)
                + jnp.dot(tri, mid, preferred_element_type=F32)
                + jnp.dot(tri, lo, preferred_element_type=F32))

    def score_stage(g, b):
        q = q_ref[g * gr:(g + 1) * gr, :]
        k = k_ref[g * gr:(g + 1) * gr, :]
        v = v_ref[g * gr:(g + 1) * gr, :]
        b_last = [b[(i + 1) * c - 1:(i + 1) * c, :] for i in range(ng)]
        qb = [block_diag(q[ch] * jnp.exp(b[ch]) * scale) for ch in chunks]
        kb = [block_diag(k[ch] * jnp.exp(-b[ch])) for ch in chunks]
        keb = [block_diag(k[ch] * jnp.exp(b_last[i] - b[ch])) for i, ch in enumerate(chunks)]
        vst = [stack_heads(v[ch], GLA_DV) for ch in chunks]
        att = [_dot_nt(qb[i], kb[i]) for i in range(ng)]
        dst = [_dot_tn(vst[i], keb[i]) for i in range(ng)]
        decay = [jnp.exp(bl) for bl in b_last]
        return qb, vst, att, dst, decay

    def output_stage(qb, vst, att, dst, decay):
        att = [jnp.where(causal, a, 0.0).astype(BF16) for a in att]
        intra = [lax.dot_general(vst[i], att[i], (((0,), (1,)), ((), ())),
                                 preferred_element_type=F32) for i in range(ng)]
        st = st_ref[...]
        inter = []
        for i in range(ng):
            inter.append(_dot_nt(st.astype(BF16), qb[i]))
            st = st * decay[i] + dst[i]
        st_ref[...] = st
        return [intra[i] + inter[i] for i in range(ng)]

    def store_stage(g, outs):
        gg = gg_ref[g * gr:(g + 1) * gr, :]
        for i, ch in enumerate(chunks):
            o = (_rms(outs[i], 0) * gout_ref[...]).T
            gate = stack_heads(gg[ch], GLA_DV).astype(F32)
            o = o * (gate * jax.nn.sigmoid(gate))
            for h in range(GLA_HEADS):
                o_ref[g * gr + i * c:g * gr + (i + 1) * c, h * GLA_DV:(h + 1) * GLA_DV] = (
                    o[h * c:(h + 1) * c].astype(o_ref.dtype))

    ngroups = q_ref.shape[0] // gr
    cums, scored, outs = {}, {}, {}
    for t in range(ngroups + 3):
        if t < ngroups:
            cums[t] = cumsum_stage(t)
        if 0 <= t - 1 < ngroups:
            scored[t - 1] = score_stage(t - 1, cums.pop(t - 1))
        if 0 <= t - 2 < ngroups:
            outs[t - 2] = output_stage(*scored.pop(t - 2))
        if 0 <= t - 3 < ngroups:
            store_stage(t - 3, outs.pop(t - 3))


def _gla(q, k, v, gg, la, g_out):
    s = q.shape[0]
    tm = min(GLA_TM, s)
    row = lambda w: pl.BlockSpec((tm, w), lambda i: (i, 0))
    return pl.pallas_call(
        _gla_kernel,
        grid=(s // tm,),
        in_specs=[row(GLA_QK_W), row(GLA_QK_W), row(GLA_V_W), row(GLA_V_W), row(GLA_QK_W),
                  pl.BlockSpec((GLA_DV, 1), lambda i: (0, 0))],
        out_specs=row(GLA_V_W),
        out_shape=jax.ShapeDtypeStruct((s, GLA_V_W), BF16),
        scratch_shapes=[pltpu.VMEM((GLA_DV, GLA_QK_W), F32)],
        compiler_params=pltpu.CompilerParams(dimension_semantics=("arbitrary",)),
        name="gla",
    )(q, k, v, gg, la, g_out)


def _moba_kernel(qt_ref, k_ref, vt_ref, kmean_ref, o_ref, bias_ref, acc_ref, s_ref):
    qi = pl.program_id(0)
    heads, nb, hd = kmean_ref.shape
    tq = qt_ref.shape[2]
    kblk = k_ref.shape[2]
    grp = MOBA_BLOCKS_PER_STEP
    lag = MOBA_MATMUL_LAG
    nslots = s_ref.shape[0]

    def step_list(ngrp):
        steps = [(h, g) for g in range(ngrp) for h in range(heads)]
        assert len(steps) % nslots == 0 and nslots >= 2 * lag
        return steps

    def score_matmul(ngrp, base, n):
        steps = step_list(ngrp)
        span = grp * ngrp
        h, g = steps[n % len(steps)]
        blk0 = jnp.minimum(base + (n // len(steps)) * span, nb - span) + g * grp
        kg = k_ref[h, pl.ds(blk0, grp)].reshape(grp * kblk, -1)
        s_ref[n % nslots] = jnp.dot(kg, qt_ref[h], preferred_element_type=F32)

    pad = acc_ref.shape[1] - hd

    def values_t(blocks):
        keys = len(blocks) * kblk
        first_row = lax.broadcasted_iota(jnp.int32, (pad, keys), 0) == 0
        ones_rows = jnp.where(first_row, 1.0, 0.0).astype(BF16)
        return jnp.concatenate([jnp.concatenate(blocks, axis=1), ones_rows], axis=0)

    gates = []
    for h in range(heads):
        km = kmean_ref[h]
        hi = km.astype(BF16)
        r1 = km - hi.astype(F32)
        mid = r1.astype(BF16)
        lo = (r1 - mid.astype(F32)).astype(BF16)
        qt = qt_ref[h]
        gates.append(jnp.dot(hi, qt, preferred_element_type=F32)
                     + jnp.dot(mid, qt, preferred_element_type=F32)
                     + jnp.dot(lo, qt, preferred_element_type=F32))
    own = [jnp.dot(k_ref[h, qi], qt_ref[h], preferred_element_type=F32) for h in range(heads)]
    for n in range(lag):
        score_matmul(MOBA_STEPS_PER_ITER, 0, n)

    blk = lax.broadcasted_iota(jnp.int32, (nb, tq), 0)
    for h in range(heads):
        gate = jnp.where(blk < qi, gates[h], NEG_INF)
        sel = jnp.zeros((nb, tq), jnp.bool_)
        for _ in range(min(MOBA_TOPK, nb)):
            m = jnp.max(gate, axis=0, keepdims=True)
            first = jnp.min(jnp.where(gate == m, blk, nb), axis=0, keepdims=True)
            pick = (blk == first) & (m > NEG_INF)
            sel = sel | pick
            gate = jnp.where(pick, NEG_INF, gate)
        bias_ref[h] = jnp.where(sel, 0.0, NEG_INF)

    kk = lax.broadcasted_iota(jnp.int32, (kblk, tq), 0)
    qq = lax.broadcasted_iota(jnp.int32, (kblk, tq), 1)
    causal = kk <= qq
    stats, probs = [], []
    for h in range(heads):
        s = jnp.where(causal, own[h], NEG_INF)
        m = jnp.max(s, axis=0, keepdims=True)
        stats.append(m)
        probs.append(jnp.exp2(s - m).astype(BF16))
    for h in range(heads):
        acc_ref[h] = jnp.dot(values_t([vt_ref[h, qi]]), probs[h], preferred_element_type=F32)

    def past_blocks(ngrp, first_block):
        span = grp * ngrp

        def body(it, carry):
            base = pl.multiple_of(first_block + it * span, grp * MOBA_STEPS_PER_ITER)
            stats = list(carry)
            for n, (h, g) in enumerate(step_list(ngrp)):
                score_matmul(ngrp, base, n + lag)
                j0 = base + g * grp
                m = stats[h]
                s = [s_ref[n % nslots, u * kblk:(u + 1) * kblk] for u in range(grp)]
                bias = [bias_ref[h, pl.ds(j0 + u, 1), :] for u in range(grp)]
                m_new = m
                for u in range(grp):
                    m_new = jnp.maximum(m_new, jnp.max(s[u], axis=0, keepdims=True) + bias[u])
                alpha = jnp.exp2(m - m_new)
                pb = jnp.concatenate(
                    [jnp.exp2(s[u] + (bias[u] - m_new)).astype(BF16) for u in range(grp)], axis=0)
                vt = values_t([vt_ref[h, j0 + u] for u in range(grp)])
                acc_ref[h] = alpha * acc_ref[h] + jnp.dot(vt, pb, preferred_element_type=F32)
                stats[h] = m_new
            return tuple(stats)

        return body

    long_ngrp = MOBA_STEPS_PER_LONG_ITER
    short_span, long_span = grp * MOBA_STEPS_PER_ITER, grp * long_ngrp
    assert step_list(long_ngrp)[:lag] == step_list(MOBA_STEPS_PER_ITER)[:lag]
    n_long = qi // long_span
    n_short = (qi - n_long * long_span + short_span - 1) // short_span
    stats = lax.fori_loop(0, n_long, past_blocks(long_ngrp, 0), tuple(stats))
    lax.fori_loop(0, n_short, past_blocks(MOBA_STEPS_PER_ITER, n_long * long_span), stats)
    for h in range(heads):
        acc = acc_ref[h]
        o_ref[:, h * hd:(h + 1) * hd] = (acc[:hd] / acc[hd:hd + 1]).T.astype(o_ref.dtype)


def _moba(qt, kb, vt, kmean):
    heads, nb, hd, blk = qt.shape
    s = nb * blk
    assert nb % (MOBA_BLOCKS_PER_STEP * MOBA_STEPS_PER_ITER) == 0
    assert nb % (MOBA_BLOCKS_PER_STEP * MOBA_STEPS_PER_LONG_ITER) == 0
    const = lambda shape: pl.BlockSpec(shape, lambda i: (0,) * len(shape),
                                       pipeline_mode=pl.Buffered(1))
    return pl.pallas_call(
        _moba_kernel,
        grid=(nb,),
        in_specs=[
            pl.BlockSpec((heads, None, hd, blk), lambda i: (0, i, 0, 0)),
            const((heads, nb, blk, hd)),
            const((heads, nb, hd, blk)),
            const((heads, nb, hd)),
        ],
        out_specs=pl.BlockSpec((blk, heads * hd), lambda i: (i, 0)),
        out_shape=jax.ShapeDtypeStruct((s, heads * hd), BF16),
        scratch_shapes=[
            pltpu.VMEM((heads, nb, blk), F32),
            pltpu.VMEM((heads, hd + BF16_SUBLANES, blk), F32),
            pltpu.VMEM((MOBA_SCORE_SLOTS, MOBA_BLOCKS_PER_STEP * blk, blk), F32),
        ],
        compiler_params=pltpu.CompilerParams(
            dimension_semantics=("arbitrary",), vmem_limit_bytes=VMEM_LIMIT_BYTES),
        name="moba",
    )(qt, kb, vt, kmean)


def _out_mlp_kernel(x_ref, og_ref, om_ref, p_ref, wo_ref, gffn_ref, wup_ref, wdn_ref,
                    gpg_ref, wpg_ref, wpp_ref, gpe_ref, o_ref):
    nsub = OUT_SUBTILES
    sub = x_ref.shape[0] // nsub
    rows = [slice(t * sub, (t + 1) * sub) for t in range(nsub)]

    e = [jnp.dot(p_ref[r, :].astype(BF16), wpp_ref[...], preferred_element_type=F32) for r in rows]
    h = [x_ref[r, :]
         + jnp.dot(og_ref[r, :], wo_ref[:GLA_V_W, :], preferred_element_type=F32)
         + jnp.dot(om_ref[r, :], wo_ref[GLA_V_W:, :], preferred_element_type=F32) for r in rows]

    u = [(_rms(ht, -1) * gffn_ref[...]).astype(BF16) for ht in h]
    mlp = [None] * nsub
    for c0 in range(0, D_FF, FF_CHUNK):
        f = [jnp.dot(ut, wup_ref[:, c0:c0 + FF_CHUNK], preferred_element_type=F32) for ut in u]
        f = [jnp.square(jnp.maximum(ft, 0.0)).astype(BF16) for ft in f]
        d = [jnp.dot(ft, wdn_ref[c0:c0 + FF_CHUNK, :], preferred_element_type=F32) for ft in f]
        mlp = [dt if mt is None else mt + dt for mt, dt in zip(mlp, d)]
    h = [ht + mt for ht, mt in zip(h, mlp)]

    u = [(_rms(ht, -1) * gpg_ref[...]).astype(BF16) for ht in h]
    gate = [jax.nn.sigmoid(jnp.dot(ut, wpg_ref[...], preferred_element_type=F32)) for ut in u]
    for t, r in enumerate(rows):
        o_ref[r, :] = h[t] + gate[t] * (_rms(e[t], -1) * gpe_ref[...])


def _out_mlp(x2, og, om, p2, wo, g_ffn, wup, wdn, g_pg, wpg, wpp, g_pe):
    s = x2.shape[0]
    tm = min(OUT_TM, s)
    const = lambda shape: pl.BlockSpec(shape, lambda i: (0,) * len(shape),
                                       pipeline_mode=pl.Buffered(1))
    row = lambda w: pl.BlockSpec((tm, w), lambda i: (i, 0))
    return pl.pallas_call(
        _out_mlp_kernel,
        grid=(s // tm,),
        in_specs=[
            row(D_MODEL), row(GLA_V_W), row(MOBA_W), row(PLE_DIM),
            const((GLA_V_W + MOBA_W, D_MODEL)),
            const((1, D_MODEL)),
            const((D_MODEL, D_FF)),
            const((D_FF, D_MODEL)),
            const((1, D_MODEL)),
            const((D_MODEL, D_MODEL)),
            const((PLE_DIM, D_MODEL)),
            const((1, D_MODEL)),
        ],
        out_specs=row(D_MODEL),
        out_shape=jax.ShapeDtypeStruct((s, D_MODEL), F32),
        compiler_params=pltpu.CompilerParams(
            dimension_semantics=("arbitrary",), vmem_limit_bytes=VMEM_LIMIT_BYTES),
        name="out_mlp",
    )(x2, og, om, p2, wo, g_ffn, wup, wdn, g_pg, wpg, wpp, g_pe)


def _layer(h2, p2, pos, g_mix, w_in, w_gla_a2, b_gla_a, g_gla_out, g_moba_q, g_moba_k,
           w_out, g_ffn, w_up, w_down, g_ple_gate, w_ple_gate, w_ple_proj, g_ple_emb):
    s = h2.shape[0]
    assert s % MOBA_BLOCK == 0 and s % GLA_CHUNK == 0
    assert s % min(IN_TM, s) == 0 and s % min(GLA_TM, s) == 0 and s % min(OUT_TM, s) == 0
    assert min(GLA_TM, s) % (GLA_CHUNKS_PER_STEP * GLA_CHUNK) == 0

    wa = jnp.concatenate(
        [w_in[:, :GLA_IN_W], jnp.zeros((D_MODEL, GA_PAD - GLA_LOWRANK), w_in.dtype)],
        axis=1).astype(BF16)
    wbt = w_in[:, GLA_IN_W:].T.astype(BF16)
    wa2 = jnp.concatenate(
        [w_gla_a2, jnp.zeros((GA_PAD - GLA_LOWRANK, GLA_QK_W), w_gla_a2.dtype)], axis=0)
    half = MOBA_HD // 2
    inv_col = (1.0 / (ROPE_THETA ** (jnp.arange(half, dtype=F32) / half))).reshape(half, 1)

    gq, gk, gv, gg, la, qt, kb, vt, kmean = _in_proj(
        h2, pos, g_mix.reshape(1, -1), wa, wbt, wa2, b_gla_a.reshape(1, -1),
        g_moba_q.reshape(-1, 1), g_moba_k.reshape(-1, 1), inv_col)
    o_gla = _gla(gq, gk, gv, gg, la, g_gla_out.reshape(-1, 1))
    o_moba = _moba(qt, kb, vt, kmean.reshape(MOBA_HEADS, s // MOBA_BLOCK, MOBA_HD))
    return _out_mlp(
        h2, o_gla, o_moba, p2, w_out.astype(BF16), g_ffn.reshape(1, -1),
        w_up.astype(BF16), w_down.astype(BF16), g_ple_gate.reshape(1, -1),
        w_ple_gate.astype(BF16), w_ple_proj.astype(BF16), g_ple_emb.reshape(1, -1))


def kernel(x, p, positions, g_mix, w_in, w_gla_a2, b_gla_a, g_gla_out, g_moba_q, g_moba_k,
           w_out, g_ffn, w_up, w_down, g_ple_gate, w_ple_gate, w_ple_proj, g_ple_emb):
    depth = p.shape[0]
    batch = x.shape[0]
    outs = []
    for b in range(batch):
        h = x[b]
        pos = positions[b:b + 1]
        for i in range(depth):
            h = _layer(h, p[i, b], pos, g_mix[i], w_in[i], w_gla_a2[i], b_gla_a[i],
                       g_gla_out[i], g_moba_q[i], g_moba_k[i], w_out[i], g_ffn[i], w_up[i],
                       w_down[i], g_ple_gate[i], w_ple_gate[i], w_ple_proj[i], g_ple_emb[i])
        outs.append(h)
    return jnp.stack(outs, axis=0)
```

```python
import functools
import math

import numpy as np
import jax
import jax.numpy as jnp
from jax import lax
from jax.experimental import pallas as pl
from jax.experimental.pallas import tpu as pltpu

F32 = jnp.float32
BF16 = jnp.bfloat16
HIGHEST = lax.Precision.HIGHEST

D_MODEL = 1024
PLE_DIM = 256
GLA_HEADS = 4
GLA_DK = 64
GLA_DV = 128
GLA_LOWRANK = 16
GLA_TAU = 16.0
GLA_CHUNK = 64
MOBA_HEADS = 4
MOBA_HD = 128
MOBA_BLOCK = 256
MOBA_TOPK = 3
ROPE_THETA = 10000.0
D_FF = 4 * D_MODEL
EPS = 1e-6

GLA_QK_W = GLA_HEADS * GLA_DK
GLA_V_W = GLA_HEADS * GLA_DV
MOBA_W = MOBA_HEADS * MOBA_HD
GLA_IN_W = 2 * GLA_QK_W + 2 * GLA_V_W + GLA_LOWRANK
LANES = 128
BF16_SUBLANES = 16
GA_PAD = LANES
WA_W = 2 * GLA_QK_W + 2 * GLA_V_W + GA_PAD

VMEM_LIMIT_BYTES = 56 * 1024 * 1024

IN_TM = 1024
GLA_CHUNKS_PER_STEP = 4
GLA_TM = 1024
OUT_TM = 512
OUT_SUBTILES = 2
FF_CHUNK = 1024
MOBA_BLOCKS_PER_STEP = 2
MOBA_STEPS_PER_ITER = 2
MOBA_STEPS_PER_LONG_ITER = 4
MOBA_MATMUL_LAG = 3
MOBA_SCORE_SLOTS = 8

MOBA_QSCALE = (MOBA_HD ** -0.5) * math.log2(math.e)
NEG_INF = float("-inf")


def _rms(x, axis):
    return x * lax.rsqrt(jnp.mean(x * x, axis=axis, keepdims=True) + EPS)


def _log_sigmoid(a):
    return jnp.minimum(a, 0.0) - jnp.log1p(jnp.exp(-jnp.abs(a)))


def _dot_nt(a, b):
    return lax.dot_general(a, b, (((1,), (1,)), ((), ())), preferred_element_type=F32)


def _dot_tn(a, b):
    return lax.dot_general(a, b, (((0,), (0,)), ((), ())), preferred_element_type=F32)


def _in_proj_kernel(x_ref, pos_ref, gmix_ref, wa_ref, wbt_ref, wa2_ref, ba_ref,
                    gq_ref, gk_ref, inv_ref,
                    q_out, k_out, v_out, gg_out, la_out,
                    qt_out, kb_out, vt_out, kmean_out):
    tm = x_ref.shape[0]
    sub = MOBA_BLOCK
    nsub = tm // sub
    half = MOBA_HD // 2
    wa2 = wa2_ref[...]
    wa2_hi = wa2.astype(BF16)
    wa2_lo = (wa2 - wa2_hi.astype(F32)).astype(BF16)

    us = [(_rms(x_ref[t * sub:(t + 1) * sub, :], -1) * gmix_ref[...]).astype(BF16)
          for t in range(nsub)]
    zts, zas = [], []
    for t in range(nsub):
        zts.append(_dot_nt(wbt_ref[...], us[t]))
        zas.append(jnp.dot(us[t], wa_ref[...], preferred_element_type=F32))

    for t in range(nsub):
        rows = slice(t * sub, (t + 1) * sub)

        zt = zts[t]
        ang = inv_ref[...] * pos_ref[:, rows].astype(F32)
        cos = jnp.cos(ang)
        sin = jnp.sin(ang)

        def norm_rope(a, g_col):
            a = _rms(a, 0) * g_col
            a1, a2 = a[:half], a[half:]
            return jnp.concatenate([a1 * cos - a2 * sin, a2 * cos + a1 * sin], axis=0)

        for h in range(MOBA_HEADS):
            r0 = h * MOBA_HD
            qr = norm_rope(zt[r0:r0 + MOBA_HD], gq_ref[...]) * MOBA_QSCALE
            kr = norm_rope(zt[MOBA_W + r0:MOBA_W + r0 + MOBA_HD], gk_ref[...])
            kblk = kr.T
            qt_out[h, t] = qr.astype(BF16)
            vt_out[h, t] = zt[2 * MOBA_W + r0:2 * MOBA_W + r0 + MOBA_HD].astype(BF16)
            kb_out[h, t] = kblk.astype(BF16)
            kmean_out[h, t] = jnp.mean(kblk, axis=0, keepdims=True)

        za = zas[t]
        q_out[rows, :] = za[:, 0:GLA_QK_W]
        k_out[rows, :] = za[:, GLA_QK_W:2 * GLA_QK_W]
        v_out[rows, :] = za[:, 2 * GLA_QK_W:2 * GLA_QK_W + GLA_V_W].astype(BF16)
        gg_out[rows, :] = za[:, 2 * GLA_QK_W + GLA_V_W:2 * GLA_QK_W + 2 * GLA_V_W].astype(BF16)

    for t in range(nsub):
        ga = zas[t][:, 2 * GLA_QK_W + 2 * GLA_V_W:]
        ga_hi = ga.astype(BF16)
        ga_lo = (ga - ga_hi.astype(F32)).astype(BF16)
        a_pre = (jnp.dot(ga_hi, wa2_hi, preferred_element_type=F32)
                 + jnp.dot(ga_hi, wa2_lo, preferred_element_type=F32)
                 + jnp.dot(ga_lo, wa2_hi, preferred_element_type=F32)) + ba_ref[...]
        la_out[t * sub:(t + 1) * sub, :] = _log_sigmoid(a_pre) * (1.0 / GLA_TAU)


def _in_proj(x2, pos, g_mix, wa, wbt, wa2, ba, gq_col, gk_col, inv_col):
    s = x2.shape[0]
    tm = min(IN_TM, s)
    nb = s // MOBA_BLOCK
    nblk = tm // MOBA_BLOCK
    const = lambda shape: pl.BlockSpec(shape, lambda i: (0,) * len(shape),
                                       pipeline_mode=pl.Buffered(1))
    row = lambda w: pl.BlockSpec((tm, w), lambda i: (i, 0))
    hblk = lambda a, b: pl.BlockSpec((MOBA_HEADS, nblk, a, b), lambda i: (0, i, 0, 0))
    return pl.pallas_call(
        _in_proj_kernel,
        grid=(s // tm,),
        in_specs=[
            row(D_MODEL),
            pl.BlockSpec((1, tm), lambda i: (0, i)),
            const((1, D_MODEL)),
            const((D_MODEL, WA_W)),
            const((3 * MOBA_W, D_MODEL)),
            const((GA_PAD, GLA_QK_W)),
            const((1, GLA_QK_W)),
            const((MOBA_HD, 1)),
            const((MOBA_HD, 1)),
            const((MOBA_HD // 2, 1)),
        ],
        out_specs=[
            row(GLA_QK_W), row(GLA_QK_W), row(GLA_V_W), row(GLA_V_W), row(GLA_QK_W),
            hblk(MOBA_HD, MOBA_BLOCK), hblk(MOBA_BLOCK, MOBA_HD), hblk(MOBA_HD, MOBA_BLOCK),
            hblk(1, MOBA_HD),
        ],
        out_shape=[
            jax.ShapeDtypeStruct((s, GLA_QK_W), F32),
            jax.ShapeDtypeStruct((s, GLA_QK_W), F32),
            jax.ShapeDtypeStruct((s, GLA_V_W), BF16),
            jax.ShapeDtypeStruct((s, GLA_V_W), BF16),
            jax.ShapeDtypeStruct((s, GLA_QK_W), F32),
            jax.ShapeDtypeStruct((MOBA_HEADS, nb, MOBA_HD, MOBA_BLOCK), BF16),
            jax.ShapeDtypeStruct((MOBA_HEADS, nb, MOBA_BLOCK, MOBA_HD), BF16),
            jax.ShapeDtypeStruct((MOBA_HEADS, nb, MOBA_HD, MOBA_BLOCK), BF16),
            jax.ShapeDtypeStruct((MOBA_HEADS, nb, 1, MOBA_HD), F32),
        ],
        compiler_params=pltpu.CompilerParams(
            dimension_semantics=("arbitrary",), vmem_limit_bytes=VMEM_LIMIT_BYTES),
        name="in_proj",
    )(x2, pos, g_mix, wa, wbt, wa2, ba, gq_col, gk_col, inv_col)


def _gla_kernel(q_ref, k_ref, v_ref, gg_ref, la_ref, gout_ref, o_ref, st_ref):
    c = GLA_CHUNK
    hc = GLA_HEADS * c

    @pl.when(pl.program_id(0) == 0)
    def _():
        st_ref[...] = jnp.zeros_like(st_ref)

    r = lax.broadcasted_iota(jnp.int32, (hc, GLA_QK_W), 0)
    l = lax.broadcasted_iota(jnp.int32, (hc, GLA_QK_W), 1)
    same_head = (r // c) == (l // GLA_DK)
    causal = same_head & ((l % c) <= (r % c))
    ng = GLA_CHUNKS_PER_STEP
    gr = ng * c
    tr = lax.broadcasted_iota(jnp.int32, (gr, gr), 0)
    tc = lax.broadcasted_iota(jnp.int32, (gr, gr), 1)
    tri = ((tr // c == tc // c) & (tc <= tr)).astype(BF16)
    scale = GLA_DK ** -0.5

    def stack_heads(a, width):
        return jnp.concatenate([a[:, h * width:(h + 1) * width] for h in range(GLA_HEADS)], axis=0)

    def block_diag(a):
        return jnp.where(same_head, jnp.concatenate([a] * GLA_HEADS, axis=0), 0.0).astype(BF16)

    chunks = [slice(i * c, (i + 1) * c) for i in range(ng)]

    def cumsum_stage(g):
        la = la_ref[g * gr:(g + 1) * gr, :]
        hi = la.astype(BF16)
        r1 = la - hi.astype(F32)
        mid = r1.astype(BF16)
        lo = (r1 - mid.astype(F32)).astype(BF16)
        return (jnp.dot(tri, hi, preferred_element_type=F32)
                + jnp.dot(tri, mid, preferred_element_type=F32)
                + jnp.dot(tri, lo, preferred_element_type=F32))

    def score_stage(g, b):
        q = q_ref[g * gr:(g + 1) * gr, :]
        k = k_ref[g * gr:(g + 1) * gr, :]
        v = v_ref[g * gr:(g + 1) * gr, :]
        b_last = [b[(i + 1) * c - 1:(i + 1) * c, :] for i in range(ng)]
        qb = [block_diag(q[ch] * jnp.exp(b[ch]) * scale) for ch in chunks]
        kb = [block_diag(k[ch] * jnp.exp(-b[ch])) for ch in chunks]
        keb = [block_diag(k[ch] * jnp.exp(b_last[i] - b[ch])) for i, ch in enumerate(chunks)]
        vst = [stack_heads(v[ch], GLA_DV) for ch in chunks]
        att = [_dot_nt(qb[i], kb[i]) for i in range(ng)]
        dst = [_dot_tn(vst[i], keb[i]) for i in range(ng)]
        decay = [jnp.exp(bl) for bl in b_last]
        return qb, vst, att, dst, decay

    def output_stage(qb, vst, att, dst, decay):
        att = [jnp.where(causal, a, 0.0).astype(BF16) for a in att]
        intra = [jnp.dot(att[i], vst[i], preferred_element_type=F32) for i in range(ng)]
        st = st_ref[...]
        inter = []
        for i in range(ng):
            inter.append(_dot_nt(qb[i], st.astype(BF16)))
            st = st * decay[i] + dst[i]
        st_ref[...] = st
        return [intra[i] + inter[i] for i in range(ng)]

    def store_stage(g, outs):
        gg = gg_ref[g * gr:(g + 1) * gr, :]
        for i, ch in enumerate(chunks):
            o = _rms(outs[i], -1) * gout_ref[...]
            gate = stack_heads(gg[ch], GLA_DV).astype(F32)
            o = o * (gate * jax.nn.sigmoid(gate))
            for h in range(GLA_HEADS):
                o_ref[g * gr + i * c:g * gr + (i + 1) * c, h * GLA_DV:(h + 1) * GLA_DV] = (
                    o[h * c:(h + 1) * c].astype(o_ref.dtype))

    ngroups = q_ref.shape[0] // gr
    cums, scored, outs = {}, {}, {}
    for t in range(ngroups + 3):
        if t < ngroups:
            cums[t] = cumsum_stage(t)
        if 0 <= t - 1 < ngroups:
            scored[t - 1] = score_stage(t - 1, cums.pop(t - 1))
        if 0 <= t - 2 < ngroups:
            outs[t - 2] = output_stage(*scored.pop(t - 2))
        if 0 <= t - 3 < ngroups:
            store_stage(t - 3, outs.pop(t - 3))


def _gla(q, k, v, gg, la, g_out):
    s = q.shape[0]
    tm = min(GLA_TM, s)
    row = lambda w: pl.BlockSpec((tm, w), lambda i: (i, 0))
    return pl.pallas_call(
        _gla_kernel,
        grid=(s // tm,),
        in_specs=[row(GLA_QK_W), row(GLA_QK_W), row(GLA_V_W), row(GLA_V_W), row(GLA_QK_W),
                  pl.BlockSpec((1, GLA_DV), lambda i: (0, 0))],
        out_specs=row(GLA_V_W),
        out_shape=jax.ShapeDtypeStruct((s, GLA_V_W), BF16),
        scratch_shapes=[pltpu.VMEM((GLA_DV, GLA_QK_W), F32)],
        compiler_params=pltpu.CompilerParams(dimension_semantics=("arbitrary",)),
        name="gla",
    )(q, k, v, gg, la, g_out)


def _moba_kernel(n_cast, qt_ref, k_ref, vt_ref, kmean_ref, *refs):
    o_ref = refs[n_cast]
    bias_ref, acc_ref, s_ref = refs[2 * n_cast + 1:]
    for w_ref, w_bf16_ref in zip(refs[:n_cast], refs[n_cast + 1:2 * n_cast + 1]):
        w_bf16_ref[...] = w_ref[...].astype(w_bf16_ref.dtype)

    qi = pl.program_id(0)
    heads, nb, hd = kmean_ref.shape
    tq = qt_ref.shape[2]
    kblk = k_ref.shape[2]
    grp = MOBA_BLOCKS_PER_STEP
    lag = MOBA_MATMUL_LAG
    nslots = s_ref.shape[0]

    def step_list(ngrp):
        steps = [(h, g) for g in range(ngrp) for h in range(heads)]
        assert len(steps) % nslots == 0 and nslots >= 2 * lag
        return steps

    def score_matmul(ngrp, base, n):
        steps = step_list(ngrp)
        span = grp * ngrp
        h, g = steps[n % len(steps)]
        blk0 = jnp.minimum(base + (n // len(steps)) * span, nb - span) + g * grp
        kg = k_ref[h, pl.ds(blk0, grp)].reshape(grp * kblk, -1)
        s_ref[n % nslots] = jnp.dot(kg, qt_ref[h], preferred_element_type=F32)

    pad = acc_ref.shape[1] - hd

    def values_t(blocks):
        keys = len(blocks) * kblk
        first_row = lax.broadcasted_iota(jnp.int32, (pad, keys), 0) == 0
        ones_rows = jnp.where(first_row, 1.0, 0.0).astype(BF16)
        return jnp.concatenate([jnp.concatenate(blocks, axis=1), ones_rows], axis=0)

    gates = []
    for h in range(heads):
        km = kmean_ref[h]
        hi = km.astype(BF16)
        r1 = km - hi.astype(F32)
        mid = r1.astype(BF16)
        lo = (r1 - mid.astype(F32)).astype(BF16)
        qt = qt_ref[h]
        gates.append(jnp.dot(hi, qt, preferred_element_type=F32)
                     + jnp.dot(mid, qt, preferred_element_type=F32)
                     + jnp.dot(lo, qt, preferred_element_type=F32))
    own = [jnp.dot(k_ref[h, qi], qt_ref[h], preferred_element_type=F32) for h in range(heads)]
    for n in range(lag):
        score_matmul(MOBA_STEPS_PER_ITER, 0, n)

    blk = lax.broadcasted_iota(jnp.int32, (nb, tq), 0)
    for h in range(heads):
        gate = jnp.where(blk < qi, gates[h], NEG_INF)
        sel = jnp.zeros((nb, tq), jnp.bool_)
        for _ in range(min(MOBA_TOPK, nb)):
            m = jnp.max(gate, axis=0, keepdims=True)
            first = jnp.min(jnp.where(gate == m, blk, nb), axis=0, keepdims=True)
            pick = (blk == first) & (m > NEG_INF)
            sel = sel | pick
            gate = jnp.where(pick, NEG_INF, gate)
        bias_ref[h] = jnp.where(sel, 0.0, NEG_INF)

    kk = lax.broadcasted_iota(jnp.int32, (kblk, tq), 0)
    qq = lax.broadcasted_iota(jnp.int32, (kblk, tq), 1)
    causal = kk <= qq
    stats, probs = [], []
    for h in range(heads):
        s = jnp.where(causal, own[h], NEG_INF)
        m = jnp.max(s, axis=0, keepdims=True)
        stats.append(m)
        probs.append(jnp.exp2(s - m).astype(BF16))
    for h in range(heads):
        acc_ref[h] = jnp.dot(values_t([vt_ref[h, qi]]), probs[h], preferred_element_type=F32)

    def past_blocks(ngrp, first_block):
        span = grp * ngrp

        def body(it, carry):
            base = pl.multiple_of(first_block + it * span, grp * MOBA_STEPS_PER_ITER)
            stats = list(carry)
            for n, (h, g) in enumerate(step_list(ngrp)):
                score_matmul(ngrp, base, n + lag)
                j0 = base + g * grp
                m = stats[h]
                s = [s_ref[n % nslots, u * kblk:(u + 1) * kblk] for u in range(grp)]
                bias = [bias_ref[h, pl.ds(j0 + u, 1), :] for u in range(grp)]
                m_new = m
                for u in range(grp):
                    m_new = jnp.maximum(m_new, jnp.max(s[u], axis=0, keepdims=True) + bias[u])
                alpha = jnp.exp2(m - m_new)
                pb = jnp.concatenate(
                    [jnp.exp2(s[u] + (bias[u] - m_new)).astype(BF16) for u in range(grp)], axis=0)
                vt = values_t([vt_ref[h, j0 + u] for u in range(grp)])
                acc_ref[h] = alpha * acc_ref[h] + jnp.dot(vt, pb, preferred_element_type=F32)
                stats[h] = m_new
            return tuple(stats)

        return body

    long_ngrp = MOBA_STEPS_PER_LONG_ITER
    short_span, long_span = grp * MOBA_STEPS_PER_ITER, grp * long_ngrp
    assert step_list(long_ngrp)[:lag] == step_list(MOBA_STEPS_PER_ITER)[:lag]
    n_long = qi // long_span
    n_short = (qi - n_long * long_span + short_span - 1) // short_span
    stats = lax.fori_loop(0, n_long, past_blocks(long_ngrp, 0), tuple(stats))
    lax.fori_loop(0, n_short, past_blocks(MOBA_STEPS_PER_ITER, n_long * long_span), stats)
    for h in range(heads):
        acc = acc_ref[h]
        o_ref[:, h * hd:(h + 1) * hd] = (acc[:hd] / acc[hd:hd + 1]).T.astype(o_ref.dtype)


def _moba(qt, kb, vt, kmean, weights):
    heads, nb, hd, blk = qt.shape
    s = nb * blk
    assert nb % (MOBA_BLOCKS_PER_STEP * MOBA_STEPS_PER_ITER) == 0
    assert nb % (MOBA_BLOCKS_PER_STEP * MOBA_STEPS_PER_LONG_ITER) == 0
    assert all(w.shape[0] % (nb * BF16_SUBLANES) == 0 for w in weights)
    const = lambda shape: pl.BlockSpec(shape, lambda i: (0,) * len(shape),
                                       pipeline_mode=pl.Buffered(1))
    slab = lambda w: pl.BlockSpec((w.shape[0] // nb, w.shape[1]), lambda i: (i, 0))
    return pl.pallas_call(
        functools.partial(_moba_kernel, len(weights)),
        grid=(nb,),
        in_specs=[
            pl.BlockSpec((heads, None, hd, blk), lambda i: (0, i, 0, 0)),
            const((heads, nb, blk, hd)),
            const((heads, nb, hd, blk)),
            const((heads, nb, hd)),
        ] + [slab(w) for w in weights],
        out_specs=[pl.BlockSpec((blk, heads * hd), lambda i: (i, 0))] + [slab(w) for w in weights],
        out_shape=[jax.ShapeDtypeStruct((s, heads * hd), BF16)]
        + [jax.ShapeDtypeStruct(w.shape, BF16) for w in weights],
        scratch_shapes=[
            pltpu.VMEM((heads, nb, blk), F32),
            pltpu.VMEM((heads, hd + BF16_SUBLANES, blk), F32),
            pltpu.VMEM((MOBA_SCORE_SLOTS, MOBA_BLOCKS_PER_STEP * blk, blk), F32),
        ],
        compiler_params=pltpu.CompilerParams(
            dimension_semantics=("arbitrary",), vmem_limit_bytes=VMEM_LIMIT_BYTES),
        name="moba",
    )(qt, kb, vt, kmean, *weights)


def _out_mlp_kernel(x_ref, og_ref, om_ref, p_ref, wo_ref, gffn_ref, wup_ref, wdn_ref,
                    gpg_ref, wpg_ref, wpp_ref, gpe_ref, o_ref):
    nsub = OUT_SUBTILES
    sub = x_ref.shape[0] // nsub
    rows = [slice(t * sub, (t + 1) * sub) for t in range(nsub)]

    e = [jnp.dot(p_ref[r, :].astype(BF16), wpp_ref[...], preferred_element_type=F32) for r in rows]
    h = [x_ref[r, :]
         + jnp.dot(og_ref[r, :], wo_ref[:GLA_V_W, :], preferred_element_type=F32)
         + jnp.dot(om_ref[r, :], wo_ref[GLA_V_W:, :], preferred_element_type=F32) for r in rows]

    u = [(_rms(ht, -1) * gffn_ref[...]).astype(BF16) for ht in h]
    mlp = [None] * nsub
    for c0 in range(0, D_FF, FF_CHUNK):
        f = [jnp.dot(ut, wup_ref[:, c0:c0 + FF_CHUNK], preferred_element_type=F32) for ut in u]
        f = [jnp.square(jnp.maximum(ft, 0.0)).astype(BF16) for ft in f]
        d = [jnp.dot(ft, wdn_ref[c0:c0 + FF_CHUNK, :], preferred_element_type=F32) for ft in f]
        mlp = [dt if mt is None else mt + dt for mt, dt in zip(mlp, d)]
    h = [ht + mt for ht, mt in zip(h, mlp)]

    u = [(_rms(ht, -1) * gpg_ref[...]).astype(BF16) for ht in h]
    gate = [jax.nn.sigmoid(jnp.dot(ut, wpg_ref[...], preferred_element_type=F32)) for ut in u]
    for t, r in enumerate(rows):
        o_ref[r, :] = h[t] + gate[t] * (_rms(e[t], -1) * gpe_ref[...])


def _out_mlp(x2, og, om, p2, wo, g_ffn, wup, wdn, g_pg, wpg, wpp, g_pe):
    s = x2.shape[0]
    tm = min(OUT_TM, s)
    const = lambda shape: pl.BlockSpec(shape, lambda i: (0,) * len(shape),
                                       pipeline_mode=pl.Buffered(1))
    row = lambda w: pl.BlockSpec((tm, w), lambda i: (i, 0))
    return pl.pallas_call(
        _out_mlp_kernel,
        grid=(s // tm,),
        in_specs=[
            row(D_MODEL), row(GLA_V_W), row(MOBA_W), row(PLE_DIM),
            const((GLA_V_W + MOBA_W, D_MODEL)),
            const((1, D_MODEL)),
            const((D_MODEL, D_FF)),
            const((D_FF, D_MODEL)),
            const((1, D_MODEL)),
            const((D_MODEL, D_MODEL)),
            const((PLE_DIM, D_MODEL)),
            const((1, D_MODEL)),
        ],
        out_specs=row(D_MODEL),
        out_shape=jax.ShapeDtypeStruct((s, D_MODEL), F32),
        compiler_params=pltpu.CompilerParams(
            dimension_semantics=("arbitrary",), vmem_limit_bytes=VMEM_LIMIT_BYTES),
        name="out_mlp",
    )(x2, og, om, p2, wo, g_ffn, wup, wdn, g_pg, wpg, wpp, g_pe)


def _layer(h2, p2, pos, g_mix, w_in, w_gla_a2, b_gla_a, g_gla_out, g_moba_q, g_moba_k,
           w_out, g_ffn, w_up, w_down, g_ple_gate, w_ple_gate, w_ple_proj, g_ple_emb):
    s = h2.shape[0]
    assert s % MOBA_BLOCK == 0 and s % GLA_CHUNK == 0
    assert s % min(IN_TM, s) == 0 and s % min(GLA_TM, s) == 0 and s % min(OUT_TM, s) == 0
    assert min(GLA_TM, s) % (GLA_CHUNKS_PER_STEP * GLA_CHUNK) == 0

    wa = jnp.concatenate(
        [w_in[:, :GLA_IN_W], jnp.zeros((D_MODEL, GA_PAD - GLA_LOWRANK), w_in.dtype)],
        axis=1).astype(BF16)
    wbt = w_in[:, GLA_IN_W:].T.astype(BF16)
    wa2 = jnp.concatenate(
        [w_gla_a2, jnp.zeros((GA_PAD - GLA_LOWRANK, GLA_QK_W), w_gla_a2.dtype)], axis=0)
    half = MOBA_HD // 2
    inv_col = (1.0 / (ROPE_THETA ** (jnp.arange(half, dtype=F32) / half))).reshape(half, 1)

    gq, gk, gv, gg, la, qt, kb, vt, kmean = _in_proj(
        h2, pos, g_mix.reshape(1, -1), wa, wbt, wa2, b_gla_a.reshape(1, -1),
        g_moba_q.reshape(-1, 1), g_moba_k.reshape(-1, 1), inv_col)
    o_gla = _gla(gq, gk, gv, gg, la, g_gla_out.reshape(1, -1))
    o_moba, wo, wup, wdn, wpg = _moba(
        qt, kb, vt, kmean.reshape(MOBA_HEADS, s // MOBA_BLOCK, MOBA_HD),
        [w_out, w_up, w_down, w_ple_gate])
    return _out_mlp(
        h2, o_gla, o_moba, p2, wo, g_ffn.reshape(1, -1), wup, wdn, g_ple_gate.reshape(1, -1),
        wpg, w_ple_proj.astype(BF16), g_ple_emb.reshape(1, -1))


def kernel(x, p, positions, g_mix, w_in, w_gla_a2, b_gla_a, g_gla_out, g_moba_q, g_moba_k,
           w_out, g_ffn, w_up, w_down, g_ple_gate, w_ple_gate, w_ple_proj, g_ple_emb):
    depth = p.shape[0]
    batch = x.shape[0]
    outs = []
    for b in range(batch):
        h = x[b]
        pos = positions[b:b + 1]
        for i in range(depth):
            h = _layer(h, p[i, b], pos, g_mix[i], w_in[i], w_gla_a2[i], b_gla_a[i],
                       g_gla_out[i], g_moba_q[i], g_moba_k[i], w_out[i], g_ffn[i], w_up[i],
                       w_down[i], g_ple_gate[i], w_ple_gate[i], w_ple_proj[i], g_ple_emb[i])
        outs.append(h)
    return jnp.stack(outs, axis=0)
```

```python
import functools
import math

import numpy as np
import jax
import jax.numpy as jnp
from jax import lax
from jax.experimental import pallas as pl
from jax.experimental.pallas import tpu as pltpu

F32 = jnp.float32
BF16 = jnp.bfloat16
HIGHEST = lax.Precision.HIGHEST

D_MODEL = 1024
PLE_DIM = 256
GLA_HEADS = 4
GLA_DK = 64
GLA_DV = 128
GLA_LOWRANK = 16
GLA_TAU = 16.0
GLA_CHUNK = 64
MOBA_HEADS = 4
MOBA_HD = 128
MOBA_BLOCK = 256
MOBA_TOPK = 3
ROPE_THETA = 10000.0
D_FF = 4 * D_MODEL
EPS = 1e-6

GLA_QK_W = GLA_HEADS * GLA_DK
GLA_V_W = GLA_HEADS * GLA_DV
MOBA_W = MOBA_HEADS * MOBA_HD
GLA_IN_W = 2 * GLA_QK_W + 2 * GLA_V_W + GLA_LOWRANK
LANES = 128
BF16_SUBLANES = 16
GA_PAD = LANES
WA_W = 2 * GLA_QK_W + 2 * GLA_V_W + GA_PAD

VMEM_LIMIT_BYTES = 56 * 1024 * 1024

IN_TM = 1024
GLA_CHUNKS_PER_STEP = 4
GLA_TM = 1024
OUT_TM = 512
OUT_SUBTILES = 2
FF_CHUNK = 1024
MOBA_BLOCKS_PER_STEP = 2
MOBA_STEPS_PER_ITER = 2
MOBA_STEPS_PER_LONG_ITER = 4
MOBA_MATMUL_LAG = 3
MOBA_SCORE_SLOTS = 8

MOBA_QSCALE = (MOBA_HD ** -0.5) * math.log2(math.e)
NEG_INF = float("-inf")


def _rms(x, axis):
    return x * lax.rsqrt(jnp.mean(x * x, axis=axis, keepdims=True) + EPS)


def _log_sigmoid(a):
    return jnp.minimum(a, 0.0) - jnp.log(1.0 + jnp.exp(-jnp.abs(a)))


def _dot_nt(a, b):
    return lax.dot_general(a, b, (((1,), (1,)), ((), ())), preferred_element_type=F32)


def _dot_tn(a, b):
    return lax.dot_general(a, b, (((0,), (0,)), ((), ())), preferred_element_type=F32)


def _in_proj_kernel(x_ref, pos_ref, gmix_ref, wa_ref, wbt_ref, wa2_ref, ba_ref,
                    gq_ref, gk_ref, inv_ref,
                    q_out, k_out, v_out, gg_out, la_out,
                    qt_out, kb_out, vt_out, kmean_out):
    tm = x_ref.shape[0]
    sub = MOBA_BLOCK
    nsub = tm // sub
    half = MOBA_HD // 2
    wa2 = wa2_ref[...]
    wa2_hi = wa2.astype(BF16)
    wa2_lo = (wa2 - wa2_hi.astype(F32)).astype(BF16)

    def gate_logs(t, za):
        ga = za[:, 2 * GLA_QK_W + 2 * GLA_V_W:]
        ga_hi = ga.astype(BF16)
        ga_lo = (ga - ga_hi.astype(F32)).astype(BF16)
        a_pre = (jnp.dot(ga_hi, wa2_hi, preferred_element_type=F32)
                 + jnp.dot(ga_hi, wa2_lo, preferred_element_type=F32)
                 + jnp.dot(ga_lo, wa2_hi, preferred_element_type=F32)) + ba_ref[...]
        la_out[t * sub:(t + 1) * sub, :] = _log_sigmoid(a_pre) * (1.0 / GLA_TAU)

    us = [(_rms(x_ref[t * sub:(t + 1) * sub, :], -1) * gmix_ref[...]).astype(BF16)
          for t in range(nsub)]
    zts, zas = [], []
    for t in range(nsub):
        zts.append(_dot_nt(wbt_ref[...], us[t]))
        zas.append(jnp.dot(us[t], wa_ref[...], preferred_element_type=F32))
    for t in range(nsub):
        gate_logs(t, zas[t])

    for t in range(nsub):
        rows = slice(t * sub, (t + 1) * sub)

        zt = zts[t]
        ang = inv_ref[...] * pos_ref[:, rows].astype(F32)
        cos = jnp.cos(ang)
        sin = jnp.sin(ang)

        def norm_rope(a, g_col):
            a = _rms(a, 0) * g_col
            a1, a2 = a[:half], a[half:]
            return jnp.concatenate([a1 * cos - a2 * sin, a2 * cos + a1 * sin], axis=0)

        for h in range(MOBA_HEADS):
            r0 = h * MOBA_HD
            qr = norm_rope(zt[r0:r0 + MOBA_HD], gq_ref[...]) * MOBA_QSCALE
            kr = norm_rope(zt[MOBA_W + r0:MOBA_W + r0 + MOBA_HD], gk_ref[...])
            kblk = kr.T
            qt_out[h, t] = qr.astype(BF16)
            vt_out[h, t] = zt[2 * MOBA_W + r0:2 * MOBA_W + r0 + MOBA_HD].astype(BF16)
            kb_out[h, t] = kblk.astype(BF16)
            kmean_out[h, t] = jnp.mean(kblk, axis=0, keepdims=True)

        za = zas[t]
        q_out[rows, :] = za[:, 0:GLA_QK_W]
        k_out[rows, :] = za[:, GLA_QK_W:2 * GLA_QK_W]
        v_out[rows, :] = za[:, 2 * GLA_QK_W:2 * GLA_QK_W + GLA_V_W].astype(BF16)
        gg_out[rows, :] = za[:, 2 * GLA_QK_W + GLA_V_W:2 * GLA_QK_W + 2 * GLA_V_W].astype(BF16)


def _in_proj(x2, pos, g_mix, wa, wbt, wa2, ba, gq_col, gk_col, inv_col):
    s = x2.shape[0]
    tm = min(IN_TM, s)
    nb = s // MOBA_BLOCK
    nblk = tm // MOBA_BLOCK
    const = lambda shape: pl.BlockSpec(shape, lambda i: (0,) * len(shape),
                                       pipeline_mode=pl.Buffered(1))
    row = lambda w: pl.BlockSpec((tm, w), lambda i: (i, 0))
    hblk = lambda a, b: pl.BlockSpec((MOBA_HEADS, nblk, a, b), lambda i: (0, i, 0, 0))
    return pl.pallas_call(
        _in_proj_kernel,
        grid=(s // tm,),
        in_specs=[
            row(D_MODEL),
            pl.BlockSpec((1, tm), lambda i: (0, i)),
            const((1, D_MODEL)),
            const((D_MODEL, WA_W)),
            const((3 * MOBA_W, D_MODEL)),
            const((GA_PAD, GLA_QK_W)),
            const((1, GLA_QK_W)),
            const((MOBA_HD, 1)),
            const((MOBA_HD, 1)),
            const((MOBA_HD // 2, 1)),
        ],
        out_specs=[
            row(GLA_QK_W), row(GLA_QK_W), row(GLA_V_W), row(GLA_V_W), row(GLA_QK_W),
            hblk(MOBA_HD, MOBA_BLOCK), hblk(MOBA_BLOCK, MOBA_HD), hblk(MOBA_HD, MOBA_BLOCK),
            hblk(1, MOBA_HD),
        ],
        out_shape=[
            jax.ShapeDtypeStruct((s, GLA_QK_W), F32),
            jax.ShapeDtypeStruct((s, GLA_QK_W), F32),
            jax.ShapeDtypeStruct((s, GLA_V_W), BF16),
            jax.ShapeDtypeStruct((s, GLA_V_W), BF16),
            jax.ShapeDtypeStruct((s, GLA_QK_W), F32),
            jax.ShapeDtypeStruct((MOBA_HEADS, nb, MOBA_HD, MOBA_BLOCK), BF16),
            jax.ShapeDtypeStruct((MOBA_HEADS, nb, MOBA_BLOCK, MOBA_HD), BF16),
            jax.ShapeDtypeStruct((MOBA_HEADS, nb, MOBA_HD, MOBA_BLOCK), BF16),
            jax.ShapeDtypeStruct((MOBA_HEADS, nb, 1, MOBA_HD), F32),
        ],
        compiler_params=pltpu.CompilerParams(
            dimension_semantics=("arbitrary",), vmem_limit_bytes=VMEM_LIMIT_BYTES),
        name="in_proj",
    )(x2, pos, g_mix, wa, wbt, wa2, ba, gq_col, gk_col, inv_col)


def _gla_kernel(q_ref, k_ref, v_ref, gg_ref, la_ref, gout_ref, o_ref, st_ref):
    c = GLA_CHUNK
    hc = GLA_HEADS * c

    @pl.when(pl.program_id(0) == 0)
    def _():
        st_ref[...] = jnp.zeros_like(st_ref)

    r = lax.broadcasted_iota(jnp.int32, (hc, GLA_QK_W), 0)
    l = lax.broadcasted_iota(jnp.int32, (hc, GLA_QK_W), 1)
    same_head = (r // c) == (l // GLA_DK)
    causal = same_head & ((l % c) <= (r % c))
    ng = GLA_CHUNKS_PER_STEP
    gr = ng * c
    tr = lax.broadcasted_iota(jnp.int32, (gr, gr), 0)
    tc = lax.broadcasted_iota(jnp.int32, (gr, gr), 1)
    tri = ((tr // c == tc // c) & (tc <= tr)).astype(BF16)
    scale = GLA_DK ** -0.5

    def stack_heads(a, width):
        return jnp.concatenate([a[:, h * width:(h + 1) * width] for h in range(GLA_HEADS)], axis=0)

    def block_diag(a):
        return jnp.where(same_head, jnp.concatenate([a] * GLA_HEADS, axis=0), 0.0).astype(BF16)

    chunks = [slice(i * c, (i + 1) * c) for i in range(ng)]

    def cumsum_stage(g):
        la = la_ref[g * gr:(g + 1) * gr, :]
        hi = la.astype(BF16)
        r1 = la - hi.astype(F32)
        mid = r1.astype(BF16)
        lo = (r1 - mid.astype(F32)).astype(BF16)
        return (jnp.dot(tri, hi, preferred_element_type=F32)
                + jnp.dot(tri, mid, preferred_element_type=F32)
                + jnp.dot(tri, lo, preferred_element_type=F32))

    def score_stage(g, b):
        q = q_ref[g * gr:(g + 1) * gr, :]
        k = k_ref[g * gr:(g + 1) * gr, :]
        v = v_ref[g * gr:(g + 1) * gr, :]
        b_last = [b[(i + 1) * c - 1:(i + 1) * c, :] for i in range(ng)]
        qb = [block_diag(q[ch] * jnp.exp(b[ch]) * scale) for ch in chunks]
        kb = [block_diag(k[ch] * jnp.exp(-b[ch])) for ch in chunks]
        keb = [block_diag(k[ch] * jnp.exp(b_last[i] - b[ch])) for i, ch in enumerate(chunks)]
        vst = [stack_heads(v[ch], GLA_DV) for ch in chunks]
        att = [_dot_nt(qb[i], kb[i]) for i in range(ng)]
        dst = [_dot_tn(vst[i], keb[i]) for i in range(ng)]
        decay = [jnp.exp(bl) for bl in b_last]
        return qb, vst, att, dst, decay

    def output_stage(qb, vst, att, dst, decay):
        att = [jnp.where(causal, a, 0.0).astype(BF16) for a in att]
        intra = [jnp.dot(att[i], vst[i], preferred_element_type=F32) for i in range(ng)]
        st = st_ref[...]
        inter = []
        for i in range(ng):
            inter.append(_dot_nt(qb[i], st.astype(BF16)))
            st = st * decay[i] + dst[i]
        st_ref[...] = st
        return [intra[i] + inter[i] for i in range(ng)]

    def store_stage(g, outs):
        gg = gg_ref[g * gr:(g + 1) * gr, :]
        for i, ch in enumerate(chunks):
            o = _rms(outs[i], -1) * gout_ref[...]
            gate = stack_heads(gg[ch], GLA_DV).astype(F32)
            o = o * (gate * jax.nn.sigmoid(gate))
            for h in range(GLA_HEADS):
                o_ref[g * gr + i * c:g * gr + (i + 1) * c, h * GLA_DV:(h + 1) * GLA_DV] = (
                    o[h * c:(h + 1) * c].astype(o_ref.dtype))

    ngroups = q_ref.shape[0] // gr
    cums, scored, outs = {}, {}, {}
    for t in range(ngroups + 3):
        if t < ngroups:
            cums[t] = cumsum_stage(t)
        if 0 <= t - 1 < ngroups:
            scored[t - 1] = score_stage(t - 1, cums.pop(t - 1))
        if 0 <= t - 2 < ngroups:
            outs[t - 2] = output_stage(*scored.pop(t - 2))
        if 0 <= t - 3 < ngroups:
            store_stage(t - 3, outs.pop(t - 3))


def _gla(q, k, v, gg, la, g_out):
    s = q.shape[0]
    tm = min(GLA_TM, s)
    row = lambda w: pl.BlockSpec((tm, w), lambda i: (i, 0))
    return pl.pallas_call(
        _gla_kernel,
        grid=(s // tm,),
        in_specs=[row(GLA_QK_W), row(GLA_QK_W), row(GLA_V_W), row(GLA_V_W), row(GLA_QK_W),
                  pl.BlockSpec((1, GLA_DV), lambda i: (0, 0))],
        out_specs=row(GLA_V_W),
        out_shape=jax.ShapeDtypeStruct((s, GLA_V_W), BF16),
        scratch_shapes=[pltpu.VMEM((GLA_DV, GLA_QK_W), F32)],
        compiler_params=pltpu.CompilerParams(dimension_semantics=("arbitrary",)),
        name="gla",
    )(q, k, v, gg, la, g_out)


def _moba_kernel(n_cast, qt_ref, k_ref, vt_ref, kmean_ref, *refs):
    o_ref = refs[n_cast]
    bias_ref, acc_ref, s_ref = refs[2 * n_cast + 1:]
    qi = pl.program_id(0)
    heads, nb, hd = kmean_ref.shape
    tq = qt_ref.shape[2]
    kblk = k_ref.shape[2]
    grp = MOBA_BLOCKS_PER_STEP
    lag = MOBA_MATMUL_LAG
    nslots = s_ref.shape[0]

    def step_list(ngrp):
        steps = [(h, g) for g in range(ngrp) for h in range(heads)]
        assert len(steps) % nslots == 0 and nslots >= 2 * lag
        return steps

    def score_matmul(ngrp, base, n):
        steps = step_list(ngrp)
        span = grp * ngrp
        h, g = steps[n % len(steps)]
        blk0 = jnp.minimum(base + (n // len(steps)) * span, nb - span) + g * grp
        kg = k_ref[h, pl.ds(blk0, grp)].reshape(grp * kblk, -1)
        s_ref[n % nslots] = jnp.dot(kg, qt_ref[h], preferred_element_type=F32)

    pad = acc_ref.shape[1] - hd

    def values_t(blocks):
        keys = len(blocks) * kblk
        first_row = lax.broadcasted_iota(jnp.int32, (pad, keys), 0) == 0
        ones_rows = jnp.where(first_row, 1.0, 0.0).astype(BF16)
        return jnp.concatenate([jnp.concatenate(blocks, axis=1), ones_rows], axis=0)

    own = [jnp.dot(k_ref[h, qi], qt_ref[h], preferred_element_type=F32) for h in range(heads)]
    gates = []
    for h in range(heads):
        km = kmean_ref[h]
        hi = km.astype(BF16)
        r1 = km - hi.astype(F32)
        mid = r1.astype(BF16)
        lo = (r1 - mid.astype(F32)).astype(BF16)
        qt = qt_ref[h]
        gates.append(jnp.dot(hi, qt, preferred_element_type=F32)
                     + jnp.dot(mid, qt, preferred_element_type=F32)
                     + jnp.dot(lo, qt, preferred_element_type=F32))
    for n in range(lag):
        score_matmul(MOBA_STEPS_PER_ITER, 0, n)

    kk = lax.broadcasted_iota(jnp.int32, (kblk, tq), 0)
    qq = lax.broadcasted_iota(jnp.int32, (kblk, tq), 1)
    causal = kk <= qq
    stats, probs = [], []
    for h in range(heads):
        s = jnp.where(causal, own[h], NEG_INF)
        m = jnp.max(s, axis=0, keepdims=True)
        stats.append(m)
        probs.append(jnp.exp2(s - m).astype(BF16))
    for h in range(heads):
        acc_ref[h] = jnp.dot(values_t([vt_ref[h, qi]]), probs[h], preferred_element_type=F32)

    blk = lax.broadcasted_iota(jnp.int32, (nb, tq), 0)
    for h in range(heads):
        gate = jnp.where(blk < qi, gates[h], NEG_INF)
        sel = jnp.zeros((nb, tq), jnp.bool_)
        for _ in range(min(MOBA_TOPK, nb)):
            m = jnp.max(gate, axis=0, keepdims=True)
            first = jnp.min(jnp.where(gate == m, blk, nb), axis=0, keepdims=True)
            pick = (blk == first) & (m > NEG_INF)
            sel = sel | pick
            gate = jnp.where(pick, NEG_INF, gate)
        bias_ref[h] = jnp.where(sel, 0.0, NEG_INF)

    def past_blocks(ngrp, first_block):
        span = grp * ngrp

        def body(it, carry):
            base = pl.multiple_of(first_block + it * span, grp * MOBA_STEPS_PER_ITER)
            stats = list(carry)
            for n, (h, g) in enumerate(step_list(ngrp)):
                score_matmul(ngrp, base, n + lag)
                j0 = base + g * grp
                m = stats[h]
                s = [s_ref[n % nslots, u * kblk:(u + 1) * kblk] for u in range(grp)]
                bias = [bias_ref[h, pl.ds(j0 + u, 1), :] for u in range(grp)]
                m_new = m
                for u in range(grp):
                    m_new = jnp.maximum(m_new, jnp.max(s[u], axis=0, keepdims=True) + bias[u])
                alpha = jnp.exp2(m - m_new)
                pb = jnp.concatenate(
                    [jnp.exp2(s[u] + (bias[u] - m_new)).astype(BF16) for u in range(grp)], axis=0)
                vt = values_t([vt_ref[h, j0 + u] for u in range(grp)])
                acc_ref[h] = alpha * acc_ref[h] + jnp.dot(vt, pb, preferred_element_type=F32)
                stats[h] = m_new
            return tuple(stats)

        return body

    long_ngrp = MOBA_STEPS_PER_LONG_ITER
    short_span, long_span = grp * MOBA_STEPS_PER_ITER, grp * long_ngrp
    assert step_list(long_ngrp)[:lag] == step_list(MOBA_STEPS_PER_ITER)[:lag]
    n_long = qi // long_span
    n_short = (qi - n_long * long_span + short_span - 1) // short_span
    stats = lax.fori_loop(0, n_long, past_blocks(long_ngrp, 0), tuple(stats))
    lax.fori_loop(0, n_short, past_blocks(MOBA_STEPS_PER_ITER, n_long * long_span), stats)
    for h in range(heads):
        acc = acc_ref[h]
        o_ref[:, h * hd:(h + 1) * hd] = (acc[:hd] / acc[hd:hd + 1]).T.astype(o_ref.dtype)
    for w_ref, w_bf16_ref in zip(refs[:n_cast], refs[n_cast + 1:2 * n_cast + 1]):
        w_bf16_ref[...] = w_ref[...].astype(w_bf16_ref.dtype)


def _moba(qt, kb, vt, kmean, weights):
    heads, nb, hd, blk = qt.shape
    s = nb * blk
    assert nb % (MOBA_BLOCKS_PER_STEP * MOBA_STEPS_PER_ITER) == 0
    assert nb % (MOBA_BLOCKS_PER_STEP * MOBA_STEPS_PER_LONG_ITER) == 0
    assert all(w.shape[0] % (nb * BF16_SUBLANES) == 0 for w in weights)
    const = lambda shape: pl.BlockSpec(shape, lambda i: (0,) * len(shape),
                                       pipeline_mode=pl.Buffered(1))
    slab = lambda w: pl.BlockSpec((w.shape[0] // nb, w.shape[1]), lambda i: (i, 0))
    return pl.pallas_call(
        functools.partial(_moba_kernel, len(weights)),
        grid=(nb,),
        in_specs=[
            pl.BlockSpec((heads, None, hd, blk), lambda i: (0, i, 0, 0)),
            const((heads, nb, blk, hd)),
            const((heads, nb, hd, blk)),
            const((heads, nb, hd)),
        ] + [slab(w) for w in weights],
        out_specs=[pl.BlockSpec((blk, heads * hd), lambda i: (i, 0))] + [slab(w) for w in weights],
        out_shape=[jax.ShapeDtypeStruct((s, heads * hd), BF16)]
        + [jax.ShapeDtypeStruct(w.shape, BF16) for w in weights],
        scratch_shapes=[
            pltpu.VMEM((heads, nb, blk), F32),
            pltpu.VMEM((heads, hd + BF16_SUBLANES, blk), F32),
            pltpu.VMEM((MOBA_SCORE_SLOTS, MOBA_BLOCKS_PER_STEP * blk, blk), F32),
        ],
        compiler_params=pltpu.CompilerParams(
            dimension_semantics=("arbitrary",), vmem_limit_bytes=VMEM_LIMIT_BYTES),
        name="moba",
    )(qt, kb, vt, kmean, *weights)


def _out_mlp_kernel(x_ref, og_ref, om_ref, p_ref, wo_ref, gffn_ref, wup_ref, wdn_ref,
                    gpg_ref, wpg_ref, wpp_ref, gpe_ref, o_ref):
    nsub = OUT_SUBTILES
    sub = x_ref.shape[0] // nsub
    rows = [slice(t * sub, (t + 1) * sub) for t in range(nsub)]

    e = [jnp.dot(p_ref[r, :].astype(BF16), wpp_ref[...], preferred_element_type=F32) for r in rows]
    h = [x_ref[r, :]
         + jnp.dot(og_ref[r, :], wo_ref[:GLA_V_W, :], preferred_element_type=F32)
         + jnp.dot(om_ref[r, :], wo_ref[GLA_V_W:, :], preferred_element_type=F32) for r in rows]

    u = [(_rms(ht, -1) * gffn_ref[...]).astype(BF16) for ht in h]
    mlp = [None] * nsub
    for c0 in range(0, D_FF, FF_CHUNK):
        f = [jnp.dot(ut, wup_ref[:, c0:c0 + FF_CHUNK], preferred_element_type=F32) for ut in u]
        f = [jnp.square(jnp.maximum(ft, 0.0)).astype(BF16) for ft in f]
        d = [jnp.dot(ft, wdn_ref[c0:c0 + FF_CHUNK, :], preferred_element_type=F32) for ft in f]
        mlp = [dt if mt is None else mt + dt for mt, dt in zip(mlp, d)]
    h = [ht + mt for ht, mt in zip(h, mlp)]

    u = [(_rms(ht, -1) * gpg_ref[...]).astype(BF16) for ht in h]
    gate = [jax.nn.sigmoid(jnp.dot(ut, wpg_ref[...], preferred_element_type=F32)) for ut in u]
    for t, r in enumerate(rows):
        o_ref[r, :] = h[t] + gate[t] * (_rms(e[t], -1) * gpe_ref[...])


def _out_mlp(x2, og, om, p2, wo, g_ffn, wup, wdn, g_pg, wpg, wpp, g_pe):
    s = x2.shape[0]
    tm = min(OUT_TM, s)
    const = lambda shape: pl.BlockSpec(shape, lambda i: (0,) * len(shape),
                                       pipeline_mode=pl.Buffered(1))
    row = lambda w: pl.BlockSpec((tm, w), lambda i: (i, 0))
    return pl.pallas_call(
        _out_mlp_kernel,
        grid=(s // tm,),
        in_specs=[
            row(D_MODEL), row(GLA_V_W), row(MOBA_W), row(PLE_DIM),
            const((GLA_V_W + MOBA_W, D_MODEL)),
            const((1, D_MODEL)),
            const((D_MODEL, D_FF)),
            const((D_FF, D_MODEL)),
            const((1, D_MODEL)),
            const((D_MODEL, D_MODEL)),
            const((PLE_DIM, D_MODEL)),
            const((1, D_MODEL)),
        ],
        out_specs=row(D_MODEL),
        out_shape=jax.ShapeDtypeStruct((s, D_MODEL), F32),
        compiler_params=pltpu.CompilerParams(
            dimension_semantics=("arbitrary",), vmem_limit_bytes=VMEM_LIMIT_BYTES),
        name="out_mlp",
    )(x2, og, om, p2, wo, g_ffn, wup, wdn, g_pg, wpg, wpp, g_pe)


def _layer(h2, p2, pos, g_mix, w_in, w_gla_a2, b_gla_a, g_gla_out, g_moba_q, g_moba_k,
           w_out, g_ffn, w_up, w_down, g_ple_gate, w_ple_gate, w_ple_proj, g_ple_emb):
    s = h2.shape[0]
    assert s % MOBA_BLOCK == 0 and s % GLA_CHUNK == 0
    assert s % min(IN_TM, s) == 0 and s % min(GLA_TM, s) == 0 and s % min(OUT_TM, s) == 0
    assert min(GLA_TM, s) % (GLA_CHUNKS_PER_STEP * GLA_CHUNK) == 0

    wa = jnp.concatenate(
        [w_in[:, :GLA_IN_W], jnp.zeros((D_MODEL, GA_PAD - GLA_LOWRANK), w_in.dtype)],
        axis=1).astype(BF16)
    wbt = w_in[:, GLA_IN_W:].T.astype(BF16)
    wa2 = jnp.concatenate(
        [w_gla_a2, jnp.zeros((GA_PAD - GLA_LOWRANK, GLA_QK_W), w_gla_a2.dtype)], axis=0)
    half = MOBA_HD // 2
    inv_col = (1.0 / (ROPE_THETA ** (jnp.arange(half, dtype=F32) / half))).reshape(half, 1)

    gq, gk, gv, gg, la, qt, kb, vt, kmean = _in_proj(
        h2, pos, g_mix.reshape(1, -1), wa, wbt, wa2, b_gla_a.reshape(1, -1),
        g_moba_q.reshape(-1, 1), g_moba_k.reshape(-1, 1), inv_col)
    o_gla = _gla(gq, gk, gv, gg, la, g_gla_out.reshape(1, -1))
    o_moba, wo, wup, wdn, wpg = _moba(
        qt, kb, vt, kmean.reshape(MOBA_HEADS, s // MOBA_BLOCK, MOBA_HD),
        [w_out, w_up, w_down, w_ple_gate])
    return _out_mlp(
        h2, o_gla, o_moba, p2, wo, g_ffn.reshape(1, -1), wup, wdn, g_ple_gate.reshape(1, -1),
        wpg, w_ple_proj.astype(BF16), g_ple_emb.reshape(1, -1))


def kernel(x, p, positions, g_mix, w_in, w_gla_a2, b_gla_a, g_gla_out, g_moba_q, g_moba_k,
           w_out, g_ffn, w_up, w_down, g_ple_gate, w_ple_gate, w_ple_proj, g_ple_emb):
    depth = p.shape[0]
    batch = x.shape[0]
    outs = []
    for b in range(batch):
        h = x[b]
        pos = positions[b:b + 1]
        for i in range(depth):
            h = _layer(h, p[i, b], pos, g_mix[i], w_in[i], w_gla_a2[i], b_gla_a[i],
                       g_gla_out[i], g_moba_q[i], g_moba_k[i], w_out[i], g_ffn[i], w_up[i],
                       w_down[i], g_ple_gate[i], w_ple_gate[i], w_ple_proj[i], g_ple_emb[i])
        outs.append(h)
    return jnp.stack(outs, axis=0)
```

```python
import functools
import math

import jax
import jax.numpy as jnp
from jax import lax
from jax.experimental import pallas as pl
from jax.experimental.pallas import tpu as pltpu

F32 = jnp.float32
BF16 = jnp.bfloat16

D_MODEL = 1024
PLE_DIM = 256
GLA_HEADS = 4
GLA_DK = 64
GLA_DV = 128
GLA_LOWRANK = 16
GLA_TAU = 16.0
GLA_CHUNK = 64
MOBA_HEADS = 4
MOBA_HD = 128
MOBA_BLOCK = 256
MOBA_TOPK = 3
ROPE_THETA = 10000.0
D_FF = 4 * D_MODEL
EPS = 1e-6

GLA_QK_W = GLA_HEADS * GLA_DK
GLA_V_W = GLA_HEADS * GLA_DV
MOBA_W = MOBA_HEADS * MOBA_HD
GLA_IN_W = 2 * GLA_QK_W + 2 * GLA_V_W + GLA_LOWRANK
LANES = 128
BF16_SUBLANES = 16
GA_PAD = LANES
WA_W = 2 * GLA_QK_W + 2 * GLA_V_W + GA_PAD

VMEM_LIMIT_BYTES = 56 * 1024 * 1024

IN_TM = 1024
GLA_CHUNKS_PER_STEP = 4
GLA_TM = 1024
OUT_TM = 512
OUT_SUBTILES = 2
FF_CHUNK = 1024
MOBA_BLOCKS_PER_STEP = 2
MOBA_STEPS_PER_ITER = 2
MOBA_STEPS_PER_LONG_ITER = 4
MOBA_MATMUL_LAG = 3
MOBA_SCORE_SLOTS = 8

MOBA_QSCALE = (MOBA_HD ** -0.5) * math.log2(math.e)
NEG_INF = float("-inf")


def _rms(x, axis):
    return x * lax.rsqrt(jnp.mean(x * x, axis=axis, keepdims=True) + EPS)


def _log_sigmoid(a):
    return jnp.minimum(a, 0.0) - jnp.log(1.0 + jnp.exp(-jnp.abs(a)))


def _dot_nt(a, b):
    return lax.dot_general(a, b, (((1,), (1,)), ((), ())), preferred_element_type=F32)


def _dot_tn(a, b):
    return lax.dot_general(a, b, (((0,), (0,)), ((), ())), preferred_element_type=F32)


def _in_proj_kernel(x_ref, pos_ref, gmix_ref, wa_ref, wbt_ref, wa2_ref, ba_ref,
                    gq_ref, gk_ref, inv_ref,
                    q_out, k_out, v_out, gg_out, la_out,
                    qt_out, kb_out, vt_out, kmean_out):
    tm = x_ref.shape[0]
    sub = MOBA_BLOCK
    nsub = tm // sub
    half = MOBA_HD // 2
    wa2 = wa2_ref[...]
    wa2_hi = wa2.astype(BF16)
    wa2_lo = (wa2 - wa2_hi.astype(F32)).astype(BF16)

    def gate_logs(t, za):
        ga = za[:, 2 * GLA_QK_W + 2 * GLA_V_W:]
        ga_hi = ga.astype(BF16)
        ga_lo = (ga - ga_hi.astype(F32)).astype(BF16)
        a_pre = (jnp.dot(ga_hi, wa2_hi, preferred_element_type=F32)
                 + jnp.dot(ga_hi, wa2_lo, preferred_element_type=F32)
                 + jnp.dot(ga_lo, wa2_hi, preferred_element_type=F32)) + ba_ref[...]
        la_out[t * sub:(t + 1) * sub, :] = _log_sigmoid(a_pre) * (1.0 / GLA_TAU)

    us = [(_rms(x_ref[t * sub:(t + 1) * sub, :], -1) * gmix_ref[...]).astype(BF16)
          for t in range(nsub)]
    zts, zas = [], []
    for t in range(nsub):
        zts.append(_dot_nt(wbt_ref[...], us[t]))
        zas.append(jnp.dot(us[t], wa_ref[...], preferred_element_type=F32))
    for t in range(nsub):
        gate_logs(t, zas[t])

    for t in range(nsub):
        rows = slice(t * sub, (t + 1) * sub)

        zt = zts[t]
        ang = inv_ref[...] * pos_ref[:, rows].astype(F32)
        cos = jnp.cos(ang)
        sin = jnp.sin(ang)

        def norm_rope(a, g_col):
            a = _rms(a, 0) * g_col
            a1, a2 = a[:half], a[half:]
            return jnp.concatenate([a1 * cos - a2 * sin, a2 * cos + a1 * sin], axis=0)

        for h in range(MOBA_HEADS):
            r0 = h * MOBA_HD
            qr = norm_rope(zt[r0:r0 + MOBA_HD], gq_ref[...]) * MOBA_QSCALE
            kr = norm_rope(zt[MOBA_W + r0:MOBA_W + r0 + MOBA_HD], gk_ref[...])
            kblk = kr.T
            qt_out[h, t] = qr.astype(BF16)
            vt_out[h, t] = zt[2 * MOBA_W + r0:2 * MOBA_W + r0 + MOBA_HD].astype(BF16)
            kb_out[h, t] = kblk.astype(BF16)
            kmean_out[h, t] = jnp.mean(kblk, axis=0, keepdims=True)

        za = zas[t]
        q_out[rows, :] = za[:, 0:GLA_QK_W]
        k_out[rows, :] = za[:, GLA_QK_W:2 * GLA_QK_W]
        v_out[rows, :] = za[:, 2 * GLA_QK_W:2 * GLA_QK_W + GLA_V_W].astype(BF16)
        gg_out[rows, :] = za[:, 2 * GLA_QK_W + GLA_V_W:2 * GLA_QK_W + 2 * GLA_V_W].astype(BF16)


def _in_proj(x2, pos, g_mix, wa, wbt, wa2, ba, gq_col, gk_col, inv_col):
    s = x2.shape[0]
    tm = min(IN_TM, s)
    nb = s // MOBA_BLOCK
    nblk = tm // MOBA_BLOCK
    const = lambda shape: pl.BlockSpec(shape, lambda i: (0,) * len(shape),
                                       pipeline_mode=pl.Buffered(1))
    row = lambda w: pl.BlockSpec((tm, w), lambda i: (i, 0))
    hblk = lambda a, b: pl.BlockSpec((MOBA_HEADS, nblk, a, b), lambda i: (0, i, 0, 0))
    return pl.pallas_call(
        _in_proj_kernel,
        grid=(s // tm,),
        in_specs=[
            row(D_MODEL),
            pl.BlockSpec((1, tm), lambda i: (0, i)),
            const((1, D_MODEL)),
            const((D_MODEL, WA_W)),
            const((3 * MOBA_W, D_MODEL)),
            const((GA_PAD, GLA_QK_W)),
            const((1, GLA_QK_W)),
            const((MOBA_HD, 1)),
            const((MOBA_HD, 1)),
            const((MOBA_HD // 2, 1)),
        ],
        out_specs=[
            row(GLA_QK_W), row(GLA_QK_W), row(GLA_V_W), row(GLA_V_W), row(GLA_QK_W),
            hblk(MOBA_HD, MOBA_BLOCK), hblk(MOBA_BLOCK, MOBA_HD), hblk(MOBA_HD, MOBA_BLOCK),
            hblk(1, MOBA_HD),
        ],
        out_shape=[
            jax.ShapeDtypeStruct((s, GLA_QK_W), F32),
            jax.ShapeDtypeStruct((s, GLA_QK_W), F32),
            jax.ShapeDtypeStruct((s, GLA_V_W), BF16),
            jax.ShapeDtypeStruct((s, GLA_V_W), BF16),
            jax.ShapeDtypeStruct((s, GLA_QK_W), F32),
            jax.ShapeDtypeStruct((MOBA_HEADS, nb, MOBA_HD, MOBA_BLOCK), BF16),
            jax.ShapeDtypeStruct((MOBA_HEADS, nb, MOBA_BLOCK, MOBA_HD), BF16),
            jax.ShapeDtypeStruct((MOBA_HEADS, nb, MOBA_HD, MOBA_BLOCK), BF16),
            jax.ShapeDtypeStruct((MOBA_HEADS, nb, 1, MOBA_HD), F32),
        ],
        compiler_params=pltpu.CompilerParams(
            dimension_semantics=("arbitrary",), vmem_limit_bytes=VMEM_LIMIT_BYTES),
        name="in_proj",
    )(x2, pos, g_mix, wa, wbt, wa2, ba, gq_col, gk_col, inv_col)


def _gla_kernel(q_ref, k_ref, v_ref, gg_ref, la_ref, gout_ref, o_ref, st_ref):
    c = GLA_CHUNK
    hc = GLA_HEADS * c

    @pl.when(pl.program_id(0) == 0)
    def _():
        st_ref[...] = jnp.zeros_like(st_ref)

    r = lax.broadcasted_iota(jnp.int32, (hc, GLA_QK_W), 0)
    l = lax.broadcasted_iota(jnp.int32, (hc, GLA_QK_W), 1)
    same_head = (r // c) == (l // GLA_DK)
    causal = same_head & ((l % c) <= (r % c))
    ng = GLA_CHUNKS_PER_STEP
    gr = ng * c
    tr = lax.broadcasted_iota(jnp.int32, (gr, gr), 0)
    tc = lax.broadcasted_iota(jnp.int32, (gr, gr), 1)
    tri = ((tr // c == tc // c) & (tc <= tr)).astype(BF16)
    scale = GLA_DK ** -0.5

    def stack_heads(a, width):
        return jnp.concatenate([a[:, h * width:(h + 1) * width] for h in range(GLA_HEADS)], axis=0)

    def block_diag(a):
        return jnp.where(same_head, jnp.concatenate([a] * GLA_HEADS, axis=0), 0.0).astype(BF16)

    chunks = [slice(i * c, (i + 1) * c) for i in range(ng)]

    def cumsum_stage(g):
        la = la_ref[g * gr:(g + 1) * gr, :]
        hi = la.astype(BF16)
        r1 = la - hi.astype(F32)
        mid = r1.astype(BF16)
        lo = (r1 - mid.astype(F32)).astype(BF16)
        return (jnp.dot(tri, hi, preferred_element_type=F32)
                + jnp.dot(tri, mid, preferred_element_type=F32)
                + jnp.dot(tri, lo, preferred_element_type=F32))

    def score_stage(g, b):
        q = q_ref[g * gr:(g + 1) * gr, :]
        k = k_ref[g * gr:(g + 1) * gr, :]
        v = v_ref[g * gr:(g + 1) * gr, :]
        b_last = [b[(i + 1) * c - 1:(i + 1) * c, :] for i in range(ng)]
        qb = [block_diag(q[ch] * jnp.exp(b[ch]) * scale) for ch in chunks]
        kb = [block_diag(k[ch] * jnp.exp(-b[ch])) for ch in chunks]
        keb = [block_diag(k[ch] * jnp.exp(b_last[i] - b[ch])) for i, ch in enumerate(chunks)]
        vst = [stack_heads(v[ch], GLA_DV) for ch in chunks]
        att = [_dot_nt(qb[i], kb[i]) for i in range(ng)]
        dst = [_dot_tn(vst[i], keb[i]) for i in range(ng)]
        decay = [jnp.exp(bl) for bl in b_last]
        return qb, vst, att, dst, decay

    def output_stage(qb, vst, att, dst, decay):
        att = [jnp.where(causal, a, 0.0).astype(BF16) for a in att]
        intra = [jnp.dot(att[i], vst[i], preferred_element_type=F32) for i in range(ng)]
        st = st_ref[...]
        inter = []
        for i in range(ng):
            inter.append(_dot_nt(qb[i], st.astype(BF16)))
            st = st * decay[i] + dst[i]
        st_ref[...] = st
        return [intra[i] + inter[i] for i in range(ng)]

    def store_stage(g, outs):
        gg = gg_ref[g * gr:(g + 1) * gr, :]
        for i, ch in enumerate(chunks):
            o = _rms(outs[i], -1) * gout_ref[...]
            gate = stack_heads(gg[ch], GLA_DV).astype(F32)
            o = o * (gate * jax.nn.sigmoid(gate))
            for h in range(GLA_HEADS):
                o_ref[g * gr + i * c:g * gr + (i + 1) * c, h * GLA_DV:(h + 1) * GLA_DV] = (
                    o[h * c:(h + 1) * c].astype(o_ref.dtype))

    ngroups = q_ref.shape[0] // gr
    cums, scored, outs = {}, {}, {}
    for t in range(ngroups + 3):
        if t < ngroups:
            cums[t] = cumsum_stage(t)
        if 0 <= t - 1 < ngroups:
            scored[t - 1] = score_stage(t - 1, cums.pop(t - 1))
        if 0 <= t - 2 < ngroups:
            outs[t - 2] = output_stage(*scored.pop(t - 2))
        if 0 <= t - 3 < ngroups:
            store_stage(t - 3, outs.pop(t - 3))


def _gla(q, k, v, gg, la, g_out):
    s = q.shape[0]
    tm = min(GLA_TM, s)
    row = lambda w: pl.BlockSpec((tm, w), lambda i: (i, 0))
    return pl.pallas_call(
        _gla_kernel,
        grid=(s // tm,),
        in_specs=[row(GLA_QK_W), row(GLA_QK_W), row(GLA_V_W), row(GLA_V_W), row(GLA_QK_W),
                  pl.BlockSpec((1, GLA_DV), lambda i: (0, 0))],
        out_specs=row(GLA_V_W),
        out_shape=jax.ShapeDtypeStruct((s, GLA_V_W), BF16),
        scratch_shapes=[pltpu.VMEM((GLA_DV, GLA_QK_W), F32)],
        compiler_params=pltpu.CompilerParams(dimension_semantics=("arbitrary",)),
        name="gla",
    )(q, k, v, gg, la, g_out)


def _moba_kernel(n_cast, qt_ref, k_ref, vt_ref, kmean_ref, *refs):
    o_ref = refs[n_cast]
    bias_ref, acc_ref, s_ref = refs[2 * n_cast + 1:]
    qi = pl.program_id(0)
    heads, nb, hd = kmean_ref.shape
    tq = qt_ref.shape[2]
    kblk = k_ref.shape[2]
    grp = MOBA_BLOCKS_PER_STEP
    lag = MOBA_MATMUL_LAG
    nslots = s_ref.shape[0]

    def step_list(ngrp):
        steps = [(h, g) for g in range(ngrp) for h in range(heads)]
        assert len(steps) % nslots == 0 and nslots >= 2 * lag
        return steps

    def score_matmul(ngrp, base, n):
        steps = step_list(ngrp)
        span = grp * ngrp
        h, g = steps[n % len(steps)]
        blk0 = jnp.minimum(base + (n // len(steps)) * span, nb - span) + g * grp
        kg = k_ref[h, pl.ds(blk0, grp)].reshape(grp * kblk, -1)
        s_ref[n % nslots] = jnp.dot(kg, qt_ref[h], preferred_element_type=F32)

    def values_t(blocks):
        return jnp.concatenate(blocks, axis=1)

    own = [jnp.dot(k_ref[h, qi], qt_ref[h], preferred_element_type=F32) for h in range(heads)]
    gates = []
    for h in range(heads):
        km = kmean_ref[h]
        hi = km.astype(BF16)
        r1 = km - hi.astype(F32)
        mid = r1.astype(BF16)
        lo = (r1 - mid.astype(F32)).astype(BF16)
        qt = qt_ref[h]
        gates.append(jnp.dot(hi, qt, preferred_element_type=F32)
                     + jnp.dot(mid, qt, preferred_element_type=F32)
                     + jnp.dot(lo, qt, preferred_element_type=F32))
    for n in range(lag):
        score_matmul(MOBA_STEPS_PER_ITER, 0, n)

    kk = lax.broadcasted_iota(jnp.int32, (kblk, tq), 0)
    qq = lax.broadcasted_iota(jnp.int32, (kblk, tq), 1)
    causal = kk <= qq
    stats, probs = [], []
    for h in range(heads):
        s = jnp.where(causal, own[h], NEG_INF)
        m = jnp.max(s, axis=0, keepdims=True)
        p = jnp.exp2(s - m)
        stats += [m, jnp.sum(p, axis=0, keepdims=True)]
        probs.append(p.astype(BF16))
    for h in range(heads):
        acc_ref[h] = jnp.dot(values_t([vt_ref[h, qi]]), probs[h], preferred_element_type=F32)

    blk = lax.broadcasted_iota(jnp.int32, (nb, tq), 0)
    for h in range(heads):
        gate = jnp.where(blk < qi, gates[h], NEG_INF)
        sel = jnp.zeros((nb, tq), jnp.bool_)
        for _ in range(min(MOBA_TOPK, nb)):
            m = jnp.max(gate, axis=0, keepdims=True)
            first = jnp.min(jnp.where(gate == m, blk, nb), axis=0, keepdims=True)
            pick = (blk == first) & (m > NEG_INF)
            sel = sel | pick
            gate = jnp.where(pick, NEG_INF, gate)
        bias_ref[h] = jnp.where(sel, 0.0, NEG_INF)

    def past_blocks(ngrp, first_block):
        span = grp * ngrp

        def body(it, carry):
            base = pl.multiple_of(first_block + it * span, grp * MOBA_STEPS_PER_ITER)
            stats = list(carry)
            for n, (h, g) in enumerate(step_list(ngrp)):
                score_matmul(ngrp, base, n + lag)
                j0 = base + g * grp
                m, l = stats[2 * h], stats[2 * h + 1]
                s = [s_ref[n % nslots, u * kblk:(u + 1) * kblk] for u in range(grp)]
                bias = [bias_ref[h, pl.ds(j0 + u, 1), :] for u in range(grp)]
                m_new = m
                for u in range(grp):
                    m_new = jnp.maximum(m_new, jnp.max(s[u], axis=0, keepdims=True) + bias[u])
                alpha = jnp.exp2(m - m_new)
                p = [jnp.exp2(s[u] + (bias[u] - m_new)) for u in range(grp)]
                l = alpha * l
                for u in range(grp):
                    l = l + jnp.sum(p[u], axis=0, keepdims=True)
                pb = jnp.concatenate([pu.astype(BF16) for pu in p], axis=0)
                vt = values_t([vt_ref[h, j0 + u] for u in range(grp)])
                acc_ref[h] = alpha * acc_ref[h] + jnp.dot(vt, pb, preferred_element_type=F32)
                stats[2 * h], stats[2 * h + 1] = m_new, l
            return tuple(stats)

        return body

    long_ngrp = MOBA_STEPS_PER_LONG_ITER
    short_span, long_span = grp * MOBA_STEPS_PER_ITER, grp * long_ngrp
    assert step_list(long_ngrp)[:lag] == step_list(MOBA_STEPS_PER_ITER)[:lag]
    n_long = qi // long_span
    n_short = (qi - n_long * long_span + short_span - 1) // short_span
    stats = lax.fori_loop(0, n_long, past_blocks(long_ngrp, 0), tuple(stats))
    stats = lax.fori_loop(0, n_short, past_blocks(MOBA_STEPS_PER_ITER, n_long * long_span), stats)
    for h in range(heads):
        o_ref[:, h * hd:(h + 1) * hd] = (acc_ref[h] / stats[2 * h + 1]).T.astype(o_ref.dtype)
    for w_ref, w_bf16_ref in zip(refs[:n_cast], refs[n_cast + 1:2 * n_cast + 1]):
        w_bf16_ref[...] = w_ref[...].astype(w_bf16_ref.dtype)


def _moba(qt, kb, vt, kmean, weights):
    heads, nb, hd, blk = qt.shape
    s = nb * blk
    assert nb % (MOBA_BLOCKS_PER_STEP * MOBA_STEPS_PER_ITER) == 0
    assert nb % (MOBA_BLOCKS_PER_STEP * MOBA_STEPS_PER_LONG_ITER) == 0
    assert all(w.shape[0] % (nb * BF16_SUBLANES) == 0 for w in weights)
    const = lambda shape: pl.BlockSpec(shape, lambda i: (0,) * len(shape),
                                       pipeline_mode=pl.Buffered(1))
    slab = lambda w: pl.BlockSpec((w.shape[0] // nb, w.shape[1]), lambda i: (i, 0))
    return pl.pallas_call(
        functools.partial(_moba_kernel, len(weights)),
        grid=(nb,),
        in_specs=[
            pl.BlockSpec((heads, None, hd, blk), lambda i: (0, i, 0, 0)),
            const((heads, nb, blk, hd)),
            const((heads, nb, hd, blk)),
            const((heads, nb, hd)),
        ] + [slab(w) for w in weights],
        out_specs=[pl.BlockSpec((blk, heads * hd), lambda i: (i, 0))] + [slab(w) for w in weights],
        out_shape=[jax.ShapeDtypeStruct((s, heads * hd), BF16)]
        + [jax.ShapeDtypeStruct(w.shape, BF16) for w in weights],
        scratch_shapes=[
            pltpu.VMEM((heads, nb, blk), F32),
            pltpu.VMEM((heads, hd, blk), F32),
            pltpu.VMEM((MOBA_SCORE_SLOTS, MOBA_BLOCKS_PER_STEP * blk, blk), F32),
        ],
        compiler_params=pltpu.CompilerParams(
            dimension_semantics=("arbitrary",), vmem_limit_bytes=VMEM_LIMIT_BYTES),
        name="moba",
    )(qt, kb, vt, kmean, *weights)


def _out_mlp_kernel(x_ref, og_ref, om_ref, p_ref, wo_ref, gffn_ref, wup_ref, wdn_ref,
                    gpg_ref, wpg_ref, wpp_ref, gpe_ref, o_ref):
    nsub = OUT_SUBTILES
    sub = x_ref.shape[0] // nsub
    rows = [slice(t * sub, (t + 1) * sub) for t in range(nsub)]

    e = [jnp.dot(p_ref[r, :].astype(BF16), wpp_ref[...], preferred_element_type=F32) for r in rows]
    h = [x_ref[r, :]
         + jnp.dot(og_ref[r, :], wo_ref[:GLA_V_W, :], preferred_element_type=F32)
         + jnp.dot(om_ref[r, :], wo_ref[GLA_V_W:, :], preferred_element_type=F32) for r in rows]

    u = [(_rms(ht, -1) * gffn_ref[...]).astype(BF16) for ht in h]
    mlp = [None] * nsub
    for c0 in range(0, D_FF, FF_CHUNK):
        f = [jnp.dot(ut, wup_ref[:, c0:c0 + FF_CHUNK], preferred_element_type=F32) for ut in u]
        f = [jnp.square(jnp.maximum(ft, 0.0)).astype(BF16) for ft in f]
        d = [jnp.dot(ft, wdn_ref[c0:c0 + FF_CHUNK, :], preferred_element_type=F32) for ft in f]
        mlp = [dt if mt is None else mt + dt for mt, dt in zip(mlp, d)]
    h = [ht + mt for ht, mt in zip(h, mlp)]

    u = [(_rms(ht, -1) * gpg_ref[...]).astype(BF16) for ht in h]
    gate = [jax.nn.sigmoid(jnp.dot(ut, wpg_ref[...], preferred_element_type=F32)) for ut in u]
    for t, r in enumerate(rows):
        o_ref[r, :] = h[t] + gate[t] * (_rms(e[t], -1) * gpe_ref[...])


def _out_mlp(x2, og, om, p2, wo, g_ffn, wup, wdn, g_pg, wpg, wpp, g_pe):
    s = x2.shape[0]
    tm = min(OUT_TM, s)
    const = lambda shape: pl.BlockSpec(shape, lambda i: (0,) * len(shape),
                                       pipeline_mode=pl.Buffered(1))
    row = lambda w: pl.BlockSpec((tm, w), lambda i: (i, 0))
    return pl.pallas_call(
        _out_mlp_kernel,
        grid=(s // tm,),
        in_specs=[
            row(D_MODEL), row(GLA_V_W), row(MOBA_W), row(PLE_DIM),
            const((GLA_V_W + MOBA_W, D_MODEL)),
            const((1, D_MODEL)),
            const((D_MODEL, D_FF)),
            const((D_FF, D_MODEL)),
            const((1, D_MODEL)),
            const((D_MODEL, D_MODEL)),
            const((PLE_DIM, D_MODEL)),
            const((1, D_MODEL)),
        ],
        out_specs=row(D_MODEL),
        out_shape=jax.ShapeDtypeStruct((s, D_MODEL), F32),
        compiler_params=pltpu.CompilerParams(
            dimension_semantics=("arbitrary",), vmem_limit_bytes=VMEM_LIMIT_BYTES),
        name="out_mlp",
    )(x2, og, om, p2, wo, g_ffn, wup, wdn, g_pg, wpg, wpp, g_pe)


def _layer(h2, p2, pos, g_mix, w_in, w_gla_a2, b_gla_a, g_gla_out, g_moba_q, g_moba_k,
           w_out, g_ffn, w_up, w_down, g_ple_gate, w_ple_gate, w_ple_proj, g_ple_emb):
    s = h2.shape[0]
    assert s % MOBA_BLOCK == 0 and s % GLA_CHUNK == 0
    assert s % min(IN_TM, s) == 0 and s % min(GLA_TM, s) == 0 and s % min(OUT_TM, s) == 0
    assert min(GLA_TM, s) % (GLA_CHUNKS_PER_STEP * GLA_CHUNK) == 0

    wa = jnp.concatenate(
        [w_in[:, :GLA_IN_W], jnp.zeros((D_MODEL, GA_PAD - GLA_LOWRANK), w_in.dtype)],
        axis=1).astype(BF16)
    wbt = w_in[:, GLA_IN_W:].T.astype(BF16)
    wa2 = jnp.concatenate(
        [w_gla_a2, jnp.zeros((GA_PAD - GLA_LOWRANK, GLA_QK_W), w_gla_a2.dtype)], axis=0)
    half = MOBA_HD // 2
    inv_col = (1.0 / (ROPE_THETA ** (jnp.arange(half, dtype=F32) / half))).reshape(half, 1)

    gq, gk, gv, gg, la, qt, kb, vt, kmean = _in_proj(
        h2, pos, g_mix.reshape(1, -1), wa, wbt, wa2, b_gla_a.reshape(1, -1),
        g_moba_q.reshape(-1, 1), g_moba_k.reshape(-1, 1), inv_col)
    o_gla = _gla(gq, gk, gv, gg, la, g_gla_out.reshape(1, -1))
    o_moba, wo, wup, wdn, wpg = _moba(
        qt, kb, vt, kmean.reshape(MOBA_HEADS, s // MOBA_BLOCK, MOBA_HD),
        [w_out, w_up, w_down, w_ple_gate])
    return _out_mlp(
        h2, o_gla, o_moba, p2, wo, g_ffn.reshape(1, -1), wup, wdn, g_ple_gate.reshape(1, -1),
        wpg, w_ple_proj.astype(BF16), g_ple_emb.reshape(1, -1))


def kernel(x, p, positions, g_mix, w_in, w_gla_a2, b_gla_a, g_gla_out, g_moba_q, g_moba_k,
           w_out, g_ffn, w_up, w_down, g_ple_gate, w_ple_gate, w_ple_proj, g_ple_emb):
    depth = p.shape[0]
    batch = x.shape[0]
    outs = []
    for b in range(batch):
        h = x[b]
        pos = positions[b:b + 1]
        for i in range(depth):
            h = _layer(h, p[i, b], pos, g_mix[i], w_in[i], w_gla_a2[i], b_gla_a[i],
                       g_gla_out[i], g_moba_q[i], g_moba_k[i], w_out[i], g_ffn[i], w_up[i],
                       w_down[i], g_ple_gate[i], w_ple_gate[i], w_ple_proj[i], g_ple_emb[i])
        outs.append(h)
    return jnp.stack(outs, axis=0)
```

```python
import functools
import math

import jax
import jax.numpy as jnp
from jax import lax
from jax.experimental import pallas as pl
from jax.experimental.pallas import tpu as pltpu

F32 = jnp.float32
BF16 = jnp.bfloat16

D_MODEL = 1024
PLE_DIM = 256
GLA_HEADS = 4
GLA_DK = 64
GLA_DV = 128
GLA_LOWRANK = 16
GLA_TAU = 16.0
GLA_CHUNK = 64
MOBA_HEADS = 4
MOBA_HD = 128
MOBA_BLOCK = 256
MOBA_TOPK = 3
ROPE_THETA = 10000.0
D_FF = 4 * D_MODEL
EPS = 1e-6

GLA_QK_W = GLA_HEADS * GLA_DK
GLA_V_W = GLA_HEADS * GLA_DV
MOBA_W = MOBA_HEADS * MOBA_HD
GLA_IN_W = 2 * GLA_QK_W + 2 * GLA_V_W + GLA_LOWRANK
LANES = 128
BF16_SUBLANES = 16
GA_PAD = LANES
WA_W = 2 * GLA_QK_W + 2 * GLA_V_W + GA_PAD

VMEM_LIMIT_BYTES = 56 * 1024 * 1024

IN_TM = 1024
GLA_CHUNKS_PER_STEP = 4
GLA_TM = 2048
OUT_TM = 512
OUT_SUBTILES = 2
FF_CHUNK = 1024
MOBA_BLOCKS_PER_STEP = 2
MOBA_STEPS_PER_ITER = 2
MOBA_STEPS_PER_LONG_ITER = 4
MOBA_MATMUL_LAG = 3
MOBA_SCORE_SLOTS = 8

MOBA_QSCALE = (MOBA_HD ** -0.5) * math.log2(math.e)
NEG_INF = float("-inf")


def _rms(x, axis):
    return x * lax.rsqrt(jnp.mean(x * x, axis=axis, keepdims=True) + EPS)


def _log_sigmoid(a):
    return jnp.minimum(a, 0.0) - jnp.log(1.0 + jnp.exp(-jnp.abs(a)))


def _dot_nt(a, b):
    return lax.dot_general(a, b, (((1,), (1,)), ((), ())), preferred_element_type=F32)


def _dot_tn(a, b):
    return lax.dot_general(a, b, (((0,), (0,)), ((), ())), preferred_element_type=F32)


def _in_proj_kernel(x_ref, pos_ref, gmix_ref, wa_ref, wbt_ref, wa2_ref, ba_ref,
                    gq_ref, gk_ref, inv_ref,
                    q_out, k_out, v_out, gg_out, la_out,
                    qt_out, kb_out, vt_out, kmean_out):
    tm = x_ref.shape[0]
    sub = MOBA_BLOCK
    nsub = tm // sub
    half = MOBA_HD // 2
    wa2 = wa2_ref[...]
    wa2_hi = wa2.astype(BF16)
    wa2_lo = (wa2 - wa2_hi.astype(F32)).astype(BF16)

    def gate_logs(t, za):
        ga = za[:, 2 * GLA_QK_W + 2 * GLA_V_W:]
        ga_hi = ga.astype(BF16)
        ga_lo = (ga - ga_hi.astype(F32)).astype(BF16)
        a_pre = (jnp.dot(ga_hi, wa2_hi, preferred_element_type=F32)
                 + jnp.dot(ga_hi, wa2_lo, preferred_element_type=F32)
                 + jnp.dot(ga_lo, wa2_hi, preferred_element_type=F32)) + ba_ref[...]
        la_out[t * sub:(t + 1) * sub, :] = _log_sigmoid(a_pre) * (1.0 / GLA_TAU)

    us = [(_rms(x_ref[t * sub:(t + 1) * sub, :], -1) * gmix_ref[...]).astype(BF16)
          for t in range(nsub)]
    zts, zas = [], []
    for t in range(nsub):
        zts.append(_dot_nt(wbt_ref[...], us[t]))
        zas.append(jnp.dot(us[t], wa_ref[...], preferred_element_type=F32))
    for t in range(nsub):
        gate_logs(t, zas[t])

    for t in range(nsub):
        rows = slice(t * sub, (t + 1) * sub)

        zt = zts[t]
        ang = inv_ref[...] * pos_ref[:, rows].astype(F32)
        cos = jnp.cos(ang)
        sin = jnp.sin(ang)

        def norm_rope(a, g_col):
            a = _rms(a, 0) * g_col
            a1, a2 = a[:half], a[half:]
            return jnp.concatenate([a1 * cos - a2 * sin, a2 * cos + a1 * sin], axis=0)

        for h in range(MOBA_HEADS):
            r0 = h * MOBA_HD
            qr = norm_rope(zt[r0:r0 + MOBA_HD], gq_ref[...]) * MOBA_QSCALE
            kr = norm_rope(zt[MOBA_W + r0:MOBA_W + r0 + MOBA_HD], gk_ref[...])
            kblk = kr.T
            qt_out[h, t] = qr.astype(BF16)
            vt_out[h, t] = zt[2 * MOBA_W + r0:2 * MOBA_W + r0 + MOBA_HD].astype(BF16)
            kb_out[h, t] = kblk.astype(BF16)
            kmean_out[h, t] = jnp.mean(kblk, axis=0, keepdims=True)

        za = zas[t]
        q_out[rows, :] = za[:, 0:GLA_QK_W]
        k_out[rows, :] = za[:, GLA_QK_W:2 * GLA_QK_W]
        v_out[rows, :] = za[:, 2 * GLA_QK_W:2 * GLA_QK_W + GLA_V_W].astype(BF16)
        gg_out[rows, :] = za[:, 2 * GLA_QK_W + GLA_V_W:2 * GLA_QK_W + 2 * GLA_V_W].astype(BF16)


def _in_proj(x2, pos, g_mix, wa, wbt, wa2, ba, gq_col, gk_col, inv_col):
    s = x2.shape[0]
    tm = min(IN_TM, s)
    nb = s // MOBA_BLOCK
    nblk = tm // MOBA_BLOCK
    const = lambda shape: pl.BlockSpec(shape, lambda i: (0,) * len(shape),
                                       pipeline_mode=pl.Buffered(1))
    row = lambda w: pl.BlockSpec((tm, w), lambda i: (i, 0))
    hblk = lambda a, b: pl.BlockSpec((MOBA_HEADS, nblk, a, b), lambda i: (0, i, 0, 0))
    return pl.pallas_call(
        _in_proj_kernel,
        grid=(s // tm,),
        in_specs=[
            row(D_MODEL),
            pl.BlockSpec((1, tm), lambda i: (0, i)),
            const((1, D_MODEL)),
            const((D_MODEL, WA_W)),
            const((3 * MOBA_W, D_MODEL)),
            const((GA_PAD, GLA_QK_W)),
            const((1, GLA_QK_W)),
            const((MOBA_HD, 1)),
            const((MOBA_HD, 1)),
            const((MOBA_HD // 2, 1)),
        ],
        out_specs=[
            row(GLA_QK_W), row(GLA_QK_W), row(GLA_V_W), row(GLA_V_W), row(GLA_QK_W),
            hblk(MOBA_HD, MOBA_BLOCK), hblk(MOBA_BLOCK, MOBA_HD), hblk(MOBA_HD, MOBA_BLOCK),
            hblk(1, MOBA_HD),
        ],
        out_shape=[
            jax.ShapeDtypeStruct((s, GLA_QK_W), F32),
            jax.ShapeDtypeStruct((s, GLA_QK_W), F32),
            jax.ShapeDtypeStruct((s, GLA_V_W), BF16),
            jax.ShapeDtypeStruct((s, GLA_V_W), BF16),
            jax.ShapeDtypeStruct((s, GLA_QK_W), F32),
            jax.ShapeDtypeStruct((MOBA_HEADS, nb, MOBA_HD, MOBA_BLOCK), BF16),
            jax.ShapeDtypeStruct((MOBA_HEADS, nb, MOBA_BLOCK, MOBA_HD), BF16),
            jax.ShapeDtypeStruct((MOBA_HEADS, nb, MOBA_HD, MOBA_BLOCK), BF16),
            jax.ShapeDtypeStruct((MOBA_HEADS, nb, 1, MOBA_HD), F32),
        ],
        compiler_params=pltpu.CompilerParams(
            dimension_semantics=("arbitrary",), vmem_limit_bytes=VMEM_LIMIT_BYTES),
        name="in_proj",
    )(x2, pos, g_mix, wa, wbt, wa2, ba, gq_col, gk_col, inv_col)


def _gla_kernel(q_ref, k_ref, v_ref, gg_ref, la_ref, gout_ref, o_ref, st_ref):
    c = GLA_CHUNK
    hc = GLA_HEADS * c

    @pl.when(pl.program_id(0) == 0)
    def _():
        st_ref[...] = jnp.zeros_like(st_ref)

    r = lax.broadcasted_iota(jnp.int32, (hc, GLA_QK_W), 0)
    l = lax.broadcasted_iota(jnp.int32, (hc, GLA_QK_W), 1)
    same_head = (r // c) == (l // GLA_DK)
    causal = same_head & ((l % c) <= (r % c))
    ng = GLA_CHUNKS_PER_STEP
    gr = ng * c
    tr = lax.broadcasted_iota(jnp.int32, (gr, gr), 0)
    tc = lax.broadcasted_iota(jnp.int32, (gr, gr), 1)
    tri = ((tr // c == tc // c) & (tc <= tr)).astype(BF16)
    scale = GLA_DK ** -0.5

    def stack_heads(a, width):
        return jnp.concatenate([a[:, h * width:(h + 1) * width] for h in range(GLA_HEADS)], axis=0)

    def block_diag(a):
        return jnp.where(same_head, jnp.concatenate([a] * GLA_HEADS, axis=0), 0.0).astype(BF16)

    chunks = [slice(i * c, (i + 1) * c) for i in range(ng)]

    def cumsum_stage(g):
        la = la_ref[g * gr:(g + 1) * gr, :]
        hi = la.astype(BF16)
        r1 = la - hi.astype(F32)
        mid = r1.astype(BF16)
        lo = (r1 - mid.astype(F32)).astype(BF16)
        return (jnp.dot(tri, hi, preferred_element_type=F32)
                + jnp.dot(tri, mid, preferred_element_type=F32)
                + jnp.dot(tri, lo, preferred_element_type=F32))

    def score_stage(g, b):
        q = q_ref[g * gr:(g + 1) * gr, :]
        k = k_ref[g * gr:(g + 1) * gr, :]
        v = v_ref[g * gr:(g + 1) * gr, :]
        b_last = [b[(i + 1) * c - 1:(i + 1) * c, :] for i in range(ng)]
        qb = [block_diag(q[ch] * jnp.exp(b[ch]) * scale) for ch in chunks]
        kb = [block_diag(k[ch] * jnp.exp(-b[ch])) for ch in chunks]
        keb = [block_diag(k[ch] * jnp.exp(b_last[i] - b[ch])) for i, ch in enumerate(chunks)]
        vst = [stack_heads(v[ch], GLA_DV) for ch in chunks]
        att = [_dot_nt(qb[i], kb[i]) for i in range(ng)]
        dst = [_dot_tn(vst[i], keb[i]) for i in range(ng)]
        decay = [jnp.exp(bl) for bl in b_last]
        return qb, vst, att, dst, decay

    def output_stage(qb, vst, att, dst, decay):
        att = [jnp.where(causal, a, 0.0).astype(BF16) for a in att]
        intra = [jnp.dot(att[i], vst[i], preferred_element_type=F32) for i in range(ng)]
        st = st_ref[...]
        inter = []
        for i in range(ng):
            inter.append(_dot_nt(qb[i], st.astype(BF16)))
            st = st * decay[i] + dst[i]
        st_ref[...] = st
        return [intra[i] + inter[i] for i in range(ng)]

    def store_stage(g, outs):
        gg = gg_ref[g * gr:(g + 1) * gr, :]
        for i, ch in enumerate(chunks):
            o = _rms(outs[i], -1) * gout_ref[...]
            gate = stack_heads(gg[ch], GLA_DV).astype(F32)
            o = o * (gate * jax.nn.sigmoid(gate))
            for h in range(GLA_HEADS):
                o_ref[g * gr + i * c:g * gr + (i + 1) * c, h * GLA_DV:(h + 1) * GLA_DV] = (
                    o[h * c:(h + 1) * c].astype(o_ref.dtype))

    ngroups = q_ref.shape[0] // gr
    cums, scored, outs = {}, {}, {}
    for t in range(ngroups + 3):
        if t < ngroups:
            cums[t] = cumsum_stage(t)
        if 0 <= t - 1 < ngroups:
            scored[t - 1] = score_stage(t - 1, cums.pop(t - 1))
        if 0 <= t - 2 < ngroups:
            outs[t - 2] = output_stage(*scored.pop(t - 2))
        if 0 <= t - 3 < ngroups:
            store_stage(t - 3, outs.pop(t - 3))


def _gla(q, k, v, gg, la, g_out):
    s = q.shape[0]
    tm = min(GLA_TM, s)
    row = lambda w: pl.BlockSpec((tm, w), lambda i: (i, 0))
    return pl.pallas_call(
        _gla_kernel,
        grid=(s // tm,),
        in_specs=[row(GLA_QK_W), row(GLA_QK_W), row(GLA_V_W), row(GLA_V_W), row(GLA_QK_W),
                  pl.BlockSpec((1, GLA_DV), lambda i: (0, 0))],
        out_specs=row(GLA_V_W),
        out_shape=jax.ShapeDtypeStruct((s, GLA_V_W), BF16),
        scratch_shapes=[pltpu.VMEM((GLA_DV, GLA_QK_W), F32)],
        compiler_params=pltpu.CompilerParams(dimension_semantics=("arbitrary",)),
        name="gla",
    )(q, k, v, gg, la, g_out)


def _moba_kernel(n_cast, qt_ref, k_ref, vt_ref, kmean_ref, *refs):
    o_ref = refs[n_cast]
    bias_ref, acc_ref, s_ref = refs[2 * n_cast + 1:]
    qi = pl.program_id(0)
    heads, nb, hd = kmean_ref.shape
    tq = qt_ref.shape[2]
    kblk = k_ref.shape[2]
    grp = MOBA_BLOCKS_PER_STEP
    lag = MOBA_MATMUL_LAG
    nslots = s_ref.shape[0]

    def step_list(ngrp):
        steps = [(h, g) for g in range(ngrp) for h in range(heads)]
        assert len(steps) % nslots == 0 and nslots >= 2 * lag
        return steps

    def score_matmul(ngrp, base, n):
        steps = step_list(ngrp)
        span = grp * ngrp
        h, g = steps[n % len(steps)]
        blk0 = jnp.minimum(base + (n // len(steps)) * span, nb - span) + g * grp
        kg = k_ref[h, pl.ds(blk0, grp)].reshape(grp * kblk, -1)
        s_ref[n % nslots] = jnp.dot(kg, qt_ref[h], preferred_element_type=F32)

    def values_t(blocks):
        return jnp.concatenate(blocks, axis=1)

    own = [jnp.dot(k_ref[h, qi], qt_ref[h], preferred_element_type=F32) for h in range(heads)]
    gates = []
    for h in range(heads):
        km = kmean_ref[h]
        hi = km.astype(BF16)
        r1 = km - hi.astype(F32)
        mid = r1.astype(BF16)
        lo = (r1 - mid.astype(F32)).astype(BF16)
        qt = qt_ref[h]
        gates.append(jnp.dot(hi, qt, preferred_element_type=F32)
                     + jnp.dot(mid, qt, preferred_element_type=F32)
                     + jnp.dot(lo, qt, preferred_element_type=F32))
    for n in range(lag):
        score_matmul(MOBA_STEPS_PER_ITER, 0, n)

    kk = lax.broadcasted_iota(jnp.int32, (kblk, tq), 0)
    qq = lax.broadcasted_iota(jnp.int32, (kblk, tq), 1)
    causal = kk <= qq
    stats, probs = [], []
    for h in range(heads):
        s = jnp.where(causal, own[h], NEG_INF)
        m = jnp.max(s, axis=0, keepdims=True)
        p = jnp.exp2(s - m)
        stats += [m, jnp.sum(p, axis=0, keepdims=True)]
        probs.append(p.astype(BF16))
    for h in range(heads):
        acc_ref[h] = jnp.dot(values_t([vt_ref[h, qi]]), probs[h], preferred_element_type=F32)

    blk = lax.broadcasted_iota(jnp.int32, (nb, tq), 0)
    for h in range(heads):
        gate = jnp.where(blk < qi, gates[h], NEG_INF)
        sel = jnp.zeros((nb, tq), jnp.bool_)
        for _ in range(min(MOBA_TOPK, nb)):
            m = jnp.max(gate, axis=0, keepdims=True)
            first = jnp.min(jnp.where(gate == m, blk, nb), axis=0, keepdims=True)
            pick = (blk == first) & (m > NEG_INF)
            sel = sel | pick
            gate = jnp.where(pick, NEG_INF, gate)
        bias_ref[h] = jnp.where(sel, 0.0, NEG_INF)

    def past_blocks(ngrp, first_block):
        span = grp * ngrp

        def body(it, carry):
            base = pl.multiple_of(first_block + it * span, grp * MOBA_STEPS_PER_ITER)
            stats = list(carry)
            for n, (h, g) in enumerate(step_list(ngrp)):
                score_matmul(ngrp, base, n + lag)
                j0 = base + g * grp
                m, l = stats[2 * h], stats[2 * h + 1]
                s = [s_ref[n % nslots, u * kblk:(u + 1) * kblk] for u in range(grp)]
                bias = [bias_ref[h, pl.ds(j0 + u, 1), :] for u in range(grp)]
                m_new = m
                for u in range(grp):
                    m_new = jnp.maximum(m_new, jnp.max(s[u], axis=0, keepdims=True) + bias[u])
                alpha = jnp.exp2(m - m_new)
                p = [jnp.exp2(s[u] + (bias[u] - m_new)) for u in range(grp)]
                l = alpha * l
                for u in range(grp):
                    l = l + jnp.sum(p[u], axis=0, keepdims=True)
                pb = jnp.concatenate([pu.astype(BF16) for pu in p], axis=0)
                vt = values_t([vt_ref[h, j0 + u] for u in range(grp)])
                acc_ref[h] = alpha * acc_ref[h] + jnp.dot(vt, pb, preferred_element_type=F32)
                stats[2 * h], stats[2 * h + 1] = m_new, l
            return tuple(stats)

        return body

    long_ngrp = MOBA_STEPS_PER_LONG_ITER
    short_span, long_span = grp * MOBA_STEPS_PER_ITER, grp * long_ngrp
    assert step_list(long_ngrp)[:lag] == step_list(MOBA_STEPS_PER_ITER)[:lag]
    n_long = qi // long_span
    n_short = (qi - n_long * long_span + short_span - 1) // short_span
    stats = lax.fori_loop(0, n_long, past_blocks(long_ngrp, 0), tuple(stats))
    stats = lax.fori_loop(0, n_short, past_blocks(MOBA_STEPS_PER_ITER, n_long * long_span), stats)
    for h in range(heads):
        o_ref[:, h * hd:(h + 1) * hd] = (acc_ref[h] / stats[2 * h + 1]).T.astype(o_ref.dtype)
    for w_ref, w_bf16_ref in zip(refs[:n_cast], refs[n_cast + 1:2 * n_cast + 1]):
        w_bf16_ref[...] = w_ref[...].astype(w_bf16_ref.dtype)


def _moba(qt, kb, vt, kmean, weights):
    heads, nb, hd, blk = qt.shape
    s = nb * blk
    assert nb % (MOBA_BLOCKS_PER_STEP * MOBA_STEPS_PER_ITER) == 0
    assert nb % (MOBA_BLOCKS_PER_STEP * MOBA_STEPS_PER_LONG_ITER) == 0
    assert all(w.shape[0] % (nb * BF16_SUBLANES) == 0 for w in weights)
    const = lambda shape: pl.BlockSpec(shape, lambda i: (0,) * len(shape),
                                       pipeline_mode=pl.Buffered(1))
    slab = lambda w: pl.BlockSpec((w.shape[0] // nb, w.shape[1]), lambda i: (i, 0))
    return pl.pallas_call(
        functools.partial(_moba_kernel, len(weights)),
        grid=(nb,),
        in_specs=[
            pl.BlockSpec((heads, None, hd, blk), lambda i: (0, i, 0, 0)),
            const((heads, nb, blk, hd)),
            const((heads, nb, hd, blk)),
            const((heads, nb, hd)),
        ] + [slab(w) for w in weights],
        out_specs=[pl.BlockSpec((blk, heads * hd), lambda i: (i, 0))] + [slab(w) for w in weights],
        out_shape=[jax.ShapeDtypeStruct((s, heads * hd), BF16)]
        + [jax.ShapeDtypeStruct(w.shape, BF16) for w in weights],
        scratch_shapes=[
            pltpu.VMEM((heads, nb, blk), F32),
            pltpu.VMEM((heads, hd, blk), F32),
            pltpu.VMEM((MOBA_SCORE_SLOTS, MOBA_BLOCKS_PER_STEP * blk, blk), F32),
        ],
        compiler_params=pltpu.CompilerParams(
            dimension_semantics=("arbitrary",), vmem_limit_bytes=VMEM_LIMIT_BYTES),
        name="moba",
    )(qt, kb, vt, kmean, *weights)


def _out_mlp_kernel(x_ref, og_ref, om_ref, p_ref, wo_ref, gffn_ref, wup_ref, wdn_ref,
                    gpg_ref, wpg_ref, wpp_ref, gpe_ref, o_ref):
    nsub = OUT_SUBTILES
    sub = x_ref.shape[0] // nsub
    rows = [slice(t * sub, (t + 1) * sub) for t in range(nsub)]

    e = [jnp.dot(p_ref[r, :].astype(BF16), wpp_ref[...], preferred_element_type=F32) for r in rows]
    h = [x_ref[r, :]
         + jnp.dot(og_ref[r, :], wo_ref[:GLA_V_W, :], preferred_element_type=F32)
         + jnp.dot(om_ref[r, :], wo_ref[GLA_V_W:, :], preferred_element_type=F32) for r in rows]

    u = [(_rms(ht, -1) * gffn_ref[...]).astype(BF16) for ht in h]
    mlp = [None] * nsub
    for c0 in range(0, D_FF, FF_CHUNK):
        f = [jnp.dot(ut, wup_ref[:, c0:c0 + FF_CHUNK], preferred_element_type=F32) for ut in u]
        f = [jnp.square(jnp.maximum(ft, 0.0)).astype(BF16) for ft in f]
        d = [jnp.dot(ft, wdn_ref[c0:c0 + FF_CHUNK, :], preferred_element_type=F32) for ft in f]
        mlp = [dt if mt is None else mt + dt for mt, dt in zip(mlp, d)]
    h = [ht + mt for ht, mt in zip(h, mlp)]

    u = [(_rms(ht, -1) * gpg_ref[...]).astype(BF16) for ht in h]
    gate = [jax.nn.sigmoid(jnp.dot(ut, wpg_ref[...], preferred_element_type=F32)) for ut in u]
    for t, r in enumerate(rows):
        o_ref[r, :] = h[t] + gate[t] * (_rms(e[t], -1) * gpe_ref[...])


def _out_mlp(x2, og, om, p2, wo, g_ffn, wup, wdn, g_pg, wpg, wpp, g_pe):
    s = x2.shape[0]
    tm = min(OUT_TM, s)
    const = lambda shape: pl.BlockSpec(shape, lambda i: (0,) * len(shape),
                                       pipeline_mode=pl.Buffered(1))
    row = lambda w: pl.BlockSpec((tm, w), lambda i: (i, 0))
    return pl.pallas_call(
        _out_mlp_kernel,
        grid=(s // tm,),
        in_specs=[
            row(D_MODEL), row(GLA_V_W), row(MOBA_W), row(PLE_DIM),
            const((GLA_V_W + MOBA_W, D_MODEL)),
            const((1, D_MODEL)),
            const((D_MODEL, D_FF)),
            const((D_FF, D_MODEL)),
            const((1, D_MODEL)),
            const((D_MODEL, D_MODEL)),
            const((PLE_DIM, D_MODEL)),
            const((1, D_MODEL)),
        ],
        out_specs=row(D_MODEL),
        out_shape=jax.ShapeDtypeStruct((s, D_MODEL), F32),
        compiler_params=pltpu.CompilerParams(
            dimension_semantics=("arbitrary",), vmem_limit_bytes=VMEM_LIMIT_BYTES),
        name="out_mlp",
    )(x2, og, om, p2, wo, g_ffn, wup, wdn, g_pg, wpg, wpp, g_pe)


def _layer(h2, p2, pos, g_mix, w_in, w_gla_a2, b_gla_a, g_gla_out, g_moba_q, g_moba_k,
           w_out, g_ffn, w_up, w_down, g_ple_gate, w_ple_gate, w_ple_proj, g_ple_emb):
    s = h2.shape[0]
    assert s % MOBA_BLOCK == 0 and s % GLA_CHUNK == 0
    assert s % min(IN_TM, s) == 0 and s % min(GLA_TM, s) == 0 and s % min(OUT_TM, s) == 0
    assert min(GLA_TM, s) % (GLA_CHUNKS_PER_STEP * GLA_CHUNK) == 0

    wa = jnp.concatenate(
        [w_in[:, :GLA_IN_W], jnp.zeros((D_MODEL, GA_PAD - GLA_LOWRANK), w_in.dtype)],
        axis=1).astype(BF16)
    wbt = w_in[:, GLA_IN_W:].T.astype(BF16)
    wa2 = jnp.concatenate(
        [w_gla_a2, jnp.zeros((GA_PAD - GLA_LOWRANK, GLA_QK_W), w_gla_a2.dtype)], axis=0)
    half = MOBA_HD // 2
    inv_col = (1.0 / (ROPE_THETA ** (jnp.arange(half, dtype=F32) / half))).reshape(half, 1)

    gq, gk, gv, gg, la, qt, kb, vt, kmean = _in_proj(
        h2, pos, g_mix.reshape(1, -1), wa, wbt, wa2, b_gla_a.reshape(1, -1),
        g_moba_q.reshape(-1, 1), g_moba_k.reshape(-1, 1), inv_col)
    o_gla = _gla(gq, gk, gv, gg, la, g_gla_out.reshape(1, -1))
    o_moba, wo, wup, wdn, wpg = _moba(
        qt, kb, vt, kmean.reshape(MOBA_HEADS, s // MOBA_BLOCK, MOBA_HD),
        [w_out, w_up, w_down, w_ple_gate])
    return _out_mlp(
        h2, o_gla, o_moba, p2, wo, g_ffn.reshape(1, -1), wup, wdn, g_ple_gate.reshape(1, -1),
        wpg, w_ple_proj.astype(BF16), g_ple_emb.reshape(1, -1))


def kernel(x, p, positions, g_mix, w_in, w_gla_a2, b_gla_a, g_gla_out, g_moba_q, g_moba_k,
           w_out, g_ffn, w_up, w_down, g_ple_gate, w_ple_gate, w_ple_proj, g_ple_emb):
    depth = p.shape[0]
    batch = x.shape[0]
    outs = []
    for b in range(batch):
        h = x[b]
        pos = positions[b:b + 1]
        for i in range(depth):
            h = _layer(h, p[i, b], pos, g_mix[i], w_in[i], w_gla_a2[i], b_gla_a[i],
                       g_gla_out[i], g_moba_q[i], g_moba_k[i], w_out[i], g_ffn[i], w_up[i],
                       w_down[i], g_ple_gate[i], w_ple_gate[i], w_ple_proj[i], g_ple_emb[i])
        outs.append(h)
    return jnp.stack(outs, axis=0)
```

```python
import functools
import math

import jax
import jax.numpy as jnp
from jax import lax
from jax.experimental import pallas as pl
from jax.experimental.pallas import tpu as pltpu

F32 = jnp.float32
BF16 = jnp.bfloat16

D_MODEL = 1024
PLE_DIM = 256
GLA_HEADS = 4
GLA_DK = 64
GLA_DV = 128
GLA_LOWRANK = 16
GLA_TAU = 16.0
GLA_CHUNK = 64
MOBA_HEADS = 4
MOBA_HD = 128
MOBA_BLOCK = 256
MOBA_TOPK = 3
ROPE_THETA = 10000.0
D_FF = 4 * D_MODEL
EPS = 1e-6

GLA_QK_W = GLA_HEADS * GLA_DK
GLA_V_W = GLA_HEADS * GLA_DV
MOBA_W = MOBA_HEADS * MOBA_HD
GLA_IN_W = 2 * GLA_QK_W + 2 * GLA_V_W + GLA_LOWRANK
LANES = 128
BF16_SUBLANES = 16
GA_PAD = LANES
WA_W = 2 * GLA_QK_W + 2 * GLA_V_W + GA_PAD

VMEM_LIMIT_BYTES = 56 * 1024 * 1024

IN_TM = 1024
GLA_CHUNKS_PER_STEP = 4
GLA_TM = 1024
OUT_TM = 512
OUT_SUBTILES = 2
FF_CHUNK = 1024
MOBA_QUERY_BLOCKS_PER_TILE = 2
MOBA_BLOCKS_PER_STEP = 2
MOBA_STEPS_PER_ITER = 2
MOBA_STEPS_PER_LONG_ITER = 4
MOBA_MATMUL_LAG = 3
MOBA_SCORE_SLOTS = 8

MOBA_QSCALE = (MOBA_HD ** -0.5) * math.log2(math.e)
NEG_INF = float("-inf")


def _rms(x, axis):
    return x * lax.rsqrt(jnp.mean(x * x, axis=axis, keepdims=True) + EPS)


def _log_sigmoid(a):
    return jnp.minimum(a, 0.0) - jnp.log(1.0 + jnp.exp(-jnp.abs(a)))


def _dot_nt(a, b):
    return lax.dot_general(a, b, (((1,), (1,)), ((), ())), preferred_element_type=F32)


def _dot_tn(a, b):
    return lax.dot_general(a, b, (((0,), (0,)), ((), ())), preferred_element_type=F32)


def _in_proj_kernel(x_ref, pos_ref, gmix_ref, wa_ref, wbt_ref, wa2_ref, ba_ref,
                    gq_ref, gk_ref, inv_ref,
                    q_out, k_out, v_out, gg_out, la_out,
                    qt_out, kb_out, vt_out, kmean_out):
    tm = x_ref.shape[0]
    sub = MOBA_BLOCK
    nsub = tm // sub
    half = MOBA_HD // 2
    wa2 = wa2_ref[...]
    wa2_hi = wa2.astype(BF16)
    wa2_lo = (wa2 - wa2_hi.astype(F32)).astype(BF16)

    def gate_logs(t, za):
        ga = za[:, 2 * GLA_QK_W + 2 * GLA_V_W:]
        ga_hi = ga.astype(BF16)
        ga_lo = (ga - ga_hi.astype(F32)).astype(BF16)
        a_pre = (jnp.dot(ga_hi, wa2_hi, preferred_element_type=F32)
                 + jnp.dot(ga_hi, wa2_lo, preferred_element_type=F32)
                 + jnp.dot(ga_lo, wa2_hi, preferred_element_type=F32)) + ba_ref[...]
        la_out[t * sub:(t + 1) * sub, :] = _log_sigmoid(a_pre) * (1.0 / GLA_TAU)

    us = [(_rms(x_ref[t * sub:(t + 1) * sub, :], -1) * gmix_ref[...]).astype(BF16)
          for t in range(nsub)]
    zts, zas = [], []
    for t in range(nsub):
        zts.append(_dot_nt(wbt_ref[...], us[t]))
        zas.append(jnp.dot(us[t], wa_ref[...], preferred_element_type=F32))
    for t in range(nsub):
        gate_logs(t, zas[t])

    for t in range(nsub):
        rows = slice(t * sub, (t + 1) * sub)

        zt = zts[t]
        ang = inv_ref[...] * pos_ref[:, rows].astype(F32)
        cos = jnp.cos(ang)
        sin = jnp.sin(ang)

        def norm_rope(a, g_col):
            a = _rms(a, 0) * g_col
            a1, a2 = a[:half], a[half:]
            return jnp.concatenate([a1 * cos - a2 * sin, a2 * cos + a1 * sin], axis=0)

        for h in range(MOBA_HEADS):
            r0 = h * MOBA_HD
            qr = norm_rope(zt[r0:r0 + MOBA_HD], gq_ref[...]) * MOBA_QSCALE
            kr = norm_rope(zt[MOBA_W + r0:MOBA_W + r0 + MOBA_HD], gk_ref[...])
            kblk = kr.T
            qt_out[h, t] = qr.astype(BF16)
            vt_out[h, t] = zt[2 * MOBA_W + r0:2 * MOBA_W + r0 + MOBA_HD].astype(BF16)
            kb_out[h, t] = kblk.astype(BF16)
            kmean_out[h, t] = jnp.mean(kblk, axis=0, keepdims=True)

        za = zas[t]
        q_out[rows, :] = za[:, 0:GLA_QK_W]
        k_out[rows, :] = za[:, GLA_QK_W:2 * GLA_QK_W]
        v_out[rows, :] = za[:, 2 * GLA_QK_W:2 * GLA_QK_W + GLA_V_W].astype(BF16)
        gg_out[rows, :] = za[:, 2 * GLA_QK_W + GLA_V_W:2 * GLA_QK_W + 2 * GLA_V_W].astype(BF16)


def _in_proj(x2, pos, g_mix, wa, wbt, wa2, ba, gq_col, gk_col, inv_col):
    s = x2.shape[0]
    tm = min(IN_TM, s)
    nb = s // MOBA_BLOCK
    nblk = tm // MOBA_BLOCK
    const = lambda shape: pl.BlockSpec(shape, lambda i: (0,) * len(shape),
                                       pipeline_mode=pl.Buffered(1))
    row = lambda w: pl.BlockSpec((tm, w), lambda i: (i, 0))
    hblk = lambda a, b: pl.BlockSpec((MOBA_HEADS, nblk, a, b), lambda i: (0, i, 0, 0))
    return pl.pallas_call(
        _in_proj_kernel,
        grid=(s // tm,),
        in_specs=[
            row(D_MODEL),
            pl.BlockSpec((1, tm), lambda i: (0, i)),
            const((1, D_MODEL)),
            const((D_MODEL, WA_W)),
            const((3 * MOBA_W, D_MODEL)),
            const((GA_PAD, GLA_QK_W)),
            const((1, GLA_QK_W)),
            const((MOBA_HD, 1)),
            const((MOBA_HD, 1)),
            const((MOBA_HD // 2, 1)),
        ],
        out_specs=[
            row(GLA_QK_W), row(GLA_QK_W), row(GLA_V_W), row(GLA_V_W), row(GLA_QK_W),
            hblk(MOBA_HD, MOBA_BLOCK), hblk(MOBA_BLOCK, MOBA_HD), hblk(MOBA_HD, MOBA_BLOCK),
            hblk(1, MOBA_HD),
        ],
        out_shape=[
            jax.ShapeDtypeStruct((s, GLA_QK_W), F32),
            jax.ShapeDtypeStruct((s, GLA_QK_W), F32),
            jax.ShapeDtypeStruct((s, GLA_V_W), BF16),
            jax.ShapeDtypeStruct((s, GLA_V_W), BF16),
            jax.ShapeDtypeStruct((s, GLA_QK_W), F32),
            jax.ShapeDtypeStruct((MOBA_HEADS, nb, MOBA_HD, MOBA_BLOCK), BF16),
            jax.ShapeDtypeStruct((MOBA_HEADS, nb, MOBA_BLOCK, MOBA_HD), BF16),
            jax.ShapeDtypeStruct((MOBA_HEADS, nb, MOBA_HD, MOBA_BLOCK), BF16),
            jax.ShapeDtypeStruct((MOBA_HEADS, nb, 1, MOBA_HD), F32),
        ],
        compiler_params=pltpu.CompilerParams(
            dimension_semantics=("arbitrary",), vmem_limit_bytes=VMEM_LIMIT_BYTES),
        name="in_proj",
    )(x2, pos, g_mix, wa, wbt, wa2, ba, gq_col, gk_col, inv_col)


def _gla_kernel(q_ref, k_ref, v_ref, gg_ref, la_ref, gout_ref, o_ref, st_ref):
    c = GLA_CHUNK
    hc = GLA_HEADS * c

    @pl.when(pl.program_id(0) == 0)
    def _():
        st_ref[...] = jnp.zeros_like(st_ref)

    r = lax.broadcasted_iota(jnp.int32, (hc, GLA_QK_W), 0)
    l = lax.broadcasted_iota(jnp.int32, (hc, GLA_QK_W), 1)
    same_head = (r // c) == (l // GLA_DK)
    causal = same_head & ((l % c) <= (r % c))
    ng = GLA_CHUNKS_PER_STEP
    gr = ng * c
    tr = lax.broadcasted_iota(jnp.int32, (gr, gr), 0)
    tc = lax.broadcasted_iota(jnp.int32, (gr, gr), 1)
    tri = ((tr // c == tc // c) & (tc <= tr)).astype(BF16)
    scale = GLA_DK ** -0.5

    def stack_heads(a, width):
        return jnp.concatenate([a[:, h * width:(h + 1) * width] for h in range(GLA_HEADS)], axis=0)

    def block_diag(a):
        return jnp.where(same_head, jnp.concatenate([a] * GLA_HEADS, axis=0), 0.0).astype(BF16)

    chunks = [slice(i * c, (i + 1) * c) for i in range(ng)]

    def cumsum_stage(g):
        la = la_ref[g * gr:(g + 1) * gr, :]
        hi = la.astype(BF16)
        r1 = la - hi.astype(F32)
        mid = r1.astype(BF16)
        lo = (r1 - mid.astype(F32)).astype(BF16)
        return (jnp.dot(tri, hi, preferred_element_type=F32)
                + jnp.dot(tri, mid, preferred_element_type=F32)
                + jnp.dot(tri, lo, preferred_element_type=F32))

    def score_stage(g, b):
        q = q_ref[g * gr:(g + 1) * gr, :]
        k = k_ref[g * gr:(g + 1) * gr, :]
        v = v_ref[g * gr:(g + 1) * gr, :]
        b_last = [b[(i + 1) * c - 1:(i + 1) * c, :] for i in range(ng)]
        qb = [block_diag(q[ch] * jnp.exp(b[ch]) * scale) for ch in chunks]
        kb = [block_diag(k[ch] * jnp.exp(-b[ch])) for ch in chunks]
        keb = [block_diag(k[ch] * jnp.exp(b_last[i] - b[ch])) for i, ch in enumerate(chunks)]
        vst = [stack_heads(v[ch], GLA_DV) for ch in chunks]
        att = [_dot_nt(qb[i], kb[i]) for i in range(ng)]
        dst = [_dot_tn(vst[i], keb[i]) for i in range(ng)]
        decay = [jnp.exp(bl) for bl in b_last]
        return qb, vst, att, dst, decay

    def output_stage(qb, vst, att, dst, decay):
        att = [jnp.where(causal, a, 0.0).astype(BF16) for a in att]
        intra = [jnp.dot(att[i], vst[i], preferred_element_type=F32) for i in range(ng)]
        st = st_ref[...]
        inter = []
        for i in range(ng):
            inter.append(_dot_nt(qb[i], st.astype(BF16)))
            st = st * decay[i] + dst[i]
        st_ref[...] = st
        return [intra[i] + inter[i] for i in range(ng)]

    def store_stage(g, outs):
        gg = gg_ref[g * gr:(g + 1) * gr, :]
        for i, ch in enumerate(chunks):
            o = _rms(outs[i], -1) * gout_ref[...]
            gate = stack_heads(gg[ch], GLA_DV).astype(F32)
            o = o * (gate * jax.nn.sigmoid(gate))
            for h in range(GLA_HEADS):
                o_ref[g * gr + i * c:g * gr + (i + 1) * c, h * GLA_DV:(h + 1) * GLA_DV] = (
                    o[h * c:(h + 1) * c].astype(o_ref.dtype))

    ngroups = q_ref.shape[0] // gr
    cums, scored, outs = {}, {}, {}
    for t in range(ngroups + 3):
        if t < ngroups:
            cums[t] = cumsum_stage(t)
        if 0 <= t - 1 < ngroups:
            scored[t - 1] = score_stage(t - 1, cums.pop(t - 1))
        if 0 <= t - 2 < ngroups:
            outs[t - 2] = output_stage(*scored.pop(t - 2))
        if 0 <= t - 3 < ngroups:
            store_stage(t - 3, outs.pop(t - 3))


def _gla(q, k, v, gg, la, g_out):
    s = q.shape[0]
    tm = min(GLA_TM, s)
    row = lambda w: pl.BlockSpec((tm, w), lambda i: (i, 0))
    return pl.pallas_call(
        _gla_kernel,
        grid=(s // tm,),
        in_specs=[row(GLA_QK_W), row(GLA_QK_W), row(GLA_V_W), row(GLA_V_W), row(GLA_QK_W),
                  pl.BlockSpec((1, GLA_DV), lambda i: (0, 0))],
        out_specs=row(GLA_V_W),
        out_shape=jax.ShapeDtypeStruct((s, GLA_V_W), BF16),
        scratch_shapes=[pltpu.VMEM((GLA_DV, GLA_QK_W), F32)],
        compiler_params=pltpu.CompilerParams(dimension_semantics=("arbitrary",)),
        name="gla",
    )(q, k, v, gg, la, g_out)


def _moba_kernel(n_cast, qt_ref, k_ref, vt_ref, kmean_ref, *refs):
    o_ref = refs[n_cast]
    bias_ref, acc_ref, s_ref = refs[2 * n_cast + 1:]
    heads, nb, hd = kmean_ref.shape
    qb = qt_ref.shape[1]
    kblk = k_ref.shape[2]
    tq = qb * kblk
    b0 = pl.program_id(0) * qb
    n_past = b0 + qb - 1
    grp = MOBA_BLOCKS_PER_STEP
    lag = MOBA_MATMUL_LAG
    nslots = s_ref.shape[0]

    def queries_t(h, blocks=range(qb)):
        return jnp.concatenate([qt_ref[h, t] for t in blocks], axis=1)

    def step_list(ngrp):
        steps = [(h, g) for g in range(ngrp) for h in range(heads)]
        assert len(steps) % nslots == 0 and nslots >= 2 * lag
        return steps

    def score_matmul(ngrp, base, n):
        steps = step_list(ngrp)
        span = grp * ngrp
        h, g = steps[n % len(steps)]
        blk0 = jnp.minimum(base + (n // len(steps)) * span, nb - span) + g * grp
        kg = k_ref[h, pl.ds(blk0, grp)].reshape(grp * kblk, -1)
        s_ref[n % nslots] = jnp.dot(kg, queries_t(h), preferred_element_type=F32)

    def values_t(blocks):
        return jnp.concatenate(blocks, axis=1)

    own = [[jnp.dot(k_ref[h, b0 + t], qt_ref[h, t], preferred_element_type=F32)
            for t in range(qb)] for h in range(heads)]
    gates = []
    for h in range(heads):
        km = kmean_ref[h]
        hi = km.astype(BF16)
        r1 = km - hi.astype(F32)
        mid = r1.astype(BF16)
        lo = (r1 - mid.astype(F32)).astype(BF16)
        qt = queries_t(h)
        gates.append(jnp.dot(hi, qt, preferred_element_type=F32)
                     + jnp.dot(mid, qt, preferred_element_type=F32)
                     + jnp.dot(lo, qt, preferred_element_type=F32))
    for n in range(lag):
        score_matmul(MOBA_STEPS_PER_ITER, 0, n)

    kk = lax.broadcasted_iota(jnp.int32, (kblk, kblk), 0)
    qq = lax.broadcasted_iota(jnp.int32, (kblk, kblk), 1)
    causal = kk <= qq
    stats, probs = [], []
    for h in range(heads):
        ms, ls, ps = [], [], []
        for t in range(qb):
            s = jnp.where(causal, own[h][t], NEG_INF)
            m = jnp.max(s, axis=0, keepdims=True)
            p = jnp.exp2(s - m)
            ms.append(m)
            ls.append(jnp.sum(p, axis=0, keepdims=True))
            ps.append(p.astype(BF16))
        stats += [jnp.concatenate(ms, axis=1), jnp.concatenate(ls, axis=1)]
        probs.append(ps)
    for h in range(heads):
        acc_ref[h] = jnp.concatenate(
            [jnp.dot(vt_ref[h, b0 + t], probs[h][t], preferred_element_type=F32)
             for t in range(qb)], axis=1)

    blk = lax.broadcasted_iota(jnp.int32, (nb, tq), 0)
    own_blk = b0 + lax.broadcasted_iota(jnp.int32, (nb, tq), 1) // kblk
    for h in range(heads):
        gate = jnp.where(blk < own_blk, gates[h], NEG_INF)
        sel = jnp.zeros((nb, tq), jnp.bool_)
        for _ in range(min(MOBA_TOPK, nb)):
            m = jnp.max(gate, axis=0, keepdims=True)
            first = jnp.min(jnp.where(gate == m, blk, nb), axis=0, keepdims=True)
            pick = (blk == first) & (m > NEG_INF)
            sel = sel | pick
            gate = jnp.where(pick, NEG_INF, gate)
        bias_ref[h] = jnp.where(sel, 0.0, NEG_INF)

    def past_blocks(ngrp, first_block):
        span = grp * ngrp

        def body(it, carry):
            base = pl.multiple_of(first_block + it * span, grp * MOBA_STEPS_PER_ITER)
            stats = list(carry)
            for n, (h, g) in enumerate(step_list(ngrp)):
                score_matmul(ngrp, base, n + lag)
                j0 = base + g * grp
                m, l = stats[2 * h], stats[2 * h + 1]
                s = [s_ref[n % nslots, u * kblk:(u + 1) * kblk] for u in range(grp)]
                bias = [bias_ref[h, pl.ds(j0 + u, 1), :] for u in range(grp)]
                m_new = m
                for u in range(grp):
                    m_new = jnp.maximum(m_new, jnp.max(s[u], axis=0, keepdims=True) + bias[u])
                alpha = jnp.exp2(m - m_new)
                p = [jnp.exp2(s[u] + (bias[u] - m_new)) for u in range(grp)]
                l = alpha * l
                for u in range(grp):
                    l = l + jnp.sum(p[u], axis=0, keepdims=True)
                pb = jnp.concatenate([pu.astype(BF16) for pu in p], axis=0)
                vt = values_t([vt_ref[h, j0 + u] for u in range(grp)])
                acc_ref[h] = alpha * acc_ref[h] + jnp.dot(vt, pb, preferred_element_type=F32)
                stats[2 * h], stats[2 * h + 1] = m_new, l
            return tuple(stats)

        return body

    long_ngrp = MOBA_STEPS_PER_LONG_ITER
    short_span, long_span = grp * MOBA_STEPS_PER_ITER, grp * long_ngrp
    assert step_list(long_ngrp)[:lag] == step_list(MOBA_STEPS_PER_ITER)[:lag]
    n_long = n_past // long_span
    n_short = (n_past - n_long * long_span + short_span - 1) // short_span
    stats = lax.fori_loop(0, n_long, past_blocks(long_ngrp, 0), tuple(stats))
    stats = lax.fori_loop(0, n_short, past_blocks(MOBA_STEPS_PER_ITER, n_long * long_span), stats)
    for h in range(heads):
        o_ref[:, h * hd:(h + 1) * hd] = (acc_ref[h] / stats[2 * h + 1]).T.astype(o_ref.dtype)
    for w_ref, w_bf16_ref in zip(refs[:n_cast], refs[n_cast + 1:2 * n_cast + 1]):
        w_bf16_ref[...] = w_ref[...].astype(w_bf16_ref.dtype)


def _moba(qt, kb, vt, kmean, weights):
    heads, nb, hd, blk = qt.shape
    s = nb * blk
    qb = MOBA_QUERY_BLOCKS_PER_TILE
    tq = qb * blk
    steps = nb // qb
    assert nb % qb == 0
    assert nb % (MOBA_BLOCKS_PER_STEP * MOBA_STEPS_PER_ITER) == 0
    assert nb % (MOBA_BLOCKS_PER_STEP * MOBA_STEPS_PER_LONG_ITER) == 0
    assert all(w.shape[0] % (steps * BF16_SUBLANES) == 0 for w in weights)
    const = lambda shape: pl.BlockSpec(shape, lambda i: (0,) * len(shape),
                                       pipeline_mode=pl.Buffered(1))
    slab = lambda w: pl.BlockSpec((w.shape[0] // steps, w.shape[1]), lambda i: (i, 0))
    return pl.pallas_call(
        functools.partial(_moba_kernel, len(weights)),
        grid=(steps,),
        in_specs=[
            pl.BlockSpec((heads, qb, hd, blk), lambda i: (0, i, 0, 0)),
            const((heads, nb, blk, hd)),
            const((heads, nb, hd, blk)),
            const((heads, nb, hd)),
        ] + [slab(w) for w in weights],
        out_specs=[pl.BlockSpec((tq, heads * hd), lambda i: (i, 0))] + [slab(w) for w in weights],
        out_shape=[jax.ShapeDtypeStruct((s, heads * hd), BF16)]
        + [jax.ShapeDtypeStruct(w.shape, BF16) for w in weights],
        scratch_shapes=[
            pltpu.VMEM((heads, nb, tq), F32),
            pltpu.VMEM((heads, hd, tq), F32),
            pltpu.VMEM((MOBA_SCORE_SLOTS, MOBA_BLOCKS_PER_STEP * blk, tq), F32),
        ],
        compiler_params=pltpu.CompilerParams(
            dimension_semantics=("arbitrary",), vmem_limit_bytes=VMEM_LIMIT_BYTES),
        name="moba",
    )(qt, kb, vt, kmean, *weights)


def _out_mlp_kernel(x_ref, og_ref, om_ref, p_ref, wo_ref, gffn_ref, wup_ref, wdn_ref,
                    gpg_ref, wpg_ref, wpp_ref, gpe_ref, o_ref):
    nsub = OUT_SUBTILES
    sub = x_ref.shape[0] // nsub
    rows = [slice(t * sub, (t + 1) * sub) for t in range(nsub)]

    e = [jnp.dot(p_ref[r, :].astype(BF16), wpp_ref[...], preferred_element_type=F32) for r in rows]
    h = [x_ref[r, :]
         + jnp.dot(og_ref[r, :], wo_ref[:GLA_V_W, :], preferred_element_type=F32)
         + jnp.dot(om_ref[r, :], wo_ref[GLA_V_W:, :], preferred_element_type=F32) for r in rows]

    u = [(_rms(ht, -1) * gffn_ref[...]).astype(BF16) for ht in h]
    mlp = [None] * nsub
    for c0 in range(0, D_FF, FF_CHUNK):
        f = [jnp.dot(ut, wup_ref[:, c0:c0 + FF_CHUNK], preferred_element_type=F32) for ut in u]
        f = [jnp.square(jnp.maximum(ft, 0.0)).astype(BF16) for ft in f]
        d = [jnp.dot(ft, wdn_ref[c0:c0 + FF_CHUNK, :], preferred_element_type=F32) for ft in f]
        mlp = [dt if mt is None else mt + dt for mt, dt in zip(mlp, d)]
    h = [ht + mt for ht, mt in zip(h, mlp)]

    u = [(_rms(ht, -1) * gpg_ref[...]).astype(BF16) for ht in h]
    gate = [jax.nn.sigmoid(jnp.dot(ut, wpg_ref[...], preferred_element_type=F32)) for ut in u]
    for t, r in enumerate(rows):
        o_ref[r, :] = h[t] + gate[t] * (_rms(e[t], -1) * gpe_ref[...])


def _out_mlp(x2, og, om, p2, wo, g_ffn, wup, wdn, g_pg, wpg, wpp, g_pe):
    s = x2.shape[0]
    tm = min(OUT_TM, s)
    const = lambda shape: pl.BlockSpec(shape, lambda i: (0,) * len(shape),
                                       pipeline_mode=pl.Buffered(1))
    row = lambda w: pl.BlockSpec((tm, w), lambda i: (i, 0))
    return pl.pallas_call(
        _out_mlp_kernel,
        grid=(s // tm,),
        in_specs=[
            row(D_MODEL), row(GLA_V_W), row(MOBA_W), row(PLE_DIM),
            const((GLA_V_W + MOBA_W, D_MODEL)),
            const((1, D_MODEL)),
            const((D_MODEL, D_FF)),
            const((D_FF, D_MODEL)),
            const((1, D_MODEL)),
            const((D_MODEL, D_MODEL)),
            const((PLE_DIM, D_MODEL)),
            const((1, D_MODEL)),
        ],
        out_specs=row(D_MODEL),
        out_shape=jax.ShapeDtypeStruct((s, D_MODEL), F32),
        compiler_params=pltpu.CompilerParams(
            dimension_semantics=("arbitrary",), vmem_limit_bytes=VMEM_LIMIT_BYTES),
        name="out_mlp",
    )(x2, og, om, p2, wo, g_ffn, wup, wdn, g_pg, wpg, wpp, g_pe)


def _layer(h2, p2, pos, g_mix, w_in, w_gla_a2, b_gla_a, g_gla_out, g_moba_q, g_moba_k,
           w_out, g_ffn, w_up, w_down, g_ple_gate, w_ple_gate, w_ple_proj, g_ple_emb):
    s = h2.shape[0]
    assert s % MOBA_BLOCK == 0 and s % GLA_CHUNK == 0
    assert s % min(IN_TM, s) == 0 and s % min(GLA_TM, s) == 0 and s % min(OUT_TM, s) == 0
    assert min(GLA_TM, s) % (GLA_CHUNKS_PER_STEP * GLA_CHUNK) == 0

    wa = jnp.concatenate(
        [w_in[:, :GLA_IN_W], jnp.zeros((D_MODEL, GA_PAD - GLA_LOWRANK), w_in.dtype)],
        axis=1).astype(BF16)
    wbt = w_in[:, GLA_IN_W:].T.astype(BF16)
    wa2 = jnp.concatenate(
        [w_gla_a2, jnp.zeros((GA_PAD - GLA_LOWRANK, GLA_QK_W), w_gla_a2.dtype)], axis=0)
    half = MOBA_HD // 2
    inv_col = (1.0 / (ROPE_THETA ** (jnp.arange(half, dtype=F32) / half))).reshape(half, 1)

    gq, gk, gv, gg, la, qt, kb, vt, kmean = _in_proj(
        h2, pos, g_mix.reshape(1, -1), wa, wbt, wa2, b_gla_a.reshape(1, -1),
        g_moba_q.reshape(-1, 1), g_moba_k.reshape(-1, 1), inv_col)
    o_gla = _gla(gq, gk, gv, gg, la, g_gla_out.reshape(1, -1))
    o_moba, wo, wup, wdn, wpg = _moba(
        qt, kb, vt, kmean.reshape(MOBA_HEADS, s // MOBA_BLOCK, MOBA_HD),
        [w_out, w_up, w_down, w_ple_gate])
    return _out_mlp(
        h2, o_gla, o_moba, p2, wo, g_ffn.reshape(1, -1), wup, wdn, g_ple_gate.reshape(1, -1),
        wpg, w_ple_proj.astype(BF16), g_ple_emb.reshape(1, -1))


def kernel(x, p, positions, g_mix, w_in, w_gla_a2, b_gla_a, g_gla_out, g_moba_q, g_moba_k,
           w_out, g_ffn, w_up, w_down, g_ple_gate, w_ple_gate, w_ple_proj, g_ple_emb):
    depth = p.shape[0]
    batch = x.shape[0]
    outs = []
    for b in range(batch):
        h = x[b]
        pos = positions[b:b + 1]
        for i in range(depth):
            h = _layer(h, p[i, b], pos, g_mix[i], w_in[i], w_gla_a2[i], b_gla_a[i],
                       g_gla_out[i], g_moba_q[i], g_moba_k[i], w_out[i], g_ffn[i], w_up[i],
                       w_down[i], g_ple_gate[i], w_ple_gate[i], w_ple_proj[i], g_ple_emb[i])
        outs.append(h)
    return jnp.stack(outs, axis=0)
```

```python
import functools
import math

import jax
import jax.numpy as jnp
from jax import lax
from jax.experimental import pallas as pl
from jax.experimental.pallas import tpu as pltpu

F32 = jnp.float32
BF16 = jnp.bfloat16

D_MODEL = 1024
PLE_DIM = 256
GLA_HEADS = 4
GLA_DK = 64
GLA_DV = 128
GLA_LOWRANK = 16
GLA_TAU = 16.0
GLA_CHUNK = 64
MOBA_HEADS = 4
MOBA_HD = 128
MOBA_BLOCK = 256
MOBA_TOPK = 3
ROPE_THETA = 10000.0
D_FF = 4 * D_MODEL
EPS = 1e-6

GLA_QK_W = GLA_HEADS * GLA_DK
GLA_V_W = GLA_HEADS * GLA_DV
MOBA_W = MOBA_HEADS * MOBA_HD
GLA_IN_W = 2 * GLA_QK_W + 2 * GLA_V_W + GLA_LOWRANK
LANES = 128
BF16_SUBLANES = 16
GA_PAD = LANES
WA_W = 2 * GLA_QK_W + 2 * GLA_V_W + GA_PAD

VMEM_LIMIT_BYTES = 56 * 1024 * 1024

IN_TM = 1024
GLA_CHUNKS_PER_STEP = 4
GLA_TM = 1024
OUT_TM = 512
OUT_SUBTILES = 2
FF_CHUNK = 1024
MOBA_QUERY_BLOCKS_PER_TILE = 2
MOBA_BLOCKS_PER_STEP = 2
MOBA_STEPS_PER_ITER = 2
MOBA_STEPS_PER_LONG_ITER = 4
MOBA_MATMUL_LAG = 2
MOBA_SCORE_SLOTS = 4

MOBA_QSCALE = (MOBA_HD ** -0.5) * math.log2(math.e)
NEG_INF = float("-inf")


def _rms(x, axis):
    return x * lax.rsqrt(jnp.mean(x * x, axis=axis, keepdims=True) + EPS)


def _log_sigmoid(a):
    return jnp.minimum(a, 0.0) - jnp.log(1.0 + jnp.exp(-jnp.abs(a)))


def _dot_nt(a, b):
    return lax.dot_general(a, b, (((1,), (1,)), ((), ())), preferred_element_type=F32)


def _dot_tn(a, b):
    return lax.dot_general(a, b, (((0,), (0,)), ((), ())), preferred_element_type=F32)


def _in_proj_kernel(x_ref, pos_ref, gmix_ref, wa_ref, wbt_ref, wa2_ref, ba_ref,
                    gq_ref, gk_ref, inv_ref,
                    q_out, k_out, v_out, gg_out, la_out,
                    qt_out, kb_out, vt_out, kmean_out):
    tm = x_ref.shape[0]
    sub = MOBA_BLOCK
    nsub = tm // sub
    half = MOBA_HD // 2
    wa2 = wa2_ref[...]
    wa2_hi = wa2.astype(BF16)
    wa2_lo = (wa2 - wa2_hi.astype(F32)).astype(BF16)

    def gate_logs(t, za):
        ga = za[:, 2 * GLA_QK_W + 2 * GLA_V_W:]
        ga_hi = ga.astype(BF16)
        ga_lo = (ga - ga_hi.astype(F32)).astype(BF16)
        a_pre = (jnp.dot(ga_hi, wa2_hi, preferred_element_type=F32)
                 + jnp.dot(ga_hi, wa2_lo, preferred_element_type=F32)
                 + jnp.dot(ga_lo, wa2_hi, preferred_element_type=F32)) + ba_ref[...]
        la_out[t * sub:(t + 1) * sub, :] = _log_sigmoid(a_pre) * (1.0 / GLA_TAU)

    us = [(_rms(x_ref[t * sub:(t + 1) * sub, :], -1) * gmix_ref[...]).astype(BF16)
          for t in range(nsub)]
    zts, zas = [], []
    for t in range(nsub):
        zts.append(_dot_nt(wbt_ref[...], us[t]))
        zas.append(jnp.dot(us[t], wa_ref[...], preferred_element_type=F32))
    for t in range(nsub):
        gate_logs(t, zas[t])

    for t in range(nsub):
        rows = slice(t * sub, (t + 1) * sub)

        zt = zts[t]
        ang = inv_ref[...] * pos_ref[:, rows].astype(F32)
        cos = jnp.cos(ang)
        sin = jnp.sin(ang)

        def norm_rope(a, g_col):
            a = _rms(a, 0) * g_col
            a1, a2 = a[:half], a[half:]
            return jnp.concatenate([a1 * cos - a2 * sin, a2 * cos + a1 * sin], axis=0)

        for h in range(MOBA_HEADS):
            r0 = h * MOBA_HD
            qr = norm_rope(zt[r0:r0 + MOBA_HD], gq_ref[...]) * MOBA_QSCALE
            kr = norm_rope(zt[MOBA_W + r0:MOBA_W + r0 + MOBA_HD], gk_ref[...])
            kblk = kr.T
            qt_out[h, t] = qr.astype(BF16)
            vt_out[h, t] = zt[2 * MOBA_W + r0:2 * MOBA_W + r0 + MOBA_HD].astype(BF16)
            kb_out[h, t] = kblk.astype(BF16)
            kmean_out[h, t] = jnp.mean(kblk, axis=0, keepdims=True)

        za = zas[t]
        q_out[rows, :] = za[:, 0:GLA_QK_W]
        k_out[rows, :] = za[:, GLA_QK_W:2 * GLA_QK_W]
        v_out[rows, :] = za[:, 2 * GLA_QK_W:2 * GLA_QK_W + GLA_V_W].astype(BF16)
        gg_out[rows, :] = za[:, 2 * GLA_QK_W + GLA_V_W:2 * GLA_QK_W + 2 * GLA_V_W].astype(BF16)


def _in_proj(x2, pos, g_mix, wa, wbt, wa2, ba, gq_col, gk_col, inv_col):
    s = x2.shape[0]
    tm = min(IN_TM, s)
    nb = s // MOBA_BLOCK
    nblk = tm // MOBA_BLOCK
    const = lambda shape: pl.BlockSpec(shape, lambda i: (0,) * len(shape),
                                       pipeline_mode=pl.Buffered(1))
    row = lambda w: pl.BlockSpec((tm, w), lambda i: (i, 0))
    hblk = lambda a, b: pl.BlockSpec((MOBA_HEADS, nblk, a, b), lambda i: (0, i, 0, 0))
    return pl.pallas_call(
        _in_proj_kernel,
        grid=(s // tm,),
        in_specs=[
            row(D_MODEL),
            pl.BlockSpec((1, tm), lambda i: (0, i)),
            const((1, D_MODEL)),
            const((D_MODEL, WA_W)),
            const((3 * MOBA_W, D_MODEL)),
            const((GA_PAD, GLA_QK_W)),
            const((1, GLA_QK_W)),
            const((MOBA_HD, 1)),
            const((MOBA_HD, 1)),
            const((MOBA_HD // 2, 1)),
        ],
        out_specs=[
            row(GLA_QK_W), row(GLA_QK_W), row(GLA_V_W), row(GLA_V_W), row(GLA_QK_W),
            hblk(MOBA_HD, MOBA_BLOCK), hblk(MOBA_BLOCK, MOBA_HD), hblk(MOBA_HD, MOBA_BLOCK),
            hblk(1, MOBA_HD),
        ],
        out_shape=[
            jax.ShapeDtypeStruct((s, GLA_QK_W), F32),
            jax.ShapeDtypeStruct((s, GLA_QK_W), F32),
            jax.ShapeDtypeStruct((s, GLA_V_W), BF16),
            jax.ShapeDtypeStruct((s, GLA_V_W), BF16),
            jax.ShapeDtypeStruct((s, GLA_QK_W), F32),
            jax.ShapeDtypeStruct((MOBA_HEADS, nb, MOBA_HD, MOBA_BLOCK), BF16),
            jax.ShapeDtypeStruct((MOBA_HEADS, nb, MOBA_BLOCK, MOBA_HD), BF16),
            jax.ShapeDtypeStruct((MOBA_HEADS, nb, MOBA_HD, MOBA_BLOCK), BF16),
            jax.ShapeDtypeStruct((MOBA_HEADS, nb, 1, MOBA_HD), F32),
        ],
        compiler_params=pltpu.CompilerParams(
            dimension_semantics=("arbitrary",), vmem_limit_bytes=VMEM_LIMIT_BYTES),
        name="in_proj",
    )(x2, pos, g_mix, wa, wbt, wa2, ba, gq_col, gk_col, inv_col)


def _gla_kernel(q_ref, k_ref, v_ref, gg_ref, la_ref, gout_ref, o_ref, st_ref):
    c = GLA_CHUNK
    hc = GLA_HEADS * c

    @pl.when(pl.program_id(0) == 0)
    def _():
        st_ref[...] = jnp.zeros_like(st_ref)

    r = lax.broadcasted_iota(jnp.int32, (hc, GLA_QK_W), 0)
    l = lax.broadcasted_iota(jnp.int32, (hc, GLA_QK_W), 1)
    same_head = (r // c) == (l // GLA_DK)
    causal = same_head & ((l % c) <= (r % c))
    ng = GLA_CHUNKS_PER_STEP
    gr = ng * c
    tr = lax.broadcasted_iota(jnp.int32, (gr, gr), 0)
    tc = lax.broadcasted_iota(jnp.int32, (gr, gr), 1)
    tri = ((tr // c == tc // c) & (tc <= tr)).astype(BF16)
    scale = GLA_DK ** -0.5

    def stack_heads(a, width):
        return jnp.concatenate([a[:, h * width:(h + 1) * width] for h in range(GLA_HEADS)], axis=0)

    def block_diag(a):
        return jnp.where(same_head, jnp.concatenate([a] * GLA_HEADS, axis=0), 0.0).astype(BF16)

    chunks = [slice(i * c, (i + 1) * c) for i in range(ng)]

    def cumsum_stage(g):
        la = la_ref[g * gr:(g + 1) * gr, :]
        hi = la.astype(BF16)
        r1 = la - hi.astype(F32)
        mid = r1.astype(BF16)
        lo = (r1 - mid.astype(F32)).astype(BF16)
        return (jnp.dot(tri, hi, preferred_element_type=F32)
                + jnp.dot(tri, mid, preferred_element_type=F32)
                + jnp.dot(tri, lo, preferred_element_type=F32))

    def score_stage(g, b):
        q = q_ref[g * gr:(g + 1) * gr, :]
        k = k_ref[g * gr:(g + 1) * gr, :]
        v = v_ref[g * gr:(g + 1) * gr, :]
        b_last = [b[(i + 1) * c - 1:(i + 1) * c, :] for i in range(ng)]
        qb = [block_diag(q[ch] * jnp.exp(b[ch]) * scale) for ch in chunks]
        kb = [block_diag(k[ch] * jnp.exp(-b[ch])) for ch in chunks]
        keb = [block_diag(k[ch] * jnp.exp(b_last[i] - b[ch])) for i, ch in enumerate(chunks)]
        vst = [stack_heads(v[ch], GLA_DV) for ch in chunks]
        att = [_dot_nt(qb[i], kb[i]) for i in range(ng)]
        dst = [_dot_tn(vst[i], keb[i]) for i in range(ng)]
        decay = [jnp.exp(bl) for bl in b_last]
        return qb, vst, att, dst, decay

    def output_stage(qb, vst, att, dst, decay):
        att = [jnp.where(causal, a, 0.0).astype(BF16) for a in att]
        intra = [jnp.dot(att[i], vst[i], preferred_element_type=F32) for i in range(ng)]
        st = st_ref[...]
        inter = []
        for i in range(ng):
            inter.append(_dot_nt(qb[i], st.astype(BF16)))
            st = st * decay[i] + dst[i]
        st_ref[...] = st
        return [intra[i] + inter[i] for i in range(ng)]

    def store_stage(g, outs):
        gg = gg_ref[g * gr:(g + 1) * gr, :]
        for i, ch in enumerate(chunks):
            o = _rms(outs[i], -1) * gout_ref[...]
            gate = stack_heads(gg[ch], GLA_DV).astype(F32)
            o = o * (gate * jax.nn.sigmoid(gate))
            for h in range(GLA_HEADS):
                o_ref[g * gr + i * c:g * gr + (i + 1) * c, h * GLA_DV:(h + 1) * GLA_DV] = (
                    o[h * c:(h + 1) * c].astype(o_ref.dtype))

    ngroups = q_ref.shape[0] // gr
    cums, scored, outs = {}, {}, {}
    for t in range(ngroups + 3):
        if t < ngroups:
            cums[t] = cumsum_stage(t)
        if 0 <= t - 1 < ngroups:
            scored[t - 1] = score_stage(t - 1, cums.pop(t - 1))
        if 0 <= t - 2 < ngroups:
            outs[t - 2] = output_stage(*scored.pop(t - 2))
        if 0 <= t - 3 < ngroups:
            store_stage(t - 3, outs.pop(t - 3))


def _gla(q, k, v, gg, la, g_out):
    s = q.shape[0]
    tm = min(GLA_TM, s)
    row = lambda w: pl.BlockSpec((tm, w), lambda i: (i, 0))
    return pl.pallas_call(
        _gla_kernel,
        grid=(s // tm,),
        in_specs=[row(GLA_QK_W), row(GLA_QK_W), row(GLA_V_W), row(GLA_V_W), row(GLA_QK_W),
                  pl.BlockSpec((1, GLA_DV), lambda i: (0, 0))],
        out_specs=row(GLA_V_W),
        out_shape=jax.ShapeDtypeStruct((s, GLA_V_W), BF16),
        scratch_shapes=[pltpu.VMEM((GLA_DV, GLA_QK_W), F32)],
        compiler_params=pltpu.CompilerParams(dimension_semantics=("arbitrary",)),
        name="gla",
    )(q, k, v, gg, la, g_out)


def _moba_kernel(n_cast, qt_ref, k_ref, vt_ref, kmean_ref, *refs):
    o_ref = refs[n_cast]
    bias_ref, acc_ref, s_ref = refs[2 * n_cast + 1:]
    heads, nb, hd = kmean_ref.shape
    qb = qt_ref.shape[1]
    kblk = k_ref.shape[2]
    tq = qb * kblk
    b0 = pl.program_id(0) * qb
    n_past = b0 + qb - 1
    grp = MOBA_BLOCKS_PER_STEP
    lag = MOBA_MATMUL_LAG
    nslots = s_ref.shape[0]

    def queries_t(h, blocks=range(qb)):
        return jnp.concatenate([qt_ref[h, t] for t in blocks], axis=1)

    def step_list(ngrp):
        steps = [(h, g) for g in range(ngrp) for h in range(heads)]
        assert len(steps) % nslots == 0 and nslots >= 2 * lag
        return steps

    def score_matmul(ngrp, base, n):
        steps = step_list(ngrp)
        span = grp * ngrp
        h, g = steps[n % len(steps)]
        blk0 = jnp.minimum(base + (n // len(steps)) * span, nb - span) + g * grp
        kg = k_ref[h, pl.ds(blk0, grp)].reshape(grp * kblk, -1)
        s_ref[n % nslots] = jnp.dot(kg, queries_t(h), preferred_element_type=F32)

    def values_t(blocks):
        return jnp.concatenate(blocks, axis=1)

    own = [[jnp.dot(k_ref[h, b0 + t], qt_ref[h, t], preferred_element_type=F32)
            for t in range(qb)] for h in range(heads)]
    gates = []
    for h in range(heads):
        km = kmean_ref[h]
        hi = km.astype(BF16)
        r1 = km - hi.astype(F32)
        mid = r1.astype(BF16)
        lo = (r1 - mid.astype(F32)).astype(BF16)
        qt = queries_t(h)
        gates.append(jnp.dot(hi, qt, preferred_element_type=F32)
                     + jnp.dot(mid, qt, preferred_element_type=F32)
                     + jnp.dot(lo, qt, preferred_element_type=F32))
    for n in range(lag):
        score_matmul(MOBA_STEPS_PER_ITER, 0, n)

    kk = lax.broadcasted_iota(jnp.int32, (kblk, kblk), 0)
    qq = lax.broadcasted_iota(jnp.int32, (kblk, kblk), 1)
    causal = kk <= qq
    stats, probs = [], []
    for h in range(heads):
        ms, ls, ps = [], [], []
        for t in range(qb):
            s = jnp.where(causal, own[h][t], NEG_INF)
            m = jnp.max(s, axis=0, keepdims=True)
            p = jnp.exp2(s - m)
            ms.append(m)
            ls.append(jnp.sum(p, axis=0, keepdims=True))
            ps.append(p.astype(BF16))
        stats += [jnp.concatenate(ms, axis=1), jnp.concatenate(ls, axis=1)]
        probs.append(ps)
    for h in range(heads):
        acc_ref[h] = jnp.concatenate(
            [jnp.dot(vt_ref[h, b0 + t], probs[h][t], preferred_element_type=F32)
             for t in range(qb)], axis=1)

    blk = lax.broadcasted_iota(jnp.int32, (nb, tq), 0)
    own_blk = b0 + lax.broadcasted_iota(jnp.int32, (nb, tq), 1) // kblk
    for h in range(heads):
        gate = jnp.where(blk < own_blk, gates[h], NEG_INF)
        sel = jnp.zeros((nb, tq), jnp.bool_)
        for _ in range(min(MOBA_TOPK, nb)):
            m = jnp.max(gate, axis=0, keepdims=True)
            first = jnp.min(jnp.where(gate == m, blk, nb), axis=0, keepdims=True)
            pick = (blk == first) & (m > NEG_INF)
            sel = sel | pick
            gate = jnp.where(pick, NEG_INF, gate)
        bias_ref[h] = jnp.where(sel, 0.0, NEG_INF)

    def past_blocks(ngrp, first_block):
        span = grp * ngrp

        def body(it, carry):
            base = pl.multiple_of(first_block + it * span, grp * MOBA_STEPS_PER_ITER)
            stats = list(carry)
            for n, (h, g) in enumerate(step_list(ngrp)):
                score_matmul(ngrp, base, n + lag)
                j0 = base + g * grp
                m, l = stats[2 * h], stats[2 * h + 1]
                s = [s_ref[n % nslots, u * kblk:(u + 1) * kblk] for u in range(grp)]
                bias = [bias_ref[h, pl.ds(j0 + u, 1), :] for u in range(grp)]
                m_new = m
                for u in range(grp):
                    m_new = jnp.maximum(m_new, jnp.max(s[u], axis=0, keepdims=True) + bias[u])
                alpha = jnp.exp2(m - m_new)
                p = [jnp.exp2(s[u] + (bias[u] - m_new)) for u in range(grp)]
                l = alpha * l
                for u in range(grp):
                    l = l + jnp.sum(p[u], axis=0, keepdims=True)
                pb = jnp.concatenate([pu.astype(BF16) for pu in p], axis=0)
                vt = values_t([vt_ref[h, j0 + u] for u in range(grp)])
                acc_ref[h] = alpha * acc_ref[h] + jnp.dot(vt, pb, preferred_element_type=F32)
                stats[2 * h], stats[2 * h + 1] = m_new, l
            return tuple(stats)

        return body

    long_ngrp = MOBA_STEPS_PER_LONG_ITER
    short_span, long_span = grp * MOBA_STEPS_PER_ITER, grp * long_ngrp
    assert step_list(long_ngrp)[:lag] == step_list(MOBA_STEPS_PER_ITER)[:lag]
    n_long = n_past // long_span
    n_short = (n_past - n_long * long_span + short_span - 1) // short_span
    stats = lax.fori_loop(0, n_long, past_blocks(long_ngrp, 0), tuple(stats))
    stats = lax.fori_loop(0, n_short, past_blocks(MOBA_STEPS_PER_ITER, n_long * long_span), stats)
    for h in range(heads):
        o_ref[:, h * hd:(h + 1) * hd] = (acc_ref[h] / stats[2 * h + 1]).T.astype(o_ref.dtype)
    for w_ref, w_bf16_ref in zip(refs[:n_cast], refs[n_cast + 1:2 * n_cast + 1]):
        w_bf16_ref[...] = w_ref[...].astype(w_bf16_ref.dtype)


def _moba(qt, kb, vt, kmean, weights):
    heads, nb, hd, blk = qt.shape
    s = nb * blk
    qb = MOBA_QUERY_BLOCKS_PER_TILE
    tq = qb * blk
    steps = nb // qb
    assert nb % qb == 0
    assert nb % (MOBA_BLOCKS_PER_STEP * MOBA_STEPS_PER_ITER) == 0
    assert nb % (MOBA_BLOCKS_PER_STEP * MOBA_STEPS_PER_LONG_ITER) == 0
    assert all(w.shape[0] % (steps * BF16_SUBLANES) == 0 for w in weights)
    const = lambda shape: pl.BlockSpec(shape, lambda i: (0,) * len(shape),
                                       pipeline_mode=pl.Buffered(1))
    slab = lambda w: pl.BlockSpec((w.shape[0] // steps, w.shape[1]), lambda i: (i, 0))
    return pl.pallas_call(
        functools.partial(_moba_kernel, len(weights)),
        grid=(steps,),
        in_specs=[
            pl.BlockSpec((heads, qb, hd, blk), lambda i: (0, i, 0, 0)),
            const((heads, nb, blk, hd)),
            const((heads, nb, hd, blk)),
            const((heads, nb, hd)),
        ] + [slab(w) for w in weights],
        out_specs=[pl.BlockSpec((tq, heads * hd), lambda i: (i, 0))] + [slab(w) for w in weights],
        out_shape=[jax.ShapeDtypeStruct((s, heads * hd), BF16)]
        + [jax.ShapeDtypeStruct(w.shape, BF16) for w in weights],
        scratch_shapes=[
            pltpu.VMEM((heads, nb, tq), F32),
            pltpu.VMEM((heads, hd, tq), F32),
            pltpu.VMEM((MOBA_SCORE_SLOTS, MOBA_BLOCKS_PER_STEP * blk, tq), F32),
        ],
        compiler_params=pltpu.CompilerParams(
            dimension_semantics=("arbitrary",), vmem_limit_bytes=VMEM_LIMIT_BYTES),
        name="moba",
    )(qt, kb, vt, kmean, *weights)


def _out_mlp_kernel(x_ref, og_ref, om_ref, p_ref, wo_ref, gffn_ref, wup_ref, wdn_ref,
                    gpg_ref, wpg_ref, wpp_ref, gpe_ref, o_ref):
    nsub = OUT_SUBTILES
    sub = x_ref.shape[0] // nsub
    rows = [slice(t * sub, (t + 1) * sub) for t in range(nsub)]

    e = [jnp.dot(p_ref[r, :].astype(BF16), wpp_ref[...], preferred_element_type=F32) for r in rows]
    h = [x_ref[r, :]
         + jnp.dot(og_ref[r, :], wo_ref[:GLA_V_W, :], preferred_element_type=F32)
         + jnp.dot(om_ref[r, :], wo_ref[GLA_V_W:, :], preferred_element_type=F32) for r in rows]

    u = [(_rms(ht, -1) * gffn_ref[...]).astype(BF16) for ht in h]
    mlp = [None] * nsub
    for c0 in range(0, D_FF, FF_CHUNK):
        f = [jnp.dot(ut, wup_ref[:, c0:c0 + FF_CHUNK], preferred_element_type=F32) for ut in u]
        f = [jnp.square(jnp.maximum(ft, 0.0)).astype(BF16) for ft in f]
        d = [jnp.dot(ft, wdn_ref[c0:c0 + FF_CHUNK, :], preferred_element_type=F32) for ft in f]
        mlp = [dt if mt is None else mt + dt for mt, dt in zip(mlp, d)]
    h = [ht + mt for ht, mt in zip(h, mlp)]

    u = [(_rms(ht, -1) * gpg_ref[...]).astype(BF16) for ht in h]
    gate = [jax.nn.sigmoid(jnp.dot(ut, wpg_ref[...], preferred_element_type=F32)) for ut in u]
    for t, r in enumerate(rows):
        o_ref[r, :] = h[t] + gate[t] * (_rms(e[t], -1) * gpe_ref[...])


def _out_mlp(x2, og, om, p2, wo, g_ffn, wup, wdn, g_pg, wpg, wpp, g_pe):
    s = x2.shape[0]
    tm = min(OUT_TM, s)
    const = lambda shape: pl.BlockSpec(shape, lambda i: (0,) * len(shape),
                                       pipeline_mode=pl.Buffered(1))
    row = lambda w: pl.BlockSpec((tm, w), lambda i: (i, 0))
    return pl.pallas_call(
        _out_mlp_kernel,
        grid=(s // tm,),
        in_specs=[
            row(D_MODEL), row(GLA_V_W), row(MOBA_W), row(PLE_DIM),
            const((GLA_V_W + MOBA_W, D_MODEL)),
            const((1, D_MODEL)),
            const((D_MODEL, D_FF)),
            const((D_FF, D_MODEL)),
            const((1, D_MODEL)),
            const((D_MODEL, D_MODEL)),
            const((PLE_DIM, D_MODEL)),
            const((1, D_MODEL)),
        ],
        out_specs=row(D_MODEL),
        out_shape=jax.ShapeDtypeStruct((s, D_MODEL), F32),
        compiler_params=pltpu.CompilerParams(
            dimension_semantics=("arbitrary",), vmem_limit_bytes=VMEM_LIMIT_BYTES),
        name="out_mlp",
    )(x2, og, om, p2, wo, g_ffn, wup, wdn, g_pg, wpg, wpp, g_pe)


def _layer(h2, p2, pos, g_mix, w_in, w_gla_a2, b_gla_a, g_gla_out, g_moba_q, g_moba_k,
           w_out, g_ffn, w_up, w_down, g_ple_gate, w_ple_gate, w_ple_proj, g_ple_emb):
    s = h2.shape[0]
    assert s % MOBA_BLOCK == 0 and s % GLA_CHUNK == 0
    assert s % min(IN_TM, s) == 0 and s % min(GLA_TM, s) == 0 and s % min(OUT_TM, s) == 0
    assert min(GLA_TM, s) % (GLA_CHUNKS_PER_STEP * GLA_CHUNK) == 0

    wa = jnp.concatenate(
        [w_in[:, :GLA_IN_W], jnp.zeros((D_MODEL, GA_PAD - GLA_LOWRANK), w_in.dtype)],
        axis=1).astype(BF16)
    wbt = w_in[:, GLA_IN_W:].T.astype(BF16)
    wa2 = jnp.concatenate(
        [w_gla_a2, jnp.zeros((GA_PAD - GLA_LOWRANK, GLA_QK_W), w_gla_a2.dtype)], axis=0)
    half = MOBA_HD // 2
    inv_col = (1.0 / (ROPE_THETA ** (jnp.arange(half, dtype=F32) / half))).reshape(half, 1)

    gq, gk, gv, gg, la, qt, kb, vt, kmean = _in_proj(
        h2, pos, g_mix.reshape(1, -1), wa, wbt, wa2, b_gla_a.reshape(1, -1),
        g_moba_q.reshape(-1, 1), g_moba_k.reshape(-1, 1), inv_col)
    o_gla = _gla(gq, gk, gv, gg, la, g_gla_out.reshape(1, -1))
    o_moba, wo, wup, wdn, wpg = _moba(
        qt, kb, vt, kmean.reshape(MOBA_HEADS, s // MOBA_BLOCK, MOBA_HD),
        [w_out, w_up, w_down, w_ple_gate])
    return _out_mlp(
        h2, o_gla, o_moba, p2, wo, g_ffn.reshape(1, -1), wup, wdn, g_ple_gate.reshape(1, -1),
        wpg, w_ple_proj.astype(BF16), g_ple_emb.reshape(1, -1))


def kernel(x, p, positions, g_mix, w_in, w_gla_a2, b_gla_a, g_gla_out, g_moba_q, g_moba_k,
           w_out, g_ffn, w_up, w_down, g_ple_gate, w_ple_gate, w_ple_proj, g_ple_emb):
    depth = p.shape[0]
    batch = x.shape[0]
    outs = []
    for b in range(batch):
        h = x[b]
        pos = positions[b:b + 1]
        for i in range(depth):
            h = _layer(h, p[i, b], pos, g_mix[i], w_in[i], w_gla_a2[i], b_gla_a[i],
                       g_gla_out[i], g_moba_q[i], g_moba_k[i], w_out[i], g_ffn[i], w_up[i],
                       w_down[i], g_ple_gate[i], w_ple_gate[i], w_ple_proj[i], g_ple_emb[i])
        outs.append(h)
    return jnp.stack(outs, axis=0)
```
